```python
import jax, jax.numpy as jnp
from jax import lax
import numpy as np

D_MODEL = 1024
BATCH = 8
SEQ = 2048
DEPTH = 2

GRID_W = 64
CTX_LEN = 256
D_MIX = D_MODEL
A_HEADS = 4
A_DH = 64
A_DV = 2 * A_DH
A_WIDTH = A_HEADS * A_DV
M_HEADS = 4
M_WIDTH = D_MIX - A_WIDTH
M_DH = M_WIDTH // M_HEADS
CONV_K = 5
CHUNK = 64
Q_BLOCK = 128
ROPE_BASE = 10000.0
ROPE_PAIRS_PER_AXIS = A_DH // 4
D_FF = 256 * ((8 * D_MODEL // 3 + 255) // 256)
N_EXPERTS = 8
TOP_K = 2
D_FF_EXPERT = 7 * D_MODEL // 2
N_DENSE = (DEPTH + 1) // 2
N_MOE = DEPTH // 2
EPS = 1e-6
N_GATES = 4 * M_HEADS
IN_COLS = 3 * A_WIDTH + 4 * M_WIDTH + N_GATES
SPLITS = (A_WIDTH, 2 * A_WIDTH, 3 * A_WIDTH, 3 * A_WIDTH + M_WIDTH, 3 * A_WIDTH + 2 * M_WIDTH,
          3 * A_WIDTH + 3 * M_WIDTH, 3 * A_WIDTH + 4 * M_WIDTH)

kernel_name = "hybrid_diffattn_mlstm_moe_dit"


def rms_norm(x, g):
    xf = x.astype(jnp.float32)
    y = xf * lax.rsqrt(jnp.mean(xf * xf, axis=-1, keepdims=True) + EPS)
    return (y * g.astype(jnp.float32)).astype(x.dtype)


def adaln(cvec, w, b):
    m = jax.nn.silu(cvec) @ w + b
    return jnp.split(m[..., None, :], 6, axis=-1)


def modulate(x, g, shift, scale):
    return rms_norm(x, g) * (1.0 + scale) + shift


def axial_rope_tables(n_tokens):
    rows = n_tokens // GRID_W
    row = jnp.repeat(jnp.arange(rows, dtype=jnp.float32), GRID_W)
    col = jnp.tile(jnp.arange(GRID_W, dtype=jnp.float32), rows)
    inv = ROPE_BASE ** (-jnp.arange(ROPE_PAIRS_PER_AXIS, dtype=jnp.float32) / ROPE_PAIRS_PER_AXIS)
    ang = jnp.concatenate([row[:, None] * inv, col[:, None] * inv], axis=-1)
    return jnp.cos(ang), jnp.sin(ang)


def apply_rope(x, cos, sin):
    c = cos[None, :, None, None, :].astype(x.dtype)
    s = sin[None, :, None, None, :].astype(x.dtype)
    x1, x2 = x[..., :A_DH // 2], x[..., A_DH // 2:]
    return jnp.concatenate([x1 * c - x2 * s, x1 * s + x2 * c], axis=-1)


def diff_heads(aq, ak, av, q_g, k_g):
    B, T, _ = aq.shape
    q = rms_norm(aq.reshape(B, T, A_HEADS, 2, A_DH), q_g)
    k = rms_norm(ak.reshape(B, T, A_HEADS, 2, A_DH), k_g)
    v = av.reshape(B, T, A_HEADS, A_DV)
    return q, k, v


def diff_attention(q, k, v, lam):
    s = jnp.einsum('bqhcd,bkhcd->bhcqk', q, k).astype(jnp.float32) * (A_DH ** -0.5)
    p = jax.nn.softmax(s, axis=-1)
    a = p[:, :, 0] - lam * p[:, :, 1]
    return jnp.einsum('bhqk,bkhv->bqhv', a.astype(v.dtype), v)


def blocked_latent_attention(q, k_all, v_all, lam):
    B, T = q.shape[:2]
    nb = T // Q_BLOCK
    qb = jnp.moveaxis(q.reshape(B, nb, Q_BLOCK, A_HEADS, 2, A_DH), 1, 0)

    def one_block(blk):
        return diff_attention(blk, k_all, v_all, lam)

    out = lax.map(one_block, qb)
    return jnp.moveaxis(out, 0, 1).reshape(B, T, A_HEADS, A_DV)


def diff_post(o, g, lam_init):
    B, T = o.shape[:2]
    return (rms_norm(o, g) * (1.0 - lam_init).astype(o.dtype)).reshape(B, T, A_WIDTH)


def centred_dwconv(x, w, b):
    pad = CONV_K // 2
    y = lax.conv_general_dilated(x, w[:, None, :].astype(x.dtype), window_strides=(1,),
                                 padding=[(pad, pad)], dimension_numbers=('NWC', 'WIO', 'NWC'),
                                 feature_group_count=x.shape[-1])
    return y + b


def to_heads(a):
    B, T, _ = a.shape
    return a.reshape(B, T, M_HEADS, M_DH).transpose(0, 2, 1, 3).astype(jnp.float32)


def mlstm_inputs(mq, mk, mv, gates, conv_w, conv_b, gate_b):
    B, T, _ = mq.shape
    qk = jax.nn.silu(centred_dwconv(jnp.concatenate([mq, mk], axis=-1), conv_w, conv_b))
    q, k = jnp.split(qk, 2, axis=-1)
    g = (gates.reshape(B, T, 2, 2, M_HEADS) + gate_b).astype(jnp.float32).transpose(2, 3, 0, 4, 1)
    log_i = g[:, 0]
    log_f = jax.nn.log_sigmoid(g[:, 1])
    return to_heads(q), to_heads(k) * (M_DH ** -0.5), to_heads(mv), log_i, log_f


def zero_state(B):
    return (jnp.zeros((B, M_HEADS, M_DH, M_DH), jnp.float32),
            jnp.zeros((B, M_HEADS, M_DH), jnp.float32),
            jnp.zeros((B, M_HEADS), jnp.float32))


def mlstm_chunkwise(q, k, v, log_i, log_f, state):
    B, H, T, dk = q.shape
    dv = v.shape[-1]
    nc = T // CHUNK

    def to_chunks(a):
        return jnp.moveaxis(a.reshape(B, H, nc, CHUNK, *a.shape[3:]), 2, 0)

    causal = jnp.tril(jnp.ones((CHUNK, CHUNK), dtype=bool))

    def step(carry, inp):
        C, n, m = carry
        qb, kb, vb, li, lf = inp
        b = jnp.cumsum(lf, axis=-1)
        dmat = jnp.where(causal, b[..., :, None] - b[..., None, :] + li[..., None, :], -jnp.inf)
        inter = b + m[..., None]
        m_t = jnp.maximum(inter, jnp.max(dmat, axis=-1))
        w_intra = jnp.exp(dmat - m_t[..., None])
        w_inter = jnp.exp(inter - m_t)
        s = jnp.einsum('bhld,bhsd->bhls', qb, kb) * w_intra
        num = jnp.einsum('bhls,bhsv->bhlv', s, vb) + w_inter[..., None] * jnp.einsum('bhld,bhdv->bhlv', qb, C)
        den = jnp.sum(s, axis=-1) + w_inter * jnp.einsum('bhld,bhd->bhl', qb, n)
        h = num / jnp.maximum(jnp.abs(den), jnp.exp(-m_t))[..., None]
        b_last = b[..., -1]
        w_log = b_last[..., None] - b + li
        m_new = jnp.maximum(b_last + m, jnp.max(w_log, axis=-1))
        w = jnp.exp(w_log - m_new[..., None])
        decay = jnp.exp(b_last + m - m_new)
        C_new = decay[..., None, None] * C + jnp.einsum('bhsd,bhsv->bhdv', kb * w[..., None], vb)
        n_new = decay[..., None] * n + jnp.einsum('bhs,bhsd->bhd', w, kb)
        return (C_new, n_new, m_new), h

    state, hs = lax.scan(step, state, (to_chunks(q), to_chunks(k), to_chunks(v), to_chunks(log_i), to_chunks(log_f)))
    return jnp.moveaxis(hs, 0, 2).reshape(B, H, T, dv), state


def flip_t(a):
    return jnp.flip(a, axis=2)


def bidir_mlstm(q, k, v, log_i, log_f, init_f, init_b):
    h_f, st_f = mlstm_chunkwise(q, k, v, log_i[0], log_f[0], init_f)
    h_b, st_b = mlstm_chunkwise(flip_t(q), flip_t(k), flip_t(v), flip_t(log_i[1]), flip_t(log_f[1]), init_b)
    return h_f + flip_t(h_b), st_f, st_b


def mlstm_post(h, o, g):
    B, T, _ = o.shape
    hn = rms_norm(jnp.transpose(h, (0, 2, 1, 3)), g.reshape(M_HEADS, M_DH)).astype(o.dtype)
    return jax.nn.sigmoid(o) * hn.reshape(B, T, M_WIDTH)


def hybrid_mixer(h_lat, h_ctx, cos, sin, w_in_l, w_out_l, q_g, k_g, lam, lam_init, subln_g_l,
                 conv_w_l, conv_b_l, gate_b_l, mnorm_g_l, need_ctx):
    pl = jnp.split(h_lat @ w_in_l, SPLITS, axis=-1)
    pc = jnp.split(h_ctx @ w_in_l, SPLITS, axis=-1)
    ql, kl, vl = diff_heads(pl[0], pl[1], pl[2], q_g, k_g)
    ql, kl = apply_rope(ql, cos, sin), apply_rope(kl, cos, sin)
    qc, kc, vc = diff_heads(pc[0], pc[1], pc[2], q_g, k_g)
    k_all = jnp.concatenate([kl, kc], axis=1)
    v_all = jnp.concatenate([vl, vc], axis=1)
    a_lat = diff_post(blocked_latent_attention(ql, k_all, v_all, lam), subln_g_l, lam_init)
    mc = mlstm_inputs(pc[3], pc[4], pc[5], pc[7], conv_w_l, conv_b_l, gate_b_l)
    ml = mlstm_inputs(pl[3], pl[4], pl[5], pl[7], conv_w_l, conv_b_l, gate_b_l)
    z = zero_state(h_lat.shape[0])
    hc_sum, st_f, st_b = bidir_mlstm(mc[0], mc[1], mc[2], mc[3], mc[4], z, z)
    hl_sum, _, _ = bidir_mlstm(ml[0], ml[1], ml[2], ml[3], ml[4], st_f, st_b)
    m_lat = mlstm_post(hl_sum, pl[6], mnorm_g_l)
    out_lat = jnp.concatenate([a_lat, m_lat], axis=-1) @ w_out_l
    out_ctx = None
    if need_ctx:
        a_ctx = diff_post(diff_attention(qc, kc, vc, lam), subln_g_l, lam_init)
        m_ctx = mlstm_post(hc_sum, pc[6], mnorm_g_l)
        out_ctx = jnp.concatenate([a_ctx, m_ctx], axis=-1) @ w_out_l
    return out_lat, out_ctx


def swiglu(h, w1, w3, w2):
    return (jax.nn.silu(h @ w1) * (h @ w3)) @ w2


def moe_swiglu(h, w_r, w1, w3, w2):
    B, T, D = h.shape
    t = h.reshape(B * T, D)
    logits = (t @ w_r).astype(jnp.float32)
    top_val, top_idx = lax.top_k(logits, TOP_K)
    top_w = jax.nn.softmax(top_val, axis=-1)
    comb = jnp.sum(jax.nn.one_hot(top_idx, N_EXPERTS, dtype=jnp.float32) * top_w[..., None], axis=1).astype(h.dtype)
    out = jnp.zeros_like(t)
    for e in range(N_EXPERTS):
        out = out + comb[:, e:e + 1] * swiglu(t, w1[e], w3[e], w2[e])
    return out.reshape(B, T, D)


def channel_mixer(h, l, ffn_w1, ffn_w3, ffn_w2, router_w, moe_w1, moe_w3, moe_w2):
    j = l // 2
    if l % 2 == 0:
        return swiglu(h, ffn_w1[j], ffn_w3[j], ffn_w2[j])
    return moe_swiglu(h, router_w[j], moe_w1[j], moe_w3[j], moe_w2[j])


def setup_inputs(seed: int = 0) -> dict:
    key = jax.random.key(seed)
    ks = jax.random.split(key, 32)
    f32 = jnp.float32

    def nrm(k, shape, scale):
        return jax.random.normal(k, shape, f32) * scale

    D = D_MODEL
    i_b = nrm(ks[19], (DEPTH, 2, M_HEADS), 0.1)
    f_b = jnp.linspace(3.0, 6.0, M_HEADS, dtype=f32)[None, None, :] + nrm(ks[20], (DEPTH, 2, M_HEADS), 0.1)
    return {
        'x': nrm(ks[0], (BATCH, SEQ, D), 1.0),
        'c': nrm(ks[1], (BATCH, D), 1.0),
        'ctx': nrm(ks[2], (BATCH, CTX_LEN, D), 1.0),
        'c_ctx': nrm(ks[3], (D,), 1.0),
        'ada_w': nrm(ks[4], (DEPTH, D, 6 * D), 0.5 * D ** -0.5),
        'ada_b': nrm(ks[5], (DEPTH, 6 * D), 0.02),
        'norm1_g': 1.0 + nrm(ks[6], (DEPTH, D), 0.05),
        'norm2_g': 1.0 + nrm(ks[7], (DEPTH, D), 0.05),
        'w_in': nrm(ks[8], (DEPTH, D, IN_COLS), D ** -0.5),
        'w_out': nrm(ks[9], (DEPTH, D_MIX, D), D_MIX ** -0.5),
        'q_norm_g': 1.0 + nrm(ks[10], (DEPTH, A_DH), 0.05),
        'k_norm_g': 1.0 + nrm(ks[11], (DEPTH, A_DH), 0.05),
        'lambda_q1': nrm(ks[12], (DEPTH, A_DH), 0.1),
        'lambda_k1': nrm(ks[13], (DEPTH, A_DH), 0.1),
        'lambda_q2': nrm(ks[14], (DEPTH, A_DH), 0.1),
        'lambda_k2': nrm(ks[15], (DEPTH, A_DH), 0.1),
        'subln_g': 1.0 + nrm(ks[16], (DEPTH, A_DV), 0.05),
        'conv_w': nrm(ks[17], (DEPTH, CONV_K, 2 * M_WIDTH), CONV_K ** -0.5),
        'conv_b': nrm(ks[18], (DEPTH, 2 * M_WIDTH), 0.02),
        'gate_b': jnp.stack([i_b, f_b], axis=2),
        'mnorm_g': 1.0 + nrm(ks[21], (DEPTH, M_WIDTH), 0.05),
        'ffn_w1': nrm(ks[22], (N_DENSE, D, D_FF), D ** -0.5),
        'ffn_w3': nrm(ks[23], (N_DENSE, D, D_FF), D ** -0.5),
        'ffn_w2': nrm(ks[24], (N_DENSE, D_FF, D), D_FF ** -0.5),
        'router_w': nrm(ks[25], (N_MOE, D, N_EXPERTS), D ** -0.5),
        'moe_w1': nrm(ks[26], (N_MOE, N_EXPERTS, D, D_FF_EXPERT), D ** -0.5),
        'moe_w3': nrm(ks[27], (N_MOE, N_EXPERTS, D, D_FF_EXPERT), D ** -0.5),
        'moe_w2': nrm(ks[28], (N_MOE, N_EXPERTS, D_FF_EXPERT, D), D_FF_EXPERT ** -0.5),
    }


def reference(x, c, ctx, c_ctx, ada_w, ada_b, norm1_g, norm2_g, w_in, w_out, q_norm_g, k_norm_g,
              lambda_q1, lambda_k1, lambda_q2, lambda_k2, subln_g, conv_w, conv_b, gate_b, mnorm_g,
              ffn_w1, ffn_w3, ffn_w2, router_w, moe_w1, moe_w3, moe_w2):
    cos, sin = axial_rope_tables(x.shape[1])
    for l in range(DEPTH):
        last = l == DEPTH - 1
        lam_init = 0.8 - 0.6 * jnp.exp(jnp.float32(-0.3 * l))
        lam = (jnp.exp(jnp.sum(lambda_q1[l] * lambda_k1[l]).astype(jnp.float32))
               - jnp.exp(jnp.sum(lambda_q2[l] * lambda_k2[l]).astype(jnp.float32)) + lam_init)
        sh1, sc1, g1, sh2, sc2, g2 = adaln(c, ada_w[l], ada_b[l])
        csh1, csc1, cg1, csh2, csc2, cg2 = adaln(c_ctx, ada_w[l], ada_b[l])
        mix_lat, mix_ctx = hybrid_mixer(
            modulate(x, norm1_g[l], sh1, sc1), modulate(ctx, norm1_g[l], csh1, csc1), cos, sin,
            w_in[l], w_out[l], q_norm_g[l], k_norm_g[l], lam, lam_init, subln_g[l],
            conv_w[l], conv_b[l], gate_b[l], mnorm_g[l], not last)
        x = x + g1 * mix_lat
        x = x + g2 * channel_mixer(modulate(x, norm2_g[l], sh2, sc2), l,
                                   ffn_w1, ffn_w3, ffn_w2, router_w, moe_w1, moe_w3, moe_w2)
        if not last:
            ctx = ctx + cg1 * mix_ctx
            ctx = ctx + cg2 * channel_mixer(modulate(ctx, norm2_g[l], csh2, csc2), l,
                                            ffn_w1, ffn_w3, ffn_w2, router_w, moe_w1, moe_w3, moe_w2)
    return x
```

```python
import functools
import math

import jax
import jax.numpy as jnp
from jax import lax
from jax.experimental import pallas as pl
from jax.experimental.pallas import tpu as pltpu

F32 = jnp.float32
BF16 = jnp.bfloat16
HIGHEST = lax.Precision.HIGHEST

GRID_W = 64
A_HEADS = 4
A_DH = 64
A_DV = 2 * A_DH
A_WIDTH = A_HEADS * A_DV
M_HEADS = 4
CONV_K = 5
ROPE_BASE = 10000.0
ROPE_PAIRS_PER_AXIS = A_DH // 4
N_EXPERTS = 8
EPS = 1e-6

LANES = 128
SUBLANES = 8
ROW_TILE = 256
MLSTM_CHUNK = 128
VMEM_LIMIT = 56 * 1024 * 1024

NT_DIMS = (((1,), (1,)), ((), ()))
TN_DIMS = (((0,), (0,)), ((), ()))


def _params(*sem):
    return pltpu.CompilerParams(dimension_semantics=sem, vmem_limit_bytes=VMEM_LIMIT)


def _silu(v):
    return v * jax.nn.sigmoid(v)


def _log_sigmoid(v):
    return jnp.minimum(v, 0.0) - jnp.log1p(jnp.exp(-jnp.abs(v)))


def _rms(v, axis=-1):
    return v * lax.rsqrt(jnp.mean(v * v, axis=axis, keepdims=True) + EPS)


def _adaln_kernel(c_ref, w_ref, b_ref, o_ref):
    s = _silu(c_ref[...])
    o_ref[0] = jnp.dot(s, w_ref[0], precision=HIGHEST, preferred_element_type=F32) + b_ref[0]


def adaln_all(cc, ada_w, ada_b):
    depth, d, d6 = ada_w.shape
    n_col = d6 // d
    return pl.pallas_call(
        _adaln_kernel,
        grid=(depth, n_col),
        in_specs=[
            pl.BlockSpec(cc.shape, lambda l, j: (0, 0)),
            pl.BlockSpec((1, d, d), lambda l, j: (l, 0, j)),
            pl.BlockSpec((1, 1, d), lambda l, j: (l, 0, j)),
        ],
        out_specs=pl.BlockSpec((1, cc.shape[0], d), lambda l, j: (l, 0, j)),
        out_shape=jax.ShapeDtypeStruct((depth, cc.shape[0], d6), F32),
        compiler_params=_params("arbitrary", "arbitrary"),
        name="adaln",
    )(cc, ada_w, ada_b.reshape(depth, 1, d6))


def _proj_in_kernel(x_ref, mod_ref, g_ref, w_ref, wg_ref, wgt_ref, cos_ref, sin_ref, qg_ref, kg_ref,
                    qk_ref, v_ref, mqk_ref, mv_ref, mo_ref, gc_ref, gr_ref):
    x = x_ref[...]
    h = _rms(x) * g_ref[...] * (1.0 + mod_ref[0, 1:2, :]) + mod_ref[0, 0:1, :]
    hb = h.astype(BF16)

    def proj(j):
        return jnp.dot(hb, w_ref[:, j * A_WIDTH:(j + 1) * A_WIDTH], preferred_element_type=F32)

    r = lax.broadcasted_iota(jnp.int32, (LANES, LANES), 0)
    c = lax.broadcasted_iota(jnp.int32, (LANES, LANES), 1)
    group_ones = jnp.where((r & -A_DH) == (c & -A_DH), 1.0, 0.0).astype(BF16)
    lane = lax.broadcasted_iota(jnp.int32, (1, LANES), 1)
    first_half = (lane & (A_DH - 1)) < (A_DH // 2)
    cos = cos_ref[...]
    sin = sin_ref[...]
    for j, gain_ref in ((0, qg_ref), (1, kg_ref)):
        acc = proj(j)
        for hd in range(A_HEADS):
            t = acc[:, hd * LANES:(hd + 1) * LANES]
            ss = jnp.dot((t * t).astype(BF16), group_ones, preferred_element_type=F32)
            tn = t * lax.rsqrt(ss * (1.0 / A_DH) + EPS) * gain_ref[...]
            rot = jnp.where(first_half, pltpu.roll(tn, LANES - A_DH // 2, 1), pltpu.roll(tn, A_DH // 2, 1))
            qk_ref[:, j * A_WIDTH + hd * LANES: j * A_WIDTH + (hd + 1) * LANES] = (tn * cos + rot * sin).astype(BF16)
    v_ref[...] = proj(2).astype(BF16)
    mqk_ref[:, :A_WIDTH] = proj(3)
    mqk_ref[:, A_WIDTH:] = proj(4)
    mv_ref[...] = proj(5).astype(BF16)
    mo_ref[...] = proj(6)
    gc_ref[...] = jnp.dot(h, wg_ref[...], precision=HIGHEST, preferred_element_type=F32)
    gr_ref[...] = lax.dot_general(wgt_ref[...], h, NT_DIMS, precision=HIGHEST, preferred_element_type=F32)


def proj_in(xs, mods, norm_g, w_main, w_gate, w_gate_t, cos_t, sin_t, qg, kg, n_ctx_tiles, tiles_per_seq):
    rows, d = xs.shape
    n_tiles = rows // ROW_TILE
    n_main = w_main.shape[1]
    n_gate = w_gate.shape[1]

    def seg(i):
        return jnp.where(i < n_ctx_tiles, 0, 1 + (i - n_ctx_tiles) // tiles_per_seq)

    def rope_blk(i):
        return jnp.where(i < n_ctx_tiles, 0, 1 + (i - n_ctx_tiles) % tiles_per_seq)

    row = lambda i: (i, 0)
    const = lambda i: (0, 0)
    return pl.pallas_call(
        _proj_in_kernel,
        grid=(n_tiles,),
        in_specs=[
            pl.BlockSpec((ROW_TILE, d), row),
            pl.BlockSpec((1, 6, d), lambda i: (seg(i), 0, 0)),
            pl.BlockSpec((1, d), const),
            pl.BlockSpec((d, n_main), const),
            pl.BlockSpec((d, n_gate), const),
            pl.BlockSpec((n_gate, d), const),
            pl.BlockSpec((ROW_TILE, LANES), lambda i: (rope_blk(i), 0)),
            pl.BlockSpec((ROW_TILE, LANES), lambda i: (rope_blk(i), 0)),
            pl.BlockSpec((1, LANES), const),
            pl.BlockSpec((1, LANES), const),
        ],
        out_specs=[
            pl.BlockSpec((ROW_TILE, 2 * A_WIDTH), row),
            pl.BlockSpec((ROW_TILE, A_WIDTH), row),
            pl.BlockSpec((ROW_TILE, 2 * A_WIDTH), row),
            pl.BlockSpec((ROW_TILE, A_WIDTH), row),
            pl.BlockSpec((ROW_TILE, A_WIDTH), row),
            pl.BlockSpec((ROW_TILE, n_gate), row),
            pl.BlockSpec((n_gate, ROW_TILE), lambda i: (0, i)),
        ],
        out_shape=[
            jax.ShapeDtypeStruct((rows, 2 * A_WIDTH), BF16),
            jax.ShapeDtypeStruct((rows, A_WIDTH), BF16),
            jax.ShapeDtypeStruct((rows, 2 * A_WIDTH), F32),
            jax.ShapeDtypeStruct((rows, A_WIDTH), BF16),
            jax.ShapeDtypeStruct((rows, A_WIDTH), F32),
            jax.ShapeDtypeStruct((rows, n_gate), F32),
            jax.ShapeDtypeStruct((n_gate, rows), F32),
        ],
        compiler_params=_params("arbitrary"),
        name="proj_in",
    )(xs, mods, norm_g, w_main, w_gate, w_gate_t, cos_t, sin_t, qg, kg)


def _conv_kernel(x_ref, prev_ref, next_ref, w_ref, b_ref, o_ref, ext_ref, *, n_ctx_tiles, tiles_per_seq, k_scale):
    i = pl.program_id(0)
    pos = (i - n_ctx_tiles) % tiles_per_seq
    is_ctx = i < n_ctx_tiles
    first = jnp.logical_or(is_ctx, pos == 0)
    last = jnp.logical_or(is_ctx, pos == tiles_per_seq - 1)
    tm = x_ref.shape[0]
    ext_ref[0:SUBLANES, :] = jnp.where(first, 0.0, prev_ref[...])
    ext_ref[SUBLANES:SUBLANES + tm, :] = x_ref[...]
    ext_ref[SUBLANES + tm:, :] = jnp.where(last, 0.0, next_ref[...])
    acc = jnp.zeros(x_ref.shape, F32) + b_ref[...]
    pad = CONV_K // 2
    for j in range(CONV_K):
        acc = acc + w_ref[j:j + 1, :] * ext_ref[SUBLANES - pad + j:SUBLANES - pad + j + tm, :]
    y = _silu(acc)
    half = y.shape[1] // 2
    o_ref[:, :half] = y[:, :half].astype(BF16)
    o_ref[:, half:] = (y[:, half:] * k_scale).astype(BF16)


def conv_silu(mqk, conv_w, conv_b, n_ctx_tiles, tiles_per_seq, k_scale):
    rows, width = mqk.shape
    n_tiles = rows // ROW_TILE
    per = ROW_TILE // SUBLANES
    n_sub = rows // SUBLANES
    kern = functools.partial(_conv_kernel, n_ctx_tiles=n_ctx_tiles, tiles_per_seq=tiles_per_seq, k_scale=k_scale)
    return pl.pallas_call(
        kern,
        grid=(n_tiles,),
        in_specs=[
            pl.BlockSpec((ROW_TILE, width), lambda i: (i, 0)),
            pl.BlockSpec((SUBLANES, width), lambda i: (jnp.maximum(i * per - 1, 0), 0)),
            pl.BlockSpec((SUBLANES, width), lambda i: (jnp.minimum((i + 1) * per, n_sub - 1), 0)),
            pl.BlockSpec(conv_w.shape, lambda i: (0, 0)),
            pl.BlockSpec((1, width), lambda i: (0, 0)),
        ],
        out_specs=pl.BlockSpec((ROW_TILE, width), lambda i: (i, 0)),
        out_shape=jax.ShapeDtypeStruct((rows, width), BF16),
        scratch_shapes=[pltpu.VMEM((ROW_TILE + 2 * SUBLANES, width), F32)],
        compiler_params=_params("arbitrary"),
        name="conv_silu",
    )(mqk, mqk, mqk, conv_w, conv_b)


def _attn_kernel(lam_ref, q_ref, kc_ref, kl_ref, vc_ref, vl_ref, g_ref, o_ref, *, first_q, post_scale):
    qt = pl.program_id(2) + first_q
    lam = lam_ref[0, 0]
    q = q_ref[...]
    lane = lax.broadcasted_iota(jnp.int32, (1, LANES), 1)
    q1 = jnp.where(lane < A_DH, q, jnp.zeros_like(q))
    q2 = jnp.where(lane >= A_DH, q, jnp.zeros_like(q))

    def scores(k_ref):
        k = k_ref[...]
        return (lax.dot_general(q1, k, NT_DIMS, preferred_element_type=F32),
                lax.dot_general(q2, k, NT_DIMS, preferred_element_type=F32))

    def finish(parts):
        es, rs = [], []
        for m in range(2):
            mx = functools.reduce(jnp.maximum, [jnp.max(s[m], axis=1, keepdims=True) for s, _ in parts])
            e = [jnp.exp(s[m] - mx) for s, _ in parts]
            z = functools.reduce(lambda a, b: a + b, [jnp.sum(ei, axis=1, keepdims=True) for ei in e])
            es.append(e)
            rs.append(1.0 / z)
        out = None
        for idx, (_, v_ref) in enumerate(parts):
            a = es[0][idx] * rs[0] - es[1][idx] * (lam * rs[1])
            o = jnp.dot(a.astype(BF16), v_ref[...], preferred_element_type=F32)
            out = o if out is None else out + o
        o_ref[...] = (_rms(out) * g_ref[...] * post_scale).astype(BF16)

    @pl.when(qt == 0)
    def _():
        finish([(scores(kc_ref), vc_ref)])

    @pl.when(qt > 0)
    def _():
        finish([(scores(kc_ref), vc_ref), (scores(kl_ref), vl_ref)])


def attention(lam, qk, v, subln_g, post_scale, batch, ctx_len, seq, with_ctx_queries):
    rows = qk.shape[0]
    tiles_per_seq = seq // ROW_TILE
    n_ctx_tiles = batch * ctx_len // ROW_TILE
    assert ctx_len == ROW_TILE
    first_q = 0 if with_ctx_queries else 1
    nq = tiles_per_seq + 1 - first_q

    def qblk(b, h, t):
        qt = t + first_q
        return jnp.where(qt == 0, b, n_ctx_tiles + b * tiles_per_seq + qt - 1)

    lat_blk0 = batch * ctx_len // seq
    kern = functools.partial(_attn_kernel, first_q=first_q, post_scale=post_scale)
    return pl.pallas_call(
        kern,
        grid=(batch, A_HEADS, nq),
        in_specs=[
            pl.BlockSpec(memory_space=pltpu.SMEM),
            pl.BlockSpec((ROW_TILE, LANES), lambda b, h, t: (qblk(b, h, t), h)),
            pl.BlockSpec((ctx_len, LANES), lambda b, h, t: (b, A_HEADS + h)),
            pl.BlockSpec((seq, LANES), lambda b, h, t: (lat_blk0 + b, A_HEADS + h)),
            pl.BlockSpec((ctx_len, LANES), lambda b, h, t: (b, h)),
            pl.BlockSpec((seq, LANES), lambda b, h, t: (lat_blk0 + b, h)),
            pl.BlockSpec((1, LANES), lambda b, h, t: (0, 0)),
        ],
        out_specs=pl.BlockSpec((ROW_TILE, LANES), lambda b, h, t: (qblk(b, h, t), h)),
        out_shape=jax.ShapeDtypeStruct((rows, A_WIDTH), BF16),
        compiler_params=_params("arbitrary", "arbitrary", "arbitrary"),
        name="diff_attention",
    )(lam, qk, qk, qk, v, v, subln_g)


def _mlstm_kernel(qkc_ref, qkl_ref, vc_ref, vl_ref, oc_ref, ol_ref, gcc_ref, gcl_ref, grc_ref, grl_ref,
                  brow_ref, bcol_ref, mg_ref, outc_ref, outl_ref, c_ref, n_ref, hc_ref, hl_ref):
    L = MLSTM_CHUNK
    dh = LANES
    width = M_HEADS * dh
    r = lax.broadcasted_iota(jnp.int32, (L, L), 0)
    c = lax.broadcasted_iota(jnp.int32, (L, L), 1)
    c_ref[...] = jnp.zeros(c_ref.shape, F32)
    n_ref[...] = jnp.zeros(n_ref.shape, F32)

    for dirn in range(2):
        visible = (c <= r) if dirn == 0 else (c >= r)
        tri = jnp.where(visible, 1.0, 0.0)
        tri_t = jnp.where((r <= c) if dirn == 0 else (r >= c), 1.0, 0.0)

        def step(ci, ms, refs, dirn=dirn, visible=visible, tri=tri, tri_t=tri_t):
            qk_ref, v_ref, o_ref, gc_ref, gr_ref, h_ref, out_ref = refs
            rows = pl.ds(pl.multiple_of(ci * L, L), L)
            g_col = gc_ref[rows, :] + brow_ref[...]
            g_row = gr_ref[:, rows] + bcol_ref[...]
            ls_col = _log_sigmoid(g_col)
            ls_row = _log_sigmoid(g_row)
            b_cols = jnp.dot(tri, ls_col, precision=HIGHEST, preferred_element_type=F32)
            b_rows = jnp.dot(ls_row, tri_t, precision=HIGHEST, preferred_element_type=F32)
            new_ms = []
            for hd in range(M_HEADS):
                gi = dirn * 2 * M_HEADS + hd
                gf = gi + M_HEADS
                ch = dirn * M_HEADS + hd
                m = ms[hd]
                b_col = b_cols[:, gf:gf + 1]
                b_row = b_rows[gf:gf + 1, :]
                li_row = g_row[gi:gi + 1, :]
                li_col = g_col[:, gi:gi + 1]
                dmat = jnp.where(visible, b_col - b_row + li_row, -jnp.inf)
                inter = b_col + m
                m_t = jnp.maximum(inter, jnp.max(dmat, axis=1, keepdims=True))
                w_intra = jnp.exp(dmat - m_t)
                w_inter = jnp.exp(inter - m_t)
                q = qk_ref[rows, hd * dh:(hd + 1) * dh]
                k = qk_ref[rows, width + hd * dh: width + (hd + 1) * dh]
                v = v_ref[rows, hd * dh:(hd + 1) * dh]
                s = lax.dot_general(q, k, NT_DIMS, preferred_element_type=F32) * w_intra
                cs = c_ref[ch]
                ns = n_ref[ch]
                num = (jnp.dot(s.astype(BF16), v, preferred_element_type=F32)
                       + w_inter * jnp.dot(q, cs.astype(BF16), preferred_element_type=F32))
                den = (jnp.sum(s, axis=1, keepdims=True)
                       + w_inter * jnp.sum(q.astype(F32) * ns, axis=1, keepdims=True))
                hh = num / jnp.maximum(jnp.abs(den), jnp.exp(-m_t))
                b_last = jnp.sum(ls_row[gf:gf + 1, :], axis=1, keepdims=True)
                w_log = b_last - b_col + li_col
                m_new = jnp.maximum(b_last + m, jnp.max(w_log, axis=0, keepdims=True))
                w = jnp.exp(w_log - m_new)
                decay = jnp.exp(b_last + m - m_new)
                kw = k.astype(F32) * w
                c_ref[ch] = decay * cs + lax.dot_general(kw.astype(BF16), v, TN_DIMS, preferred_element_type=F32)
                n_ref[ch] = decay * ns + jnp.sum(kw, axis=0, keepdims=True)
                new_ms.append(m_new)
                cols = slice(hd * dh, (hd + 1) * dh)
                if dirn == 0:
                    h_ref[rows, cols] = hh
                else:
                    tot = h_ref[rows, cols] + hh
                    hn = _rms(tot) * mg_ref[:, cols]
                    out_ref[rows, cols] = (jax.nn.sigmoid(o_ref[rows, cols]) * hn).astype(BF16)
            return tuple(new_ms)

        ctx_refs = (qkc_ref, vc_ref, oc_ref, gcc_ref, grc_ref, hc_ref, outc_ref)
        lat_refs = (qkl_ref, vl_ref, ol_ref, gcl_ref, grl_ref, hl_ref, outl_ref)
        ms = tuple(jnp.zeros((1, 1), F32) for _ in range(M_HEADS))
        for refs in (ctx_refs, lat_refs):
            n_chunks = refs[0].shape[0] // L

            def body(i, ms, refs=refs, n_chunks=n_chunks, dirn=dirn):
                ci = i if dirn == 0 else n_chunks - 1 - i
                return step(ci, ms, refs)

            ms = lax.fori_loop(0, n_chunks, body, ms)


def mlstm(mqk, mv, mo, g_col, g_row, gate_b, mnorm_g, batch, ctx_len, seq):
    rows, w2 = mqk.shape
    width = w2 // 2
    n_gate = g_col.shape[1]
    lat0 = batch * ctx_len // seq
    ctx = lambda b: (b, 0)
    lat = lambda b: (lat0 + b, 0)
    const = lambda b: (0, 0)
    out_c, out_l = pl.pallas_call(
        _mlstm_kernel,
        grid=(batch,),
        in_specs=[
            pl.BlockSpec((ctx_len, w2), ctx), pl.BlockSpec((seq, w2), lat),
            pl.BlockSpec((ctx_len, width), ctx), pl.BlockSpec((seq, width), lat),
            pl.BlockSpec((ctx_len, width), ctx), pl.BlockSpec((seq, width), lat),
            pl.BlockSpec((ctx_len, n_gate), ctx), pl.BlockSpec((seq, n_gate), lat),
            pl.BlockSpec((n_gate, ctx_len), lambda b: (0, b)), pl.BlockSpec((n_gate, seq), lambda b: (0, lat0 + b)),
            pl.BlockSpec((1, n_gate), const), pl.BlockSpec((n_gate, 1), const),
            pl.BlockSpec((1, width), const),
        ],
        out_specs=[pl.BlockSpec((ctx_len, width), ctx), pl.BlockSpec((seq, width), lat)],
        out_shape=[jax.ShapeDtypeStruct((batch * ctx_len, width), BF16),
                   jax.ShapeDtypeStruct((rows, width), BF16)],
        scratch_shapes=[
            pltpu.VMEM((2 * M_HEADS, LANES, LANES), F32),
            pltpu.VMEM((2 * M_HEADS, 1, LANES), F32),
            pltpu.VMEM((ctx_len, width), F32),
            pltpu.VMEM((seq, width), F32),
        ],
        compiler_params=_params("arbitrary"),
        name="mlstm",
    )(mqk, mqk, mv, mv, mo, mo, g_col, g_col, g_row, g_row,
      gate_b.reshape(1, n_gate), gate_b.reshape(n_gate, 1), mnorm_g)
    return out_c, out_l


def _router_top2(logits):
    n = logits.shape[1]
    lane = lax.broadcasted_iota(jnp.int32, logits.shape, 1)
    v1 = jnp.max(logits, axis=1, keepdims=True)
    i1 = jnp.min(jnp.where(logits == v1, lane, n), axis=1, keepdims=True)
    rest = jnp.where(lane == i1, -jnp.inf, logits)
    v2 = jnp.max(rest, axis=1, keepdims=True)
    i2 = jnp.min(jnp.where(rest == v2, lane, n), axis=1, keepdims=True)
    e = jnp.exp(v2 - v1)
    w1 = 1.0 / (1.0 + e)
    w2 = e / (1.0 + e)
    return jnp.where(lane == i1, w1, 0.0) + jnp.where(lane == i2, w2, 0.0)


def _out_proj_kernel(*refs, with_router):
    if with_router:
        a_ref, m_ref, w_ref, x_ref, mod_ref, g_ref, rw_ref, x1_ref, h2_ref, comb_ref = refs
    else:
        a_ref, m_ref, w_ref, x_ref, mod_ref, g_ref, x1_ref, h2_ref = refs
    half = a_ref.shape[1]
    mix = (jnp.dot(a_ref[...], w_ref[:half, :], preferred_element_type=F32)
           + jnp.dot(m_ref[...], w_ref[half:, :], preferred_element_type=F32))
    x1 = x_ref[...] + mod_ref[0, 2:3, :] * mix
    x1_ref[...] = x1
    h2 = _rms(x1) * g_ref[...] * (1.0 + mod_ref[0, 4:5, :]) + mod_ref[0, 3:4, :]
    h2_ref[...] = h2.astype(BF16)
    if with_router:
        logits = jnp.dot(h2, rw_ref[...], precision=HIGHEST, preferred_element_type=F32)
        comb_ref[...] = _router_top2(logits)


def out_proj(a, m, w_out, xs, mods, norm_g, router_w, first_tile, n_ctx_tiles, tiles_per_seq):
    rows, d = xs.shape
    half = a.shape[1]
    n_tiles = rows // ROW_TILE - first_tile
    out_rows = n_tiles * ROW_TILE
    with_router = router_w is not None

    def seg(i):
        t = i + first_tile
        return jnp.where(t < n_ctx_tiles, 0, 1 + (t - n_ctx_tiles) // tiles_per_seq)

    src = lambda i: (i + first_tile, 0)
    dst = lambda i: (i, 0)
    const = lambda i: (0, 0)
    in_specs = [
        pl.BlockSpec((ROW_TILE, half), src), pl.BlockSpec((ROW_TILE, half), src),
        pl.BlockSpec(w_out.shape, const),
        pl.BlockSpec((ROW_TILE, d), src),
        pl.BlockSpec((1, 6, d), lambda i: (seg(i), 0, 0)),
        pl.BlockSpec((1, d), const),
    ]
    out_specs = [pl.BlockSpec((ROW_TILE, d), dst), pl.BlockSpec((ROW_TILE, d), dst)]
    out_shape = [jax.ShapeDtypeStruct((out_rows, d), F32), jax.ShapeDtypeStruct((out_rows, d), BF16)]
    args = [a, m, w_out, xs, mods, norm_g]
    if with_router:
        ne = router_w.shape[1]
        in_specs.append(pl.BlockSpec(router_w.shape, const))
        out_specs.append(pl.BlockSpec((ROW_TILE, ne), dst))
        out_shape.append(jax.ShapeDtypeStruct((out_rows, ne), F32))
        args.append(router_w)
    return pl.pallas_call(
        functools.partial(_out_proj_kernel, with_router=with_router),
        grid=(n_tiles,),
        in_specs=in_specs, out_specs=out_specs, out_shape=out_shape,
        compiler_params=_params("arbitrary"),
        name="out_proj",
    )(*args)


def _ffn_kernel(h_ref, w1_ref, w3_ref, w2_ref, x_ref, mod_ref, o_ref, acc_ref):
    f = pl.program_id(1)

    @pl.when(f == 0)
    def _():
        acc_ref[...] = jnp.zeros(acc_ref.shape, F32)

    h = h_ref[...]
    a = _silu(jnp.dot(h, w1_ref[...], preferred_element_type=F32)) * jnp.dot(h, w3_ref[...], preferred_element_type=F32)
    acc_ref[...] += jnp.dot(a.astype(BF16), w2_ref[...], preferred_element_type=F32)

    @pl.when(f == pl.num_programs(1) - 1)
    def _():
        o_ref[...] = x_ref[...] + mod_ref[0, 5:6, :] * acc_ref[...]


def ffn(h2, w1, w3, w2, x1, mods, tm, tf, n_ctx_rows, rows_per_seq):
    rows, d = x1.shape
    dff = w1.shape[1]

    def seg(i):
        r0 = i * tm
        return jnp.where(r0 < n_ctx_rows, 0, 1 + (r0 - n_ctx_rows) // rows_per_seq)

    return pl.pallas_call(
        _ffn_kernel,
        grid=(rows // tm, dff // tf),
        in_specs=[
            pl.BlockSpec((tm, d), lambda i, f: (i, 0)),
            pl.BlockSpec((d, tf), lambda i, f: (0, f)),
            pl.BlockSpec((d, tf), lambda i, f: (0, f)),
            pl.BlockSpec((tf, d), lambda i, f: (f, 0)),
            pl.BlockSpec((tm, d), lambda i, f: (i, 0)),
            pl.BlockSpec((1, 6, d), lambda i, f: (seg(i), 0, 0)),
        ],
        out_specs=pl.BlockSpec((tm, d), lambda i, f: (i, 0)),
        out_shape=jax.ShapeDtypeStruct((rows, d), F32),
        scratch_shapes=[pltpu.VMEM((tm, d), F32)],
        compiler_params=_params("arbitrary", "arbitrary"),
        name="ffn",
    )(h2, w1, w3, w2, x1, mods)


def _moe_dense_kernel(h_ref, comb_ref, w1_ref, w3_ref, w2_ref, x_ref, mod_ref, o_ref, acc_ref):
    e = pl.program_id(1)
    f = pl.program_id(2)

    @pl.when(jnp.logical_and(e == 0, f == 0))
    def _():
        acc_ref[...] = jnp.zeros(acc_ref.shape, F32)

    comb = comb_ref[...]
    lane = lax.broadcasted_iota(jnp.int32, comb.shape, 1)
    wt = jnp.sum(jnp.where(lane == e, comb, 0.0), axis=1, keepdims=True)
    h = h_ref[...]
    a = _silu(jnp.dot(h, w1_ref[0], preferred_element_type=F32)) * jnp.dot(h, w3_ref[0], preferred_element_type=F32)
    acc_ref[...] += jnp.dot((a * wt).astype(BF16), w2_ref[0], preferred_element_type=F32)

    @pl.when(jnp.logical_and(e == pl.num_programs(1) - 1, f == pl.num_programs(2) - 1))
    def _():
        o_ref[...] = x_ref[...] + mod_ref[0, 5:6, :] * acc_ref[...]


def moe_dense(h2, comb, w1, w3, w2, x1, mods, tm, tf, rows_per_seq):
    rows, d = x1.shape
    ne, _, dff = w1.shape
    return pl.pallas_call(
        _moe_dense_kernel,
        grid=(rows // tm, ne, dff // tf),
        in_specs=[
            pl.BlockSpec((tm, d), lambda i, e, f: (i, 0)),
            pl.BlockSpec((tm, ne), lambda i, e, f: (i, 0)),
            pl.BlockSpec((1, d, tf), lambda i, e, f: (e, 0, f)),
            pl.BlockSpec((1, d, tf), lambda i, e, f: (e, 0, f)),
            pl.BlockSpec((1, tf, d), lambda i, e, f: (e, f, 0)),
            pl.BlockSpec((tm, d), lambda i, e, f: (i, 0)),
            pl.BlockSpec((1, 6, d), lambda i, e, f: (1 + (i * tm) // rows_per_seq, 0, 0)),
        ],
        out_specs=pl.BlockSpec((tm, d), lambda i, e, f: (i, 0)),
        out_shape=jax.ShapeDtypeStruct((rows, d), F32),
        scratch_shapes=[pltpu.VMEM((tm, d), F32)],
        compiler_params=_params("arbitrary", "arbitrary", "arbitrary"),
        name="moe",
    )(h2, comb, w1, w3, w2, x1, mods)


def _rope_tables(ctx_len, seq):
    rows = seq // GRID_W
    row = jnp.repeat(jnp.arange(rows, dtype=F32), GRID_W)
    col = jnp.tile(jnp.arange(GRID_W, dtype=F32), rows)
    inv = ROPE_BASE ** (-jnp.arange(ROPE_PAIRS_PER_AXIS, dtype=F32) / ROPE_PAIRS_PER_AXIS)
    ang = jnp.concatenate([row[:, None] * inv, col[:, None] * inv], axis=-1)
    cos, sin = jnp.cos(ang), jnp.sin(ang)
    cos_t = jnp.concatenate([cos, cos, cos, cos], axis=-1)
    sin_t = jnp.concatenate([-sin, sin, -sin, sin], axis=-1)
    cos_t = jnp.concatenate([jnp.ones((ctx_len, LANES), F32), cos_t], axis=0)
    sin_t = jnp.concatenate([jnp.zeros((ctx_len, LANES), F32), sin_t], axis=0)
    return cos_t, sin_t


def kernel(x, c, ctx, c_ctx, ada_w, ada_b, norm1_g, norm2_g, w_in, w_out, q_norm_g, k_norm_g, lambda_q1, lambda_k1, lambda_q2, lambda_k2, subln_g, conv_w, conv_b, gate_b, mnorm_g, ffn_w1, ffn_w3, ffn_w2, router_w, moe_w1, moe_w3, moe_w2):
    batch, seq, d = x.shape
    ctx_len = ctx.shape[1]
    depth = ada_w.shape[0]
    assert depth == 2, "layer 0 runs the dense FFN on all rows, layer 1 the experts on the latents"
    n_ctx_rows = batch * ctx_len
    n_ctx_tiles = n_ctx_rows // ROW_TILE
    tiles_per_seq = seq // ROW_TILE
    n_main = w_in.shape[2] - gate_b[0].size
    m_dh = (n_main - 3 * A_WIDTH) // 4 // M_HEADS

    xs = jnp.concatenate([ctx.reshape(n_ctx_rows, d), x.reshape(batch * seq, d)], axis=0)

    cc = jnp.zeros((2 * SUBLANES, d), F32).at[:batch].set(c).at[batch].set(c_ctx)
    mod_rows = adaln_all(cc, ada_w, ada_b)
    mods_all = jnp.concatenate([mod_rows[:, batch:batch + 1], mod_rows[:, :batch]], axis=1)
    mods_all = mods_all.reshape(depth, batch + 1, 6, d)

    cos_t, sin_t = _rope_tables(ctx_len, seq)

    for l in range(depth):
        last = l == depth - 1
        lam_init = 0.8 - 0.6 * math.exp(-0.3 * l)
        lam = (jnp.exp(jnp.sum(lambda_q1[l] * lambda_k1[l]).astype(F32))
               - jnp.exp(jnp.sum(lambda_q2[l] * lambda_k2[l]).astype(F32)) + lam_init)
        mods = mods_all[l]
        w_main = w_in[l, :, :n_main].astype(BF16)
        w_gate = w_in[l, :, n_main:]
        qg = jnp.tile(q_norm_g[l] * (A_DH ** -0.5), 2).reshape(1, LANES)
        kg = jnp.tile(k_norm_g[l], 2).reshape(1, LANES)

        qk, av, mqk, mv, mo, g_col, g_row = proj_in(
            xs, mods, norm1_g[l].reshape(1, d), w_main, w_gate, w_gate.T, cos_t, sin_t, qg, kg,
            n_ctx_tiles, tiles_per_seq)
        mqk_c = conv_silu(mqk, conv_w[l], conv_b[l].reshape(1, -1), n_ctx_tiles, tiles_per_seq, m_dh ** -0.5)
        a_out = attention(lam.reshape(1, 1), qk, av, subln_g[l].reshape(1, LANES), 1.0 - lam_init,
                          batch, ctx_len, seq, with_ctx_queries=not last)
        m_ctx, m_out = mlstm(mqk_c, mv, mo, g_col, g_row, gate_b[l], mnorm_g[l].reshape(1, -1),
                             batch, ctx_len, seq)
        w_out_b = w_out[l].astype(BF16)
        if not last:
            m_out = lax.dynamic_update_slice(m_out, m_ctx, (0, 0))
            x1, h2 = out_proj(a_out, m_out, w_out_b, xs, mods, norm2_g[l].reshape(1, d), None,
                              0, n_ctx_tiles, tiles_per_seq)
            j = l // 2
            xs = ffn(h2, ffn_w1[j].astype(BF16), ffn_w3[j].astype(BF16), ffn_w2[j].astype(BF16), x1, mods,
                     512, 1408, n_ctx_rows, seq)
        else:
            j = l // 2
            x1, h2, comb = out_proj(a_out, m_out, w_out_b, xs, mods, norm2_g[l].reshape(1, d), router_w[j],
                                    n_ctx_tiles, n_ctx_tiles, tiles_per_seq)
            out = moe_dense(h2, comb, moe_w1[j].astype(BF16), moe_w3[j].astype(BF16), moe_w2[j].astype(BF16),
                            x1, mods, 1024, 512, seq)
    return out.reshape(batch, seq, d)
```

```python
import functools
import math

import jax
import jax.numpy as jnp
from jax import lax
from jax.experimental import pallas as pl
from jax.experimental.pallas import tpu as pltpu

F32 = jnp.float32
BF16 = jnp.bfloat16
HIGHEST = lax.Precision.HIGHEST

GRID_W = 64
A_HEADS = 4
A_DH = 64
A_DV = 2 * A_DH
A_WIDTH = A_HEADS * A_DV
M_HEADS = 4
CONV_K = 5
ROPE_BASE = 10000.0
ROPE_PAIRS_PER_AXIS = A_DH // 4
N_EXPERTS = 8
EPS = 1e-6

LANES = 128
SUBLANES = 8
ROW_TILE = 256
MLSTM_CHUNK = 128
VMEM_LIMIT = 56 * 1024 * 1024

NT_DIMS = (((1,), (1,)), ((), ()))
TN_DIMS = (((0,), (0,)), ((), ()))


def _params(*sem):
    return pltpu.CompilerParams(dimension_semantics=sem, vmem_limit_bytes=VMEM_LIMIT)


def _silu(v):
    return v * jax.nn.sigmoid(v)


def _log_sigmoid(v):
    return jnp.minimum(v, 0.0) - jnp.log1p(jnp.exp(-jnp.abs(v)))


def _rms(v, axis=-1):
    return v * lax.rsqrt(jnp.mean(v * v, axis=axis, keepdims=True) + EPS)


def _adaln_kernel(c_ref, w_ref, b_ref, o_ref):
    s = _silu(c_ref[...])
    o_ref[0] = jnp.dot(s, w_ref[0], precision=HIGHEST, preferred_element_type=F32) + b_ref[0]


def adaln_all(cc, ada_w, ada_b):
    depth, d, d6 = ada_w.shape
    n_col = d6 // d
    return pl.pallas_call(
        _adaln_kernel,
        grid=(depth, n_col),
        in_specs=[
            pl.BlockSpec(cc.shape, lambda l, j: (0, 0)),
            pl.BlockSpec((1, d, d), lambda l, j: (l, 0, j)),
            pl.BlockSpec((1, 1, d), lambda l, j: (l, 0, j)),
        ],
        out_specs=pl.BlockSpec((1, cc.shape[0], d), lambda l, j: (l, 0, j)),
        out_shape=jax.ShapeDtypeStruct((depth, cc.shape[0], d6), F32),
        compiler_params=_params("arbitrary", "arbitrary"),
        name="adaln",
    )(cc, ada_w, ada_b.reshape(depth, 1, d6))


def _proj_in_kernel(x_ref, mod_ref, g_ref, w_ref, wg_ref, wgt_ref, cos_ref, sin_ref, qg_ref, kg_ref,
                    qk_ref, v_ref, mqk_ref, mv_ref, mo_ref, gc_ref, gr_ref):
    x = x_ref[...]
    h = _rms(x) * g_ref[...] * (1.0 + mod_ref[0, 1:2, :]) + mod_ref[0, 0:1, :]
    hb = h.astype(BF16)

    def proj(j):
        return jnp.dot(hb, w_ref[:, j * A_WIDTH:(j + 1) * A_WIDTH], preferred_element_type=F32)

    r = lax.broadcasted_iota(jnp.int32, (LANES, LANES), 0)
    c = lax.broadcasted_iota(jnp.int32, (LANES, LANES), 1)
    group_ones = jnp.where((r & -A_DH) == (c & -A_DH), 1.0, 0.0).astype(BF16)
    lane = lax.broadcasted_iota(jnp.int32, (1, LANES), 1)
    first_half = (lane & (A_DH - 1)) < (A_DH // 2)
    cos = cos_ref[...]
    sin = sin_ref[...]
    for j, gain_ref in ((0, qg_ref), (1, kg_ref)):
        acc = proj(j)
        for hd in range(A_HEADS):
            t = acc[:, hd * LANES:(hd + 1) * LANES]
            ss = jnp.dot((t * t).astype(BF16), group_ones, preferred_element_type=F32)
            tn = t * lax.rsqrt(ss * (1.0 / A_DH) + EPS) * gain_ref[...]
            rot = jnp.where(first_half, pltpu.roll(tn, LANES - A_DH // 2, 1), pltpu.roll(tn, A_DH // 2, 1))
            qk_ref[:, j * A_WIDTH + hd * LANES: j * A_WIDTH + (hd + 1) * LANES] = (tn * cos + rot * sin).astype(BF16)
    v_ref[...] = proj(2).astype(BF16)
    mqk_ref[:, :A_WIDTH] = proj(3)
    mqk_ref[:, A_WIDTH:] = proj(4)
    mv_ref[...] = proj(5).astype(BF16)
    mo_ref[...] = proj(6)
    gc_ref[...] = jnp.dot(h, wg_ref[...], precision=HIGHEST, preferred_element_type=F32)
    gr_ref[...] = lax.dot_general(wgt_ref[...], h, NT_DIMS, precision=HIGHEST, preferred_element_type=F32)


def proj_in(xs, mods, norm_g, w_main, w_gate, w_gate_t, cos_t, sin_t, qg, kg, n_ctx_tiles, tiles_per_seq):
    rows, d = xs.shape
    n_tiles = rows // ROW_TILE
    n_main = w_main.shape[1]
    n_gate = w_gate.shape[1]

    def seg(i):
        return jnp.where(i < n_ctx_tiles, 0, 1 + (i - n_ctx_tiles) // tiles_per_seq)

    def rope_blk(i):
        return jnp.where(i < n_ctx_tiles, 0, 1 + (i - n_ctx_tiles) % tiles_per_seq)

    row = lambda i: (i, 0)
    const = lambda i: (0, 0)
    return pl.pallas_call(
        _proj_in_kernel,
        grid=(n_tiles,),
        in_specs=[
            pl.BlockSpec((ROW_TILE, d), row),
            pl.BlockSpec((1, 6, d), lambda i: (seg(i), 0, 0)),
            pl.BlockSpec((1, d), const),
            pl.BlockSpec((d, n_main), const),
            pl.BlockSpec((d, n_gate), const),
            pl.BlockSpec((n_gate, d), const),
            pl.BlockSpec((ROW_TILE, LANES), lambda i: (rope_blk(i), 0)),
            pl.BlockSpec((ROW_TILE, LANES), lambda i: (rope_blk(i), 0)),
            pl.BlockSpec((1, LANES), const),
            pl.BlockSpec((1, LANES), const),
        ],
        out_specs=[
            pl.BlockSpec((ROW_TILE, 2 * A_WIDTH), row),
            pl.BlockSpec((ROW_TILE, A_WIDTH), row),
            pl.BlockSpec((ROW_TILE, 2 * A_WIDTH), row),
            pl.BlockSpec((ROW_TILE, A_WIDTH), row),
            pl.BlockSpec((ROW_TILE, A_WIDTH), row),
            pl.BlockSpec((ROW_TILE, n_gate), row),
            pl.BlockSpec((n_gate, ROW_TILE), lambda i: (0, i)),
        ],
        out_shape=[
            jax.ShapeDtypeStruct((rows, 2 * A_WIDTH), BF16),
            jax.ShapeDtypeStruct((rows, A_WIDTH), BF16),
            jax.ShapeDtypeStruct((rows, 2 * A_WIDTH), F32),
            jax.ShapeDtypeStruct((rows, A_WIDTH), BF16),
            jax.ShapeDtypeStruct((rows, A_WIDTH), F32),
            jax.ShapeDtypeStruct((rows, n_gate), F32),
            jax.ShapeDtypeStruct((n_gate, rows), F32),
        ],
        compiler_params=_params("arbitrary"),
        name="proj_in",
    )(xs, mods, norm_g, w_main, w_gate, w_gate_t, cos_t, sin_t, qg, kg)


def _conv_kernel(x_ref, prev_ref, next_ref, w_ref, b_ref, o_ref, ext_ref, *, n_ctx_tiles, tiles_per_seq, k_scale):
    i = pl.program_id(0)
    pos = (i - n_ctx_tiles) % tiles_per_seq
    is_ctx = i < n_ctx_tiles
    first = jnp.logical_or(is_ctx, pos == 0)
    last = jnp.logical_or(is_ctx, pos == tiles_per_seq - 1)
    tm = x_ref.shape[0]
    ext_ref[0:SUBLANES, :] = jnp.where(first, 0.0, prev_ref[...])
    ext_ref[SUBLANES:SUBLANES + tm, :] = x_ref[...]
    ext_ref[SUBLANES + tm:, :] = jnp.where(last, 0.0, next_ref[...])
    acc = jnp.zeros(x_ref.shape, F32) + b_ref[...]
    pad = CONV_K // 2
    for j in range(CONV_K):
        acc = acc + w_ref[j:j + 1, :] * ext_ref[SUBLANES - pad + j:SUBLANES - pad + j + tm, :]
    y = _silu(acc)
    half = y.shape[1] // 2
    o_ref[:, :half] = y[:, :half].astype(BF16)
    o_ref[:, half:] = (y[:, half:] * k_scale).astype(BF16)


def conv_silu(mqk, conv_w, conv_b, n_ctx_tiles, tiles_per_seq, k_scale):
    rows, width = mqk.shape
    n_tiles = rows // ROW_TILE
    per = ROW_TILE // SUBLANES
    n_sub = rows // SUBLANES
    kern = functools.partial(_conv_kernel, n_ctx_tiles=n_ctx_tiles, tiles_per_seq=tiles_per_seq, k_scale=k_scale)
    return pl.pallas_call(
        kern,
        grid=(n_tiles,),
        in_specs=[
            pl.BlockSpec((ROW_TILE, width), lambda i: (i, 0)),
            pl.BlockSpec((SUBLANES, width), lambda i: (jnp.maximum(i * per - 1, 0), 0)),
            pl.BlockSpec((SUBLANES, width), lambda i: (jnp.minimum((i + 1) * per, n_sub - 1), 0)),
            pl.BlockSpec(conv_w.shape, lambda i: (0, 0)),
            pl.BlockSpec((1, width), lambda i: (0, 0)),
        ],
        out_specs=pl.BlockSpec((ROW_TILE, width), lambda i: (i, 0)),
        out_shape=jax.ShapeDtypeStruct((rows, width), BF16),
        scratch_shapes=[pltpu.VMEM((ROW_TILE + 2 * SUBLANES, width), F32)],
        compiler_params=_params("arbitrary"),
        name="conv_silu",
    )(mqk, mqk, mqk, conv_w, conv_b)


def _attn_kernel(lam_ref, q_ref, kc_ref, kl_ref, vc_ref, vl_ref, g_ref, o_ref, *, first_q, post_scale):
    qt = pl.program_id(2) + first_q
    lam = lam_ref[0, 0]
    q = q_ref[...]
    lane = lax.broadcasted_iota(jnp.int32, (1, LANES), 1)
    q1 = jnp.where(lane < A_DH, q, jnp.zeros_like(q))
    q2 = jnp.where(lane >= A_DH, q, jnp.zeros_like(q))

    def scores(k_ref):
        k = k_ref[...]
        return (lax.dot_general(q1, k, NT_DIMS, preferred_element_type=F32),
                lax.dot_general(q2, k, NT_DIMS, preferred_element_type=F32))

    def finish(parts):
        es, rs = [], []
        for m in range(2):
            mx = functools.reduce(jnp.maximum, [jnp.max(s[m], axis=1, keepdims=True) for s, _ in parts])
            e = [jnp.exp(s[m] - mx) for s, _ in parts]
            z = functools.reduce(lambda a, b: a + b, [jnp.sum(ei, axis=1, keepdims=True) for ei in e])
            es.append(e)
            rs.append(1.0 / z)
        out = None
        for idx, (_, v_ref) in enumerate(parts):
            a = es[0][idx] * rs[0] - es[1][idx] * (lam * rs[1])
            o = jnp.dot(a.astype(BF16), v_ref[...], preferred_element_type=F32)
            out = o if out is None else out + o
        o_ref[...] = (_rms(out) * g_ref[...] * post_scale).astype(BF16)

    @pl.when(qt == 0)
    def _():
        finish([(scores(kc_ref), vc_ref)])

    @pl.when(qt > 0)
    def _():
        finish([(scores(kc_ref), vc_ref), (scores(kl_ref), vl_ref)])


def attention(lam, qk, v, subln_g, post_scale, batch, ctx_len, seq, with_ctx_queries):
    rows = qk.shape[0]
    tiles_per_seq = seq // ROW_TILE
    n_ctx_tiles = batch * ctx_len // ROW_TILE
    assert ctx_len == ROW_TILE
    first_q = 0 if with_ctx_queries else 1
    nq = tiles_per_seq + 1 - first_q

    def qblk(b, h, t):
        qt = t + first_q
        return jnp.where(qt == 0, b, n_ctx_tiles + b * tiles_per_seq + qt - 1)

    lat_blk0 = batch * ctx_len // seq
    kern = functools.partial(_attn_kernel, first_q=first_q, post_scale=post_scale)
    return pl.pallas_call(
        kern,
        grid=(batch, A_HEADS, nq),
        in_specs=[
            pl.BlockSpec(memory_space=pltpu.SMEM),
            pl.BlockSpec((ROW_TILE, LANES), lambda b, h, t: (qblk(b, h, t), h)),
            pl.BlockSpec((ctx_len, LANES), lambda b, h, t: (b, A_HEADS + h)),
            pl.BlockSpec((seq, LANES), lambda b, h, t: (lat_blk0 + b, A_HEADS + h)),
            pl.BlockSpec((ctx_len, LANES), lambda b, h, t: (b, h)),
            pl.BlockSpec((seq, LANES), lambda b, h, t: (lat_blk0 + b, h)),
            pl.BlockSpec((1, LANES), lambda b, h, t: (0, 0)),
        ],
        out_specs=pl.BlockSpec((ROW_TILE, LANES), lambda b, h, t: (qblk(b, h, t) - first_q * n_ctx_tiles, h)),
        out_shape=jax.ShapeDtypeStruct((rows - first_q * n_ctx_tiles * ROW_TILE, A_WIDTH), BF16),
        compiler_params=_params("arbitrary", "arbitrary", "arbitrary"),
        name="diff_attention",
    )(lam, qk, qk, qk, v, v, subln_g)


def _mlstm_kernel(qkc_ref, qkl_ref, vc_ref, vl_ref, oc_ref, ol_ref, gcc_ref, gcl_ref, grc_ref, grl_ref,
                  brow_ref, bcol_ref, mg_ref, outc_ref, outl_ref, c_ref, n_ref, hc_ref, hl_ref):
    L = MLSTM_CHUNK
    dh = LANES
    width = M_HEADS * dh
    r = lax.broadcasted_iota(jnp.int32, (L, L), 0)
    c = lax.broadcasted_iota(jnp.int32, (L, L), 1)
    c_ref[...] = jnp.zeros(c_ref.shape, F32)
    n_ref[...] = jnp.zeros(n_ref.shape, F32)

    for dirn in range(2):
        visible = (c <= r) if dirn == 0 else (c >= r)
        tri = jnp.where(visible, 1.0, 0.0)
        tri_t = jnp.where((r <= c) if dirn == 0 else (r >= c), 1.0, 0.0)

        def step(ci, ms, refs, dirn=dirn, visible=visible, tri=tri, tri_t=tri_t):
            qk_ref, v_ref, o_ref, gc_ref, gr_ref, h_ref, out_ref = refs
            rows = pl.ds(pl.multiple_of(ci * L, L), L)
            g_col = gc_ref[rows, :] + brow_ref[...]
            g_row = gr_ref[:, rows] + bcol_ref[...]
            ls_col = _log_sigmoid(g_col)
            ls_row = _log_sigmoid(g_row)
            b_cols = jnp.dot(tri, ls_col, precision=HIGHEST, preferred_element_type=F32)
            b_rows = jnp.dot(ls_row, tri_t, precision=HIGHEST, preferred_element_type=F32)
            new_ms = []
            for hd in range(M_HEADS):
                gi = dirn * 2 * M_HEADS + hd
                gf = gi + M_HEADS
                ch = dirn * M_HEADS + hd
                m = ms[hd]
                b_col = b_cols[:, gf:gf + 1]
                b_row = b_rows[gf:gf + 1, :]
                li_row = g_row[gi:gi + 1, :]
                li_col = g_col[:, gi:gi + 1]
                dmat = jnp.where(visible, b_col - b_row + li_row, -jnp.inf)
                inter = b_col + m
                m_t = jnp.maximum(inter, jnp.max(dmat, axis=1, keepdims=True))
                w_intra = jnp.exp(dmat - m_t)
                w_inter = jnp.exp(inter - m_t)
                q = qk_ref[rows, hd * dh:(hd + 1) * dh]
                k = qk_ref[rows, width + hd * dh: width + (hd + 1) * dh]
                v = v_ref[rows, hd * dh:(hd + 1) * dh]
                s = lax.dot_general(q, k, NT_DIMS, preferred_element_type=F32) * w_intra
                cs = c_ref[ch]
                ns = n_ref[ch]
                num = (jnp.dot(s.astype(BF16), v, preferred_element_type=F32)
                       + w_inter * jnp.dot(q, cs.astype(BF16), preferred_element_type=F32))
                den = (jnp.sum(s, axis=1, keepdims=True)
                       + w_inter * jnp.sum(q.astype(F32) * ns, axis=1, keepdims=True))
                hh = num / jnp.maximum(jnp.abs(den), jnp.exp(-m_t))
                b_last = jnp.sum(ls_row[gf:gf + 1, :], axis=1, keepdims=True)
                w_log = b_last - b_col + li_col
                m_new = jnp.maximum(b_last + m, jnp.max(w_log, axis=0, keepdims=True))
                w = jnp.exp(w_log - m_new)
                decay = jnp.exp(b_last + m - m_new)
                kw = k.astype(F32) * w
                c_ref[ch] = decay * cs + lax.dot_general(kw.astype(BF16), v, TN_DIMS, preferred_element_type=F32)
                n_ref[ch] = decay * ns + jnp.sum(kw, axis=0, keepdims=True)
                new_ms.append(m_new)
                cols = slice(hd * dh, (hd + 1) * dh)
                if dirn == 0:
                    h_ref[rows, cols] = hh
                else:
                    tot = h_ref[rows, cols] + hh
                    hn = _rms(tot) * mg_ref[:, cols]
                    out_ref[rows, cols] = (jax.nn.sigmoid(o_ref[rows, cols]) * hn).astype(BF16)
            return tuple(new_ms)

        ctx_refs = (qkc_ref, vc_ref, oc_ref, gcc_ref, grc_ref, hc_ref, outc_ref)
        lat_refs = (qkl_ref, vl_ref, ol_ref, gcl_ref, grl_ref, hl_ref, outl_ref)
        ms = tuple(jnp.zeros((1, 1), F32) for _ in range(M_HEADS))
        for refs in (ctx_refs, lat_refs):
            n_chunks = refs[0].shape[0] // L

            def body(i, ms, refs=refs, n_chunks=n_chunks, dirn=dirn):
                ci = i if dirn == 0 else n_chunks - 1 - i
                return step(ci, ms, refs)

            ms = lax.fori_loop(0, n_chunks, body, ms)


def mlstm(mqk, mv, mo, g_col, g_row, gate_b, mnorm_g, batch, ctx_len, seq):
    rows, w2 = mqk.shape
    width = w2 // 2
    n_gate = g_col.shape[1]
    lat0 = batch * ctx_len // seq
    ctx = lambda b: (b, 0)
    lat = lambda b: (lat0 + b, 0)
    const = lambda b: (0, 0)
    out_c, out_l = pl.pallas_call(
        _mlstm_kernel,
        grid=(batch,),
        in_specs=[
            pl.BlockSpec((ctx_len, w2), ctx), pl.BlockSpec((seq, w2), lat),
            pl.BlockSpec((ctx_len, width), ctx), pl.BlockSpec((seq, width), lat),
            pl.BlockSpec((ctx_len, width), ctx), pl.BlockSpec((seq, width), lat),
            pl.BlockSpec((ctx_len, n_gate), ctx), pl.BlockSpec((seq, n_gate), lat),
            pl.BlockSpec((n_gate, ctx_len), lambda b: (0, b)), pl.BlockSpec((n_gate, seq), lambda b: (0, lat0 + b)),
            pl.BlockSpec((1, n_gate), const), pl.BlockSpec((n_gate, 1), const),
            pl.BlockSpec((1, width), const),
        ],
        out_specs=[pl.BlockSpec((ctx_len, width), ctx), pl.BlockSpec((seq, width), ctx)],
        out_shape=[jax.ShapeDtypeStruct((batch * ctx_len, width), BF16),
                   jax.ShapeDtypeStruct((batch * seq, width), BF16)],
        scratch_shapes=[
            pltpu.VMEM((2 * M_HEADS, LANES, LANES), F32),
            pltpu.VMEM((2 * M_HEADS, 1, LANES), F32),
            pltpu.VMEM((ctx_len, width), F32),
            pltpu.VMEM((seq, width), F32),
        ],
        compiler_params=_params("arbitrary"),
        name="mlstm",
    )(mqk, mqk, mv, mv, mo, mo, g_col, g_col, g_row, g_row,
      gate_b.reshape(1, n_gate), gate_b.reshape(n_gate, 1), mnorm_g)
    return out_c, out_l


def _router_top2(logits):
    lane = lax.broadcasted_iota(jnp.int32, logits.shape, 1)
    logits = jnp.where(lane < N_EXPERTS, logits, -jnp.inf)
    v1 = jnp.max(logits, axis=1, keepdims=True)
    i1 = jnp.min(jnp.where(logits == v1, lane, LANES), axis=1, keepdims=True)
    rest = jnp.where(lane == i1, -jnp.inf, logits)
    v2 = jnp.max(rest, axis=1, keepdims=True)
    i2 = jnp.min(jnp.where(rest == v2, lane, LANES), axis=1, keepdims=True)
    e = jnp.exp(v2 - v1)
    w1 = 1.0 / (1.0 + e)
    w2 = e / (1.0 + e)
    mask = jnp.where(jnp.logical_or(lane == i1, lane == i2), 1.0, 0.0)
    expert = lane & (N_EXPERTS - 1)
    comb = jnp.where(lane < 3 * N_EXPERTS,
                     jnp.where(expert == i1, w1, 0.0) + jnp.where(expert == i2, w2, 0.0), 0.0)
    hi = comb.astype(BF16)
    rem = comb - hi.astype(F32)
    mid = rem.astype(BF16)
    lo = (rem - mid.astype(F32)).astype(BF16)
    wcols = jnp.where(lane < N_EXPERTS, hi, jnp.where(lane < 2 * N_EXPERTS, mid, lo))
    return mask, wcols


def _out_proj_kernel(*refs, with_router, n_ctx_tiles):
    if with_router:
        a_ref, m_ref, w_ref, x_ref, mod_ref, g_ref, rw_ref, x1_ref, h2_ref, sel_ref, wc_ref, cnt_ref = refs
        m = m_ref[...]
    else:
        a_ref, m_ref, mc_ref, w_ref, x_ref, mod_ref, g_ref, x1_ref, h2_ref = refs
        m = jnp.where(pl.program_id(0) < n_ctx_tiles, mc_ref[...], m_ref[...])
    half = a_ref.shape[1]
    mix = (jnp.dot(a_ref[...], w_ref[:half, :], preferred_element_type=F32)
           + jnp.dot(m, w_ref[half:, :], preferred_element_type=F32))
    x1 = x_ref[...] + mod_ref[0, 2:3, :] * mix
    x1_ref[...] = x1
    h2 = _rms(x1) * g_ref[...] * (1.0 + mod_ref[0, 4:5, :]) + mod_ref[0, 3:4, :]
    h2_ref[...] = h2.astype(BF16)
    if with_router:
        logits = jnp.dot(h2, rw_ref[...], precision=HIGHEST, preferred_element_type=F32)
        mask, wcols = _router_top2(logits)
        sel_ref[...] = mask.astype(BF16)
        wc_ref[...] = wcols
        cnt_ref[0] = jnp.sum(mask, axis=0, keepdims=True)


def out_proj(a, m_lat, m_ctx, w_out, xs, mods, norm_g, router_w, n_ctx_tiles, tiles_per_seq):
    rows, d = xs.shape
    half = a.shape[1]
    with_router = router_w is not None
    assert with_router == (m_ctx is None)
    first_tile = n_ctx_tiles if m_ctx is None else 0
    n_tiles = rows // ROW_TILE - first_tile
    out_rows = n_tiles * ROW_TILE

    def seg(i):
        t = i + first_tile
        return jnp.where(t < n_ctx_tiles, 0, 1 + (t - n_ctx_tiles) // tiles_per_seq)

    src = lambda i: (i + first_tile, 0)
    dst = lambda i: (i, 0)
    lat = lambda i: (jnp.maximum(i + first_tile - n_ctx_tiles, 0), 0)
    const = lambda i: (0, 0)
    in_specs = [pl.BlockSpec((ROW_TILE, half), dst), pl.BlockSpec((ROW_TILE, half), lat)]
    args = [a, m_lat]
    if m_ctx is not None:
        in_specs.append(pl.BlockSpec((ROW_TILE, half), lambda i: (jnp.minimum(i, n_ctx_tiles - 1), 0)))
        args.append(m_ctx)
    in_specs += [
        pl.BlockSpec(w_out.shape, const),
        pl.BlockSpec((ROW_TILE, d), src),
        pl.BlockSpec((1, 6, d), lambda i: (seg(i), 0, 0)),
        pl.BlockSpec((1, d), const),
    ]
    args += [w_out, xs, mods, norm_g]
    out_specs = [pl.BlockSpec((ROW_TILE, d), dst), pl.BlockSpec((ROW_TILE, d), dst)]
    out_shape = [jax.ShapeDtypeStruct((out_rows, d), F32), jax.ShapeDtypeStruct((out_rows, d), BF16)]
    if with_router:
        rw = jnp.pad(router_w, ((0, 0), (0, LANES - router_w.shape[1])))
        in_specs.append(pl.BlockSpec(rw.shape, const))
        out_specs += [pl.BlockSpec((ROW_TILE, LANES), dst), pl.BlockSpec((ROW_TILE, LANES), dst),
                      pl.BlockSpec((1, 1, LANES), lambda i: (i, 0, 0))]
        out_shape += [jax.ShapeDtypeStruct((out_rows, LANES), BF16),
                      jax.ShapeDtypeStruct((out_rows, LANES), BF16),
                      jax.ShapeDtypeStruct((n_tiles, 1, LANES), F32)]
        args.append(rw)
    return pl.pallas_call(
        functools.partial(_out_proj_kernel, with_router=with_router, n_ctx_tiles=n_ctx_tiles),
        grid=(n_tiles,),
        in_specs=in_specs, out_specs=out_specs, out_shape=out_shape,
        compiler_params=_params("arbitrary"),
        name="out_proj",
    )(*args)


def _ffn_kernel(h_ref, w1_ref, w3_ref, w2_ref, x_ref, mod_ref, o_ref, acc_ref):
    f = pl.program_id(1)

    @pl.when(f == 0)
    def _():
        acc_ref[...] = jnp.zeros(acc_ref.shape, F32)

    h = h_ref[...]
    a = _silu(jnp.dot(h, w1_ref[...], preferred_element_type=F32)) * jnp.dot(h, w3_ref[...], preferred_element_type=F32)
    acc_ref[...] += jnp.dot(a.astype(BF16), w2_ref[...], preferred_element_type=F32)

    @pl.when(f == pl.num_programs(1) - 1)
    def _():
        o_ref[...] = x_ref[...] + mod_ref[0, 5:6, :] * acc_ref[...]


def ffn(h2, w1, w3, w2, x1, mods, tm, tf, n_ctx_rows, rows_per_seq):
    rows, d = x1.shape
    dff = w1.shape[1]

    def seg(i):
        r0 = i * tm
        return jnp.where(r0 < n_ctx_rows, 0, 1 + (r0 - n_ctx_rows) // rows_per_seq)

    return pl.pallas_call(
        _ffn_kernel,
        grid=(rows // tm, dff // tf),
        in_specs=[
            pl.BlockSpec((tm, d), lambda i, f: (i, 0)),
            pl.BlockSpec((d, tf), lambda i, f: (0, f)),
            pl.BlockSpec((d, tf), lambda i, f: (0, f)),
            pl.BlockSpec((tf, d), lambda i, f: (f, 0)),
            pl.BlockSpec((tm, d), lambda i, f: (i, 0)),
            pl.BlockSpec((1, 6, d), lambda i, f: (seg(i), 0, 0)),
        ],
        out_specs=pl.BlockSpec((tm, d), lambda i, f: (i, 0)),
        out_shape=jax.ShapeDtypeStruct((rows, d), F32),
        scratch_shapes=[pltpu.VMEM((tm, d), F32)],
        compiler_params=_params("arbitrary", "arbitrary"),
        name="ffn",
    )(h2, w1, w3, w2, x1, mods)


MOE_TOKEN_TILE = 512
MOE_ROW_TILE = 512
MOE_FF_TILE = 512
SLOT_ALIGN = 16
STAGE_ROWS = 2 * MOE_TOKEN_TILE + LANES


def _pow2_chunks(max_chunks):
    k = 1
    while k * 2 <= max_chunks:
        k *= 2
    while k >= 1:
        yield k
        k //= 2


def _segment_copies(n_chunks, src, dst, src_ref, dst_ref, sem, max_chunks, action):
    for k in _pow2_chunks(max_chunks):
        rows = k * SLOT_ALIGN
        taken = (n_chunks & ~(2 * k - 1)) * SLOT_ALIGN

        @pl.when((n_chunks & k) != 0)
        def _(rows=rows, taken=taken):
            s = pl.multiple_of(src + taken, SLOT_ALIGN)
            t = pl.multiple_of(dst + taken, SLOT_ALIGN)
            cp = pltpu.make_async_copy(src_ref.at[pl.ds(s, rows), :], dst_ref.at[pl.ds(t, rows), :], sem)
            if action == "start":
                cp.start()
            else:
                cp.wait()


def _slot_one_hot(sel_ref, off_ref):
    sel = sel_ref[...]
    tt = sel.shape[0]
    r = lax.broadcasted_iota(jnp.int32, (tt, tt), 0)
    c = lax.broadcasted_iota(jnp.int32, (tt, tt), 1)
    earlier = jnp.where(c < r, 1.0, 0.0).astype(BF16)
    rank = jnp.dot(earlier, sel, preferred_element_type=F32)
    pos = off_ref[0] + rank
    chosen = sel > 0
    pa = jnp.min(jnp.where(chosen, pos, float(STAGE_ROWS)), axis=1, keepdims=True)
    pb = jnp.max(jnp.where(chosen, pos, -1.0), axis=1, keepdims=True)
    slot = lax.broadcasted_iota(jnp.int32, (tt, STAGE_ROWS), 1).astype(F32)
    return jnp.where(jnp.logical_or(slot == pa, slot == pb), 1.0, 0.0).astype(BF16)


def _dispatch_kernel(seg_start_ref, seg_chunks_ref, tile_off_ref, tail_start_ref, tail_chunks_ref, n_used_ref,
                     h_ref, wc_ref, sel_ref, off_ref, xs_ref, stage_ref, zero_ref, sem):
    t = pl.program_id(0)
    one_hot = _slot_one_hot(sel_ref, off_ref)
    payload = jnp.concatenate([h_ref[...], wc_ref[...]], axis=1)
    stage_ref[...] = lax.dot_general(one_hot, payload, TN_DIMS, preferred_element_type=F32).astype(BF16)
    max_chunks = MOE_TOKEN_TILE // SLOT_ALIGN
    for action in ("start", "wait"):
        for e in range(N_EXPERTS):
            idx = t * N_EXPERTS + e
            _segment_copies(seg_chunks_ref[idx], tile_off_ref[idx], seg_start_ref[idx],
                            stage_ref, xs_ref, sem, max_chunks, action)

    @pl.when(t == pl.num_programs(0) - 1)
    def _():
        zero_ref[...] = jnp.zeros(zero_ref.shape, BF16)
        for action in ("start", "wait"):
            for e in range(N_EXPERTS):
                _segment_copies(tail_chunks_ref[e], 0, tail_start_ref[e], zero_ref, xs_ref, sem,
                                MOE_ROW_TILE // SLOT_ALIGN - 1, action)

        def clear_tile(j, carry):
            rows = pl.ds(pl.multiple_of(j * MOE_ROW_TILE, MOE_ROW_TILE), MOE_ROW_TILE)
            cp = pltpu.make_async_copy(zero_ref, xs_ref.at[rows, :], sem)
            cp.start()
            cp.wait()
            return carry

        lax.fori_loop(n_used_ref[0], xs_ref.shape[0] // MOE_ROW_TILE, clear_tile, 0)


def moe_dispatch(plan, h2, wcols, sel, n_slots):
    rows, d = h2.shape
    tt = MOE_TOKEN_TILE
    width = d + LANES
    tile = lambda t, *_: (t, 0)
    grid_spec = pltpu.PrefetchScalarGridSpec(
        num_scalar_prefetch=6,
        grid=(rows // tt,),
        in_specs=[
            pl.BlockSpec((tt, d), tile),
            pl.BlockSpec((tt, LANES), tile),
            pl.BlockSpec((tt, LANES), tile),
            pl.BlockSpec((1, 1, LANES), lambda t, *_: (t, 0, 0)),
        ],
        out_specs=pl.BlockSpec(memory_space=pl.ANY),
        scratch_shapes=[pltpu.VMEM((STAGE_ROWS, width), BF16),
                        pltpu.VMEM((MOE_ROW_TILE, width), BF16),
                        pltpu.SemaphoreType.DMA],
    )
    return pl.pallas_call(
        _dispatch_kernel,
        grid_spec=grid_spec,
        out_shape=jax.ShapeDtypeStruct((n_slots, width), BF16),
        compiler_params=_params("arbitrary"),
        name="moe_dispatch",
    )(plan["seg_start"], plan["seg_chunks"], plan["tile_off"], plan["tail_start"], plan["tail_chunks"],
      plan["n_used"], h2, wcols, sel, plan["off_lanes"])


def _moe_kernel(tile_expert_ref, n_used_ref, xs_ref, w1_ref, w3_ref, w2_ref, ys_ref, acc_ref):
    i = pl.program_id(0)
    f = pl.program_id(1)

    @pl.when(i < n_used_ref[0])
    def _():
        @pl.when(f == 0)
        def _():
            acc_ref[...] = jnp.zeros(acc_ref.shape, F32)

        d = w1_ref.shape[1]
        x = xs_ref[:, :d]
        wc = xs_ref[:, d:].astype(F32)
        lane = lax.broadcasted_iota(jnp.int32, wc.shape, 1)
        mine = jnp.logical_and((lane & (N_EXPERTS - 1)) == tile_expert_ref[i], lane < 3 * N_EXPERTS)
        wt = jnp.sum(jnp.where(mine, wc, 0.0), axis=1, keepdims=True)
        a = _silu(jnp.dot(x, w1_ref[0], preferred_element_type=F32)) * jnp.dot(x, w3_ref[0], preferred_element_type=F32)
        acc_ref[...] += jnp.dot((a * wt).astype(BF16), w2_ref[0], preferred_element_type=F32)

        @pl.when(f == pl.num_programs(1) - 1)
        def _():
            ys_ref[...] = acc_ref[...].astype(BF16)

    @pl.when(jnp.logical_and(i >= n_used_ref[0], f == 0))
    def _():
        ys_ref[...] = jnp.zeros(ys_ref.shape, BF16)


def moe_experts(plan, xs, w1, w3, w2):
    n_slots, width = xs.shape
    ne, d, dff = w1.shape
    tm, tf = MOE_ROW_TILE, MOE_FF_TILE
    n_f = dff // tf

    def row(i, f, te, nu):
        return jnp.minimum(i, nu[0] - 1)

    def ff(i, f, te, nu):
        return jnp.where(i < nu[0], f, n_f - 1)

    grid_spec = pltpu.PrefetchScalarGridSpec(
        num_scalar_prefetch=2,
        grid=(n_slots // tm, n_f),
        in_specs=[
            pl.BlockSpec((tm, width), lambda i, f, te, nu: (row(i, f, te, nu), 0)),
            pl.BlockSpec((1, d, tf), lambda i, f, te, nu: (te[row(i, f, te, nu)], 0, ff(i, f, te, nu))),
            pl.BlockSpec((1, d, tf), lambda i, f, te, nu: (te[row(i, f, te, nu)], 0, ff(i, f, te, nu))),
            pl.BlockSpec((1, tf, d), lambda i, f, te, nu: (te[row(i, f, te, nu)], ff(i, f, te, nu), 0)),
        ],
        out_specs=pl.BlockSpec((tm, d), lambda i, f, te, nu: (i, 0)),
        scratch_shapes=[pltpu.VMEM((tm, d), F32)],
    )
    return pl.pallas_call(
        _moe_kernel,
        grid_spec=grid_spec,
        out_shape=jax.ShapeDtypeStruct((n_slots, d), BF16),
        compiler_params=_params("arbitrary", "arbitrary"),
        name="moe_experts",
    )(plan["tile_expert"], plan["n_used"], xs, w1, w3, w2)


def _combine_kernel(seg_start_ref, seg_chunks_ref, tile_off_ref,
                    ys_ref, sel_ref, off_ref, x_ref, mod_ref, o_ref, ybuf_ref, sem):
    t = pl.program_id(0)

    @pl.when(t == 0)
    def _():
        ybuf_ref[...] = jnp.zeros(ybuf_ref.shape, BF16)

    max_chunks = MOE_TOKEN_TILE // SLOT_ALIGN

    def copies(action):
        for e in range(N_EXPERTS):
            idx = t * N_EXPERTS + e
            _segment_copies(seg_chunks_ref[idx], seg_start_ref[idx], tile_off_ref[idx],
                            ys_ref, ybuf_ref, sem, max_chunks, action)

    copies("start")
    one_hot = _slot_one_hot(sel_ref, off_ref)
    copies("wait")
    y = jnp.dot(one_hot, ybuf_ref[...], preferred_element_type=F32)
    o_ref[...] = x_ref[...] + mod_ref[0, 5:6, :] * y


def moe_combine(plan, ys, sel, x1, mods, rows_per_seq):
    rows, d = x1.shape
    tt = MOE_TOKEN_TILE
    tile = lambda t, *_: (t, 0)
    grid_spec = pltpu.PrefetchScalarGridSpec(
        num_scalar_prefetch=3,
        grid=(rows // tt,),
        in_specs=[
            pl.BlockSpec(memory_space=pl.ANY),
            pl.BlockSpec((tt, LANES), tile),
            pl.BlockSpec((1, 1, LANES), lambda t, *_: (t, 0, 0)),
            pl.BlockSpec((tt, d), tile),
            pl.BlockSpec((1, 6, d), lambda t, *_: (1 + (t * tt) // rows_per_seq, 0, 0)),
        ],
        out_specs=pl.BlockSpec((tt, d), tile),
        scratch_shapes=[pltpu.VMEM((STAGE_ROWS, d), BF16), pltpu.SemaphoreType.DMA],
    )
    return pl.pallas_call(
        _combine_kernel,
        grid_spec=grid_spec,
        out_shape=jax.ShapeDtypeStruct((rows, d), F32),
        compiler_params=_params("arbitrary"),
        name="moe_combine",
    )(plan["seg_start"], plan["seg_chunks"], plan["tile_off"], ys, sel, plan["off_lanes"], x1, mods)


def _moe_plan(counts, n_row_tiles):
    counts = counts.astype(jnp.int32)
    seg = (counts + SLOT_ALIGN - 1) // SLOT_ALIGN * SLOT_ALIGN
    tile_off = jnp.cumsum(seg, axis=1) - seg
    total = jnp.sum(seg, axis=0)
    region = (total + MOE_ROW_TILE - 1) // MOE_ROW_TILE * MOE_ROW_TILE
    region_end = jnp.cumsum(region)
    region_start = region_end - region
    seg_start = region_start[None, :] + jnp.cumsum(seg, axis=0) - seg
    first_row = jnp.arange(n_row_tiles, dtype=jnp.int32) * MOE_ROW_TILE
    tile_expert = jnp.minimum(jnp.sum(first_row[:, None] >= region_end[None, :], axis=1), counts.shape[1] - 1)
    off_lanes = jnp.pad(tile_off.astype(F32), ((0, 0), (0, LANES - counts.shape[1])))
    return {
        "seg_start": seg_start.reshape(-1),
        "seg_chunks": (seg // SLOT_ALIGN).reshape(-1),
        "tile_off": tile_off.reshape(-1),
        "tail_start": region_start + total,
        "tail_chunks": (region - total) // SLOT_ALIGN,
        "tile_expert": tile_expert.astype(jnp.int32),
        "n_used": (region_end[-1:] // MOE_ROW_TILE).astype(jnp.int32),
        "off_lanes": off_lanes.reshape(counts.shape[0], 1, LANES),
    }


def _rope_tables(ctx_len, seq):
    rows = seq // GRID_W
    row = jnp.repeat(jnp.arange(rows, dtype=F32), GRID_W)
    col = jnp.tile(jnp.arange(GRID_W, dtype=F32), rows)
    inv = ROPE_BASE ** (-jnp.arange(ROPE_PAIRS_PER_AXIS, dtype=F32) / ROPE_PAIRS_PER_AXIS)
    ang = jnp.concatenate([row[:, None] * inv, col[:, None] * inv], axis=-1)
    cos, sin = jnp.cos(ang), jnp.sin(ang)
    cos_t = jnp.concatenate([cos, cos, cos, cos], axis=-1)
    sin_t = jnp.concatenate([-sin, sin, -sin, sin], axis=-1)
    cos_t = jnp.concatenate([jnp.ones((ctx_len, LANES), F32), cos_t], axis=0)
    sin_t = jnp.concatenate([jnp.zeros((ctx_len, LANES), F32), sin_t], axis=0)
    return cos_t, sin_t


def kernel(x, c, ctx, c_ctx, ada_w, ada_b, norm1_g, norm2_g, w_in, w_out, q_norm_g, k_norm_g, lambda_q1, lambda_k1, lambda_q2, lambda_k2, subln_g, conv_w, conv_b, gate_b, mnorm_g, ffn_w1, ffn_w3, ffn_w2, router_w, moe_w1, moe_w3, moe_w2):
    batch, seq, d = x.shape
    ctx_len = ctx.shape[1]
    depth = ada_w.shape[0]
    assert depth == 2, "layer 0 runs the dense FFN on all rows, layer 1 the experts on the latents"
    n_ctx_rows = batch * ctx_len
    n_ctx_tiles = n_ctx_rows // ROW_TILE
    tiles_per_seq = seq // ROW_TILE
    n_main = w_in.shape[2] - gate_b[0].size
    m_dh = (n_main - 3 * A_WIDTH) // 4 // M_HEADS

    xs = jnp.concatenate([ctx.reshape(n_ctx_rows, d), x.reshape(batch * seq, d)], axis=0)

    cc = jnp.zeros((2 * SUBLANES, d), F32).at[:batch].set(c).at[batch].set(c_ctx)
    mod_rows = adaln_all(cc, ada_w, ada_b)
    mods_all = jnp.concatenate([mod_rows[:, batch:batch + 1], mod_rows[:, :batch]], axis=1)
    mods_all = mods_all.reshape(depth, batch + 1, 6, d)

    cos_t, sin_t = _rope_tables(ctx_len, seq)

    for l in range(depth):
        last = l == depth - 1
        lam_init = 0.8 - 0.6 * math.exp(-0.3 * l)
        lam = (jnp.exp(jnp.sum(lambda_q1[l] * lambda_k1[l]).astype(F32))
               - jnp.exp(jnp.sum(lambda_q2[l] * lambda_k2[l]).astype(F32)) + lam_init)
        mods = mods_all[l]
        w_main = w_in[l, :, :n_main].astype(BF16)
        w_gate = w_in[l, :, n_main:]
        qg = jnp.tile(q_norm_g[l] * (A_DH ** -0.5), 2).reshape(1, LANES)
        kg = jnp.tile(k_norm_g[l], 2).reshape(1, LANES)

        qk, av, mqk, mv, mo, g_col, g_row = proj_in(
            xs, mods, norm1_g[l].reshape(1, d), w_main, w_gate, w_gate.T, cos_t, sin_t, qg, kg,
            n_ctx_tiles, tiles_per_seq)
        mqk_c = conv_silu(mqk, conv_w[l], conv_b[l].reshape(1, -1), n_ctx_tiles, tiles_per_seq, m_dh ** -0.5)
        a_out = attention(lam.reshape(1, 1), qk, av, subln_g[l].reshape(1, LANES), 1.0 - lam_init,
                          batch, ctx_len, seq, with_ctx_queries=not last)
        m_ctx, m_out = mlstm(mqk_c, mv, mo, g_col, g_row, gate_b[l], mnorm_g[l].reshape(1, -1),
                             batch, ctx_len, seq)
        w_out_b = w_out[l].astype(BF16)
        if not last:
            x1, h2 = out_proj(a_out, m_out, m_ctx, w_out_b, xs, mods, norm2_g[l].reshape(1, d), None,
                              n_ctx_tiles, tiles_per_seq)
            j = l // 2
            xs = ffn(h2, ffn_w1[j].astype(BF16), ffn_w3[j].astype(BF16), ffn_w2[j].astype(BF16), x1, mods,
                     512, 1408, n_ctx_rows, seq)
        else:
            j = l // 2
            x1, h2, sel, wcols, cnt = out_proj(a_out, m_out, None, w_out_b, xs, mods, norm2_g[l].reshape(1, d),
                                               router_w[j], n_ctx_tiles, tiles_per_seq)
            n_tok_tiles = x1.shape[0] // MOE_TOKEN_TILE
            counts = cnt[:, 0, :N_EXPERTS].reshape(n_tok_tiles, MOE_TOKEN_TILE // ROW_TILE, N_EXPERTS).sum(axis=1)
            worst = 2 * x1.shape[0] + n_tok_tiles * N_EXPERTS * (SLOT_ALIGN - 1) + N_EXPERTS * (MOE_ROW_TILE - SLOT_ALIGN)
            n_row_tiles = -(-worst // MOE_ROW_TILE)
            plan = _moe_plan(counts, n_row_tiles)
            slots = moe_dispatch(plan, h2, wcols, sel, n_row_tiles * MOE_ROW_TILE)
            ys = moe_experts(plan, slots, moe_w1[j].astype(BF16), moe_w3[j].astype(BF16), moe_w2[j].astype(BF16))
            out = moe_combine(plan, ys, sel, x1, mods, seq)
    return out.reshape(batch, seq, d)
```

```python
import functools
import math

import jax
import jax.numpy as jnp
from jax import lax
from jax.experimental import pallas as pl
from jax.experimental.pallas import tpu as pltpu

F32 = jnp.float32
BF16 = jnp.bfloat16
HIGHEST = lax.Precision.HIGHEST

GRID_W = 64
A_HEADS = 4
A_DH = 64
A_DV = 2 * A_DH
A_WIDTH = A_HEADS * A_DV
M_HEADS = 4
N_GATES = 4 * M_HEADS
CONV_K = 5
ROPE_BASE = 10000.0
ROPE_PAIRS_PER_AXIS = A_DH // 4
N_EXPERTS = 8
EPS = 1e-6

LANES = 128
SUBLANES = 8
ROW_TILE = 256
MLSTM_CHUNK = 128
VMEM_LIMIT = 56 * 1024 * 1024

NT_DIMS = (((1,), (1,)), ((), ()))
TN_DIMS = (((0,), (0,)), ((), ()))

GATE_ORDER = tuple(d * 2 * M_HEADS + k * M_HEADS + h for k in range(2) for d in range(2) for h in range(M_HEADS))


def _params(*sem):
    return pltpu.CompilerParams(dimension_semantics=sem, vmem_limit_bytes=VMEM_LIMIT)


def _silu(v):
    return v * jax.nn.sigmoid(v)


def _log_sigmoid(v):
    return jnp.minimum(v, 0.0) - jnp.log1p(jnp.exp(-jnp.abs(v)))


def _rms(v, axis=-1):
    return v * lax.rsqrt(jnp.mean(v * v, axis=axis, keepdims=True) + EPS)


def _split_bf16(w):
    hi = w.astype(BF16)
    return hi, (w - hi.astype(F32)).astype(BF16)


def _adaln_kernel(c_ref, w_ref, b_ref, o_ref):
    s = _silu(c_ref[...])
    o_ref[0] = jnp.dot(s, w_ref[0], precision=HIGHEST, preferred_element_type=F32) + b_ref[0]


def adaln_all(cc, ada_w, ada_b):
    depth, d, d6 = ada_w.shape
    n_col = d6 // d
    return pl.pallas_call(
        _adaln_kernel,
        grid=(depth, n_col),
        in_specs=[
            pl.BlockSpec(cc.shape, lambda l, j: (0, 0)),
            pl.BlockSpec((1, d, d), lambda l, j: (l, 0, j)),
            pl.BlockSpec((1, 1, d), lambda l, j: (l, 0, j)),
        ],
        out_specs=pl.BlockSpec((1, cc.shape[0], d), lambda l, j: (l, 0, j)),
        out_shape=jax.ShapeDtypeStruct((depth, cc.shape[0], d6), F32),
        compiler_params=_params("arbitrary", "arbitrary"),
        name="adaln",
    )(cc, ada_w, ada_b.reshape(depth, 1, d6))


def _proj_in_kernel(xa_ref, xb_ref, prev_ref, next_ref, mod_ref, g_ref, w_ref, wgh_ref, cw_ref, cb_ref,
                    cos_ref, sin_ref, qg_ref, kg_ref,
                    qk_ref, v_ref, mq_ref, mkt_ref, mv_ref, mo_ref, gr_ref, ext_ref,
                    *, n_ctx_tiles, tiles_per_seq, k_scale):
    i = pl.program_id(0)
    is_ctx = i < n_ctx_tiles
    pos = (i - n_ctx_tiles) % tiles_per_seq
    first = jnp.logical_or(is_ctx, pos == 0)
    last = jnp.logical_or(is_ctx, pos == tiles_per_seq - 1)
    tm = xa_ref.shape[0]
    halo = SUBLANES

    x_ext = jnp.concatenate([prev_ref[...], jnp.where(is_ctx, xa_ref[...], xb_ref[...]), next_ref[...]], axis=0)
    h_ext = _rms(x_ext) * g_ref[...] * (1.0 + mod_ref[0, 1:2, :]) + mod_ref[0, 0:1, :]
    h = h_ext[halo:halo + tm]
    hb = h.astype(BF16)

    def proj(j):
        return jnp.dot(hb, w_ref[:, j * A_WIDTH:(j + 1) * A_WIDTH], preferred_element_type=F32)

    r = lax.broadcasted_iota(jnp.int32, (LANES, LANES), 0)
    c = lax.broadcasted_iota(jnp.int32, (LANES, LANES), 1)
    group_ones = jnp.where((r & -A_DH) == (c & -A_DH), 1.0, 0.0).astype(BF16)
    lane = lax.broadcasted_iota(jnp.int32, (1, LANES), 1)
    first_half = (lane & (A_DH - 1)) < (A_DH // 2)
    cos = cos_ref[...]
    sin = sin_ref[...]
    for j, gain_ref in ((0, qg_ref), (1, kg_ref)):
        acc = proj(j)
        for hd in range(A_HEADS):
            t = acc[:, hd * LANES:(hd + 1) * LANES]
            ss = jnp.dot((t * t).astype(BF16), group_ones, preferred_element_type=F32)
            tn = t * lax.rsqrt(ss * (1.0 / A_DH) + EPS) * gain_ref[...]
            rot = jnp.where(first_half, pltpu.roll(tn, LANES - A_DH // 2, 1), pltpu.roll(tn, A_DH // 2, 1))
            qk_ref[:, j * A_WIDTH + hd * LANES: j * A_WIDTH + (hd + 1) * LANES] = (tn * cos + rot * sin).astype(BF16)
    v_ref[...] = proj(2).astype(BF16)

    pre = jnp.dot(h_ext.astype(BF16), w_ref[:, 3 * A_WIDTH:5 * A_WIDTH], preferred_element_type=F32)
    ext_ref[0:halo, :] = jnp.where(first, 0.0, pre[0:halo])
    ext_ref[halo:halo + tm, :] = pre[halo:halo + tm]
    ext_ref[halo + tm:, :] = jnp.where(last, 0.0, pre[halo + tm:])
    acc = jnp.zeros((tm, 2 * A_WIDTH), F32) + cb_ref[...]
    pad = CONV_K // 2
    for j in range(CONV_K):
        acc = acc + cw_ref[j:j + 1, :] * ext_ref[halo - pad + j:halo - pad + j + tm, :]
    y = _silu(acc)
    mq_ref[...] = y[:, :A_WIDTH].astype(BF16)
    for ci in range(tm // MLSTM_CHUNK):
        yk = y[ci * MLSTM_CHUNK:(ci + 1) * MLSTM_CHUNK, A_WIDTH:] * k_scale
        mkt_ref[ci] = yk.T.astype(BF16)

    mv_ref[...] = proj(5).astype(BF16)
    mo_ref[...] = proj(6)

    n_main = 7 * A_WIDTH
    p = jnp.dot(hb, w_ref[:, n_main:n_main + 2 * LANES], preferred_element_type=F32)
    h_lo = (h - hb.astype(F32)).astype(BF16)
    gates = p[:, :LANES] + p[:, LANES:] + jnp.dot(h_lo, wgh_ref[...], preferred_element_type=F32)
    gr_ref[...] = gates.T[:N_GATES, :]


def proj_in(xa, xb, b_first_tile, mods, norm_g, w_all, w_gate_hi, conv_w, conv_b, cos_t, sin_t, qg, kg,
            n_ctx_tiles, tiles_per_seq, k_scale):
    d = xa.shape[1]
    n_lat_tiles = xb.shape[0] // ROW_TILE - b_first_tile
    n_tiles = n_ctx_tiles + n_lat_tiles
    rows = n_tiles * ROW_TILE
    per = ROW_TILE // SUBLANES

    def seg(i):
        return jnp.where(i < n_ctx_tiles, 0, 1 + (i - n_ctx_tiles) // tiles_per_seq)

    def rope_blk(i):
        return jnp.where(i < n_ctx_tiles, 0, 1 + (i - n_ctx_tiles) % tiles_per_seq)

    def lat(i):
        return jnp.maximum(i - n_ctx_tiles, 0)

    row = lambda i: (i, 0)
    const = lambda i: (0, 0)
    kern = functools.partial(_proj_in_kernel, n_ctx_tiles=n_ctx_tiles, tiles_per_seq=tiles_per_seq, k_scale=k_scale)
    return pl.pallas_call(
        kern,
        grid=(n_tiles,),
        in_specs=[
            pl.BlockSpec((ROW_TILE, d), lambda i: (jnp.minimum(i, n_ctx_tiles - 1), 0)),
            pl.BlockSpec((ROW_TILE, d), lambda i: (lat(i) + b_first_tile, 0)),
            pl.BlockSpec((SUBLANES, d), lambda i: (jnp.maximum(lat(i) * per - 1, 0) + b_first_tile * per, 0)),
            pl.BlockSpec((SUBLANES, d),
                         lambda i: (jnp.minimum((lat(i) + 1) * per, n_lat_tiles * per - 1) + b_first_tile * per, 0)),
            pl.BlockSpec((1, 6, d), lambda i: (seg(i), 0, 0)),
            pl.BlockSpec((1, d), const),
            pl.BlockSpec(w_all.shape, const),
            pl.BlockSpec(w_gate_hi.shape, const),
            pl.BlockSpec(conv_w.shape, const),
            pl.BlockSpec(conv_b.shape, const),
            pl.BlockSpec((ROW_TILE, LANES), lambda i: (rope_blk(i), 0)),
            pl.BlockSpec((ROW_TILE, LANES), lambda i: (rope_blk(i), 0)),
            pl.BlockSpec((1, LANES), const),
            pl.BlockSpec((1, LANES), const),
        ],
        out_specs=[
            pl.BlockSpec((ROW_TILE, 2 * A_WIDTH), row),
            pl.BlockSpec((ROW_TILE, A_WIDTH), row),
            pl.BlockSpec((ROW_TILE, A_WIDTH), row),
            pl.BlockSpec((ROW_TILE // MLSTM_CHUNK, A_WIDTH, MLSTM_CHUNK), lambda i: (i, 0, 0)),
            pl.BlockSpec((ROW_TILE, A_WIDTH), row),
            pl.BlockSpec((ROW_TILE, A_WIDTH), row),
            pl.BlockSpec((N_GATES, ROW_TILE), lambda i: (0, i)),
        ],
        out_shape=[
            jax.ShapeDtypeStruct((rows, 2 * A_WIDTH), BF16),
            jax.ShapeDtypeStruct((rows, A_WIDTH), BF16),
            jax.ShapeDtypeStruct((rows, A_WIDTH), BF16),
            jax.ShapeDtypeStruct((rows // MLSTM_CHUNK, A_WIDTH, MLSTM_CHUNK), BF16),
            jax.ShapeDtypeStruct((rows, A_WIDTH), BF16),
            jax.ShapeDtypeStruct((rows, A_WIDTH), F32),
            jax.ShapeDtypeStruct((N_GATES, rows), F32),
        ],
        scratch_shapes=[pltpu.VMEM((ROW_TILE + 2 * SUBLANES, 2 * A_WIDTH), F32)],
        compiler_params=_params("arbitrary"),
        name="proj_in",
    )(xa, xb, xb, xb, mods, norm_g, w_all, w_gate_hi, conv_w, conv_b, cos_t, sin_t, qg, kg)


ATTN_KEY_BLOCK = 256


def _attn_kernel(lam_ref, q_ref, kc_ref, kl_ref, vc_ref, vl_ref, g_ref, o_ref, s_ref, acc_ref, *, first_q, post_scale):
    qt = pl.program_id(2) + first_q
    lam = lam_ref[0, 0]
    q = q_ref[...]
    lane = lax.broadcasted_iota(jnp.int32, (1, LANES), 1)
    q1 = jnp.where(lane < A_DH, q, jnp.zeros_like(q))
    q2 = jnp.where(lane >= A_DH, q, jnp.zeros_like(q))

    tq = q.shape[0]
    kb = ATTN_KEY_BLOCK

    def attend(blocks):
        run_max = [jnp.full((tq, LANES), -jnp.inf, F32) for _ in range(2)]
        for j, (k_ref, _, r0) in enumerate(blocks):
            k = k_ref[r0:r0 + kb, :]
            for m, qm in enumerate((q1, q2)):
                s = lax.dot_general(qm, k, NT_DIMS, preferred_element_type=F32)
                s_ref[m, :, j * kb:(j + 1) * kb] = s
                for c0 in range(0, kb, LANES):
                    run_max[m] = jnp.maximum(run_max[m], s[:, c0:c0 + LANES])
        mx = [jnp.broadcast_to(jnp.max(rm, axis=1, keepdims=True), (tq, kb)) for rm in run_max]
        part = [jnp.zeros((tq, LANES), F32) for _ in range(2)]
        for j, (_, v_ref, r0) in enumerate(blocks):
            v = v_ref[r0:r0 + kb, :]
            for m in range(2):
                e = jnp.exp(s_ref[m, :, j * kb:(j + 1) * kb] - mx[m])
                for c0 in range(0, kb, LANES):
                    part[m] = part[m] + e[:, c0:c0 + LANES]
                pv = jnp.dot(e.astype(BF16), v, preferred_element_type=F32)
                acc_ref[m] = pv if j == 0 else acc_ref[m] + pv
        z = [jnp.sum(p, axis=1, keepdims=True) for p in part]
        out = acc_ref[0] * (1.0 / z[0]) - acc_ref[1] * (lam / z[1])
        o_ref[...] = (_rms(out) * g_ref[...] * post_scale).astype(BF16)

    ctx_blocks = [(kc_ref, vc_ref, r0) for r0 in range(0, kc_ref.shape[0], kb)]
    lat_blocks = [(kl_ref, vl_ref, r0) for r0 in range(0, kl_ref.shape[0], kb)]

    @pl.when(qt == 0)
    def _():
        attend(ctx_blocks)

    @pl.when(qt > 0)
    def _():
        attend(ctx_blocks + lat_blocks)


def attention(lam, qk, v, subln_g, post_scale, batch, ctx_len, seq, with_ctx_queries):
    rows = qk.shape[0]
    tiles_per_seq = seq // ROW_TILE
    n_ctx_tiles = batch * ctx_len // ROW_TILE
    assert ctx_len == ROW_TILE
    first_q = 0 if with_ctx_queries else 1
    nq = tiles_per_seq + 1 - first_q

    def qblk(b, h, t):
        qt = t + first_q
        return jnp.where(qt == 0, b, n_ctx_tiles + b * tiles_per_seq + qt - 1)

    lat_blk0 = batch * ctx_len // seq
    kern = functools.partial(_attn_kernel, first_q=first_q, post_scale=post_scale)
    return pl.pallas_call(
        kern,
        grid=(batch, A_HEADS, nq),
        in_specs=[
            pl.BlockSpec(memory_space=pltpu.SMEM),
            pl.BlockSpec((ROW_TILE, LANES), lambda b, h, t: (qblk(b, h, t), h)),
            pl.BlockSpec((ctx_len, LANES), lambda b, h, t: (b, A_HEADS + h)),
            pl.BlockSpec((seq, LANES), lambda b, h, t: (lat_blk0 + b, A_HEADS + h)),
            pl.BlockSpec((ctx_len, LANES), lambda b, h, t: (b, h)),
            pl.BlockSpec((seq, LANES), lambda b, h, t: (lat_blk0 + b, h)),
            pl.BlockSpec((1, LANES), lambda b, h, t: (0, 0)),
        ],
        out_specs=pl.BlockSpec((ROW_TILE, LANES), lambda b, h, t: (qblk(b, h, t) - first_q * n_ctx_tiles, h)),
        out_shape=jax.ShapeDtypeStruct((rows - first_q * n_ctx_tiles * ROW_TILE, A_WIDTH), BF16),
        scratch_shapes=[pltpu.VMEM((2, ROW_TILE, ctx_len + seq), F32),
                        pltpu.VMEM((2, ROW_TILE, LANES), F32)],
        compiler_params=_params("arbitrary", "arbitrary", "arbitrary"),
        name="diff_attention",
    )(lam, qk, qk, qk, v, v, subln_g)


def _mlstm_kernel(qc_ref, ql_ref, ktc_ref, ktl_ref, vc_ref, vl_ref, oc_ref, ol_ref, grc_ref, grl_ref, gb_ref, mg_ref,
                  outc_ref, outl_ref, row_ref, col_ref, s_ref, hf_ref, hb_ref):
    L = MLSTM_CHUNK
    dh = LANES
    n_ch = 2 * M_HEADS
    n_ctx = qc_ref.shape[0] // L
    n_lat = ql_ref.shape[0] // L
    n_chunks = n_ctx + n_lat
    nt = n_chunks * L

    g = jnp.concatenate([grc_ref[...], grl_ref[...]], axis=1) + gb_ref[...]
    log_i = g[:n_ch]
    log_f = _log_sigmoid(g[n_ch:])
    sub = lax.broadcasted_iota(jnp.int32, (n_ch, nt), 0)
    lane_in_chunk = lax.broadcasted_iota(jnp.int32, (n_ch, nt), 1) & (L - 1)
    fwd_rows = sub < M_HEADS

    def chunk_scan(x, op, ident):
        xf, xb, k = x, x, 1
        while k < L:
            xf = op(xf, jnp.where(lane_in_chunk >= k, pltpu.roll(xf, k, 1), ident))
            xb = op(xb, jnp.where(lane_in_chunk < L - k, pltpu.roll(xb, nt - k, 1), ident))
            k *= 2
        return jnp.where(fwd_rows, xf, xb)

    b = chunk_scan(log_f, jnp.add, 0.0)
    a = log_i - b
    a_run = chunk_scan(a, jnp.maximum, -jnp.inf)

    fwd_col = lax.broadcasted_iota(jnp.int32, (n_ch, 1), 0) < M_HEADS

    def at_chunk_end(x, ci):
        xc = x[:, ci * L:(ci + 1) * L]
        return jnp.where(fwd_col, xc[:, L - 1:L], xc[:, 0:1])

    b_last = [at_chunk_end(b, ci) for ci in range(n_chunks)]
    a_last = [at_chunk_end(a_run, ci) for ci in range(n_chunks)]
    order_f = list(range(n_chunks))
    order_b = list(range(n_ctx - 1, -1, -1)) + list(range(n_chunks - 1, n_ctx - 1, -1))

    def carried(order):
        m, out = jnp.zeros((n_ch, 1), F32), {}
        for ci in order:
            out[ci] = m
            m = b_last[ci] + jnp.maximum(m, a_last[ci])
        return out

    m_f, m_b = carried(order_f), carried(order_b)
    for ci in range(n_chunks):
        m_prev = jnp.where(fwd_col, m_f[ci], m_b[ci])
        sl = slice(ci * L, (ci + 1) * L)
        mu = jnp.maximum(m_prev, a_run[:, sl])
        mu_last = jnp.maximum(m_prev, a_last[ci])
        row_ref[ci] = jnp.concatenate([a[:, sl], jnp.exp(a[:, sl] - mu_last), jnp.broadcast_to(m_prev, (n_ch, L)),
                                       jnp.broadcast_to(jnp.exp(m_prev - mu_last), (n_ch, L))], axis=0)
        tile = jnp.concatenate([mu, b[:, sl] + mu, jnp.zeros((L - 2 * n_ch, L), F32)], axis=0)
        col_ref[ci] = tile.T

    s_ref[...] = jnp.zeros(s_ref.shape, F32)
    rr = lax.broadcasted_iota(jnp.int32, (L, L), 0)
    cc = lax.broadcasted_iota(jnp.int32, (L, L), 1)
    visible = (cc <= rr, cc >= rr)
    ones = jnp.ones((L, dh), BF16)

    def chunk_rows(ci):
        return pl.ds(ci * L if isinstance(ci, int) else pl.multiple_of(ci * L, L), L)

    def twice(t):
        return jnp.concatenate([t, t], axis=1)

    def step(local, glob, q_ref, kt_ref, v_ref):
        for d in range(2):
            rows = chunk_rows(local[d])
            grow = chunk_rows(glob[d])
            cs = col_ref[glob[d]]
            rs = row_ref[glob[d]]
            h_ref = hf_ref if d == 0 else hb_ref
            for hd in range(M_HEADS):
                r = d * M_HEADS + hd
                mu = jnp.broadcast_to(cs[:, r:r + 1], (L, L))
                m_t = jnp.broadcast_to(cs[:, n_ch + r:n_ch + r + 1], (L, dh))
                a_row, w_row, m_prev, decay = (rs[j * n_ch + r:j * n_ch + r + 1, :] for j in range(4))
                w_intra = jnp.exp(jnp.where(visible[d], a_row - mu, -jnp.inf))
                w_inter = jnp.exp(m_prev - mu)
                cols = slice(hd * dh, (hd + 1) * dh)
                q = q_ref[rows, cols]
                kt = kt_ref[local[d], cols, :]
                v_ext = jnp.concatenate([v_ref[rows, cols], ones], axis=1)
                s = (jnp.dot(q, kt, preferred_element_type=F32) * w_intra).astype(BF16)
                state = s_ref[r]
                t = (jnp.dot(s, v_ext, preferred_element_type=F32)
                     + twice(w_inter) * jnp.dot(q, state.astype(BF16), preferred_element_type=F32))
                h_ref[grow, cols] = t[:, :dh] / jnp.maximum(jnp.abs(t[:, dh:]), jnp.exp(-m_t))
                ktw = (kt.astype(F32) * w_row).astype(BF16)
                s_ref[r] = twice(jnp.broadcast_to(decay, (dh, L))) * state + jnp.dot(ktw, v_ext, preferred_element_type=F32)

    for i in range(n_ctx):
        step((i, n_ctx - 1 - i), (i, n_ctx - 1 - i), qc_ref, ktc_ref, vc_ref)

    def lat_step(i, carry):
        step((i, n_lat - 1 - i), (n_ctx + i, n_chunks - 1 - i), ql_ref, ktl_ref, vl_ref)
        return carry

    lax.fori_loop(0, n_lat, lat_step, 0)

    def finish(local, glob, o_ref, out_ref):
        rows = chunk_rows(local)
        grow = chunk_rows(glob)
        for hd in range(M_HEADS):
            cols = slice(hd * dh, (hd + 1) * dh)
            hn = _rms(hf_ref[grow, cols] + hb_ref[grow, cols]) * mg_ref[:, cols]
            out_ref[rows, cols] = (jax.nn.sigmoid(o_ref[rows, cols]) * hn).astype(BF16)

    for i in range(n_ctx):
        finish(i, i, oc_ref, outc_ref)

    def lat_finish(i, carry):
        finish(i, n_ctx + i, ol_ref, outl_ref)
        return carry

    lax.fori_loop(0, n_lat, lat_finish, 0)


def mlstm(mq, mkt, mv, mo, g_row, gate_b, mnorm_g, batch, ctx_len, seq):
    rows, width = mq.shape
    assert MLSTM_CHUNK == LANES and width == M_HEADS * LANES
    lat0 = batch * ctx_len // seq
    n_chunks = (ctx_len + seq) // MLSTM_CHUNK
    ctx = lambda b: (b, 0)
    lat = lambda b: (lat0 + b, 0)
    const = lambda b: (0, 0)
    out_c, out_l = pl.pallas_call(
        _mlstm_kernel,
        grid=(batch,),
        in_specs=[
            pl.BlockSpec((ctx_len, width), ctx), pl.BlockSpec((seq, width), lat),
            pl.BlockSpec((ctx_len // MLSTM_CHUNK, width, MLSTM_CHUNK), lambda b: (b, 0, 0)),
            pl.BlockSpec((seq // MLSTM_CHUNK, width, MLSTM_CHUNK), lambda b: (lat0 + b, 0, 0)),
            pl.BlockSpec((ctx_len, width), ctx), pl.BlockSpec((seq, width), lat),
            pl.BlockSpec((ctx_len, width), ctx), pl.BlockSpec((seq, width), lat),
            pl.BlockSpec((N_GATES, ctx_len), lambda b: (0, b)), pl.BlockSpec((N_GATES, seq), lambda b: (0, lat0 + b)),
            pl.BlockSpec((N_GATES, 1), const),
            pl.BlockSpec((1, width), const),
        ],
        out_specs=[pl.BlockSpec((ctx_len, width), ctx), pl.BlockSpec((seq, width), ctx)],
        out_shape=[jax.ShapeDtypeStruct((batch * ctx_len, width), BF16),
                   jax.ShapeDtypeStruct((batch * seq, width), BF16)],
        scratch_shapes=[
            pltpu.VMEM((n_chunks, 8 * M_HEADS, MLSTM_CHUNK), F32),
            pltpu.VMEM((n_chunks, MLSTM_CHUNK, MLSTM_CHUNK), F32),
            pltpu.VMEM((2 * M_HEADS, LANES, 2 * LANES), F32),
            pltpu.VMEM((ctx_len + seq, width), F32),
            pltpu.VMEM((ctx_len + seq, width), F32),
        ],
        compiler_params=_params("arbitrary"),
        name="mlstm",
    )(mq, mq, mkt, mkt, mv, mv, mo, mo, g_row, g_row, gate_b, mnorm_g)
    return out_c, out_l


def _router_top2(logits):
    lane = lax.broadcasted_iota(jnp.int32, logits.shape, 1)
    logits = jnp.where(lane < N_EXPERTS, logits, -jnp.inf)
    v1 = jnp.max(logits, axis=1, keepdims=True)
    i1 = jnp.min(jnp.where(logits == v1, lane, LANES), axis=1, keepdims=True)
    rest = jnp.where(lane == i1, -jnp.inf, logits)
    v2 = jnp.max(rest, axis=1, keepdims=True)
    i2 = jnp.min(jnp.where(rest == v2, lane, LANES), axis=1, keepdims=True)
    e = jnp.exp(v2 - v1)
    w1 = 1.0 / (1.0 + e)
    w2 = e / (1.0 + e)
    mask = jnp.where(jnp.logical_or(lane == i1, lane == i2), 1.0, 0.0)
    expert = lane & (N_EXPERTS - 1)
    comb = jnp.where(lane < 3 * N_EXPERTS,
                     jnp.where(expert == i1, w1, 0.0) + jnp.where(expert == i2, w2, 0.0), 0.0)
    hi = comb.astype(BF16)
    rem = comb - hi.astype(F32)
    mid = rem.astype(BF16)
    lo = (rem - mid.astype(F32)).astype(BF16)
    wcols = jnp.where(lane < N_EXPERTS, hi, jnp.where(lane < 2 * N_EXPERTS, mid, lo))
    return mask, wcols


def _out_proj_kernel(*refs, with_router, n_ctx_tiles):
    if with_router:
        a_ref, m_ref, w_ref, x_ref, mod_ref, g_ref, rw_ref, x1_ref, h2_ref, sel_ref, wc_ref, cnt_ref = refs
        m = m_ref[...]
        x = x_ref[...]
    else:
        a_ref, m_ref, mc_ref, w_ref, xc_ref, x_ref, mod_ref, g_ref, x1_ref, h2_ref = refs
        is_ctx = pl.program_id(0) < n_ctx_tiles
        m = jnp.where(is_ctx, mc_ref[...], m_ref[...])
        x = jnp.where(is_ctx, xc_ref[...], x_ref[...])
    half = a_ref.shape[1]
    mix = (jnp.dot(a_ref[...], w_ref[:half, :], preferred_element_type=F32)
           + jnp.dot(m, w_ref[half:, :], preferred_element_type=F32))
    x1 = x + mod_ref[0, 2:3, :] * mix
    x1_ref[...] = x1
    h2 = _rms(x1) * g_ref[...] * (1.0 + mod_ref[0, 4:5, :]) + mod_ref[0, 3:4, :]
    hb = h2.astype(BF16)
    h2_ref[...] = hb
    if with_router:
        p = jnp.dot(hb, rw_ref[...], preferred_element_type=F32)
        h_lo = (h2 - hb.astype(F32)).astype(BF16)
        logits = p[:, :LANES] + p[:, LANES:] + jnp.dot(h_lo, rw_ref[:, :LANES], preferred_element_type=F32)
        mask, wcols = _router_top2(logits)
        sel_ref[...] = mask.astype(BF16)
        wc_ref[...] = wcols
        cnt_ref[0] = jnp.sum(mask, axis=0, keepdims=True)


def out_proj(a, m_lat, m_ctx, w_out, x_ctx, x_lat, lat_first_tile, mods, norm_g, router_w, n_ctx_tiles, tiles_per_seq):
    d = x_lat.shape[1]
    half = a.shape[1]
    with_router = router_w is not None
    assert with_router == (m_ctx is None) == (x_ctx is None)
    first_tile = n_ctx_tiles if m_ctx is None else 0
    n_lat_tiles = x_lat.shape[0] // ROW_TILE - lat_first_tile
    n_tiles = n_lat_tiles + n_ctx_tiles - first_tile
    out_rows = n_tiles * ROW_TILE

    def seg(i):
        t = i + first_tile
        return jnp.where(t < n_ctx_tiles, 0, 1 + (t - n_ctx_tiles) // tiles_per_seq)

    dst = lambda i: (i, 0)
    lat = lambda i: (jnp.maximum(i + first_tile - n_ctx_tiles, 0), 0)
    ctx = lambda i: (jnp.minimum(i, n_ctx_tiles - 1), 0)
    const = lambda i: (0, 0)
    in_specs = [pl.BlockSpec((ROW_TILE, half), dst), pl.BlockSpec((ROW_TILE, half), lat)]
    args = [a, m_lat]
    if m_ctx is not None:
        in_specs.append(pl.BlockSpec((ROW_TILE, half), ctx))
        args.append(m_ctx)
    in_specs.append(pl.BlockSpec(w_out.shape, const))
    args.append(w_out)
    if x_ctx is not None:
        in_specs.append(pl.BlockSpec((ROW_TILE, d), ctx))
        args.append(x_ctx)
    in_specs += [
        pl.BlockSpec((ROW_TILE, d), lambda i: (lat(i)[0] + lat_first_tile, 0)),
        pl.BlockSpec((1, 6, d), lambda i: (seg(i), 0, 0)),
        pl.BlockSpec((1, d), const),
    ]
    args += [x_lat, mods, norm_g]
    out_specs = [pl.BlockSpec((ROW_TILE, d), dst), pl.BlockSpec((ROW_TILE, d), dst)]
    out_shape = [jax.ShapeDtypeStruct((out_rows, d), F32), jax.ShapeDtypeStruct((out_rows, d), BF16)]
    if with_router:
        pad = ((0, 0), (0, LANES - router_w.shape[1]))
        rw = jnp.concatenate([jnp.pad(t, pad) for t in _split_bf16(router_w)], axis=1)
        in_specs.append(pl.BlockSpec(rw.shape, const))
        out_specs += [pl.BlockSpec((ROW_TILE, LANES), dst), pl.BlockSpec((ROW_TILE, LANES), dst),
                      pl.BlockSpec((1, 1, LANES), lambda i: (i, 0, 0))]
        out_shape += [jax.ShapeDtypeStruct((out_rows, LANES), BF16),
                      jax.ShapeDtypeStruct((out_rows, LANES), BF16),
                      jax.ShapeDtypeStruct((n_tiles, 1, LANES), F32)]
        args.append(rw)
    return pl.pallas_call(
        functools.partial(_out_proj_kernel, with_router=with_router, n_ctx_tiles=n_ctx_tiles),
        grid=(n_tiles,),
        in_specs=in_specs, out_specs=out_specs, out_shape=out_shape,
        compiler_params=_params("arbitrary"),
        name="out_proj",
    )(*args)


def _ffn_kernel(h_ref, w1_ref, w3_ref, w2_ref, x_ref, mod_ref, o_ref, acc_ref):
    f = pl.program_id(1)

    @pl.when(f == 0)
    def _():
        acc_ref[...] = jnp.zeros(acc_ref.shape, F32)

    h = h_ref[...]
    a = _silu(jnp.dot(h, w1_ref[...], preferred_element_type=F32)) * jnp.dot(h, w3_ref[...], preferred_element_type=F32)
    acc_ref[...] += jnp.dot(a.astype(BF16), w2_ref[...], preferred_element_type=F32)

    @pl.when(f == pl.num_programs(1) - 1)
    def _():
        o_ref[...] = x_ref[...] + mod_ref[0, 5:6, :] * acc_ref[...]


def ffn(h2, w1, w3, w2, x1, mods, tm, tf, n_ctx_rows, rows_per_seq):
    rows, d = x1.shape
    dff = w1.shape[1]

    def seg(i):
        r0 = i * tm
        return jnp.where(r0 < n_ctx_rows, 0, 1 + (r0 - n_ctx_rows) // rows_per_seq)

    return pl.pallas_call(
        _ffn_kernel,
        grid=(rows // tm, dff // tf),
        in_specs=[
            pl.BlockSpec((tm, d), lambda i, f: (i, 0)),
            pl.BlockSpec((d, tf), lambda i, f: (0, f)),
            pl.BlockSpec((d, tf), lambda i, f: (0, f)),
            pl.BlockSpec((tf, d), lambda i, f: (f, 0)),
            pl.BlockSpec((tm, d), lambda i, f: (i, 0)),
            pl.BlockSpec((1, 6, d), lambda i, f: (seg(i), 0, 0)),
        ],
        out_specs=pl.BlockSpec((tm, d), lambda i, f: (i, 0)),
        out_shape=jax.ShapeDtypeStruct((rows, d), F32),
        scratch_shapes=[pltpu.VMEM((tm, d), F32)],
        compiler_params=_params("arbitrary", "arbitrary"),
        name="ffn",
    )(h2, w1, w3, w2, x1, mods)


MOE_TOKEN_TILE = 512
MOE_ROW_TILE = 512
MOE_FF_TILE = 512
SLOT_ALIGN = 16
STAGE_ROWS = 2 * MOE_TOKEN_TILE + LANES


def _pow2_chunks(max_chunks):
    k = 1
    while k * 2 <= max_chunks:
        k *= 2
    while k >= 1:
        yield k
        k //= 2


def _segment_copies(n_chunks, src, dst, src_ref, dst_ref, sem, max_chunks, action):
    for k in _pow2_chunks(max_chunks):
        rows = k * SLOT_ALIGN
        taken = (n_chunks & ~(2 * k - 1)) * SLOT_ALIGN

        @pl.when((n_chunks & k) != 0)
        def _(rows=rows, taken=taken):
            s = pl.multiple_of(src + taken, SLOT_ALIGN)
            t = pl.multiple_of(dst + taken, SLOT_ALIGN)
            cp = pltpu.make_async_copy(src_ref.at[pl.ds(s, rows), :], dst_ref.at[pl.ds(t, rows), :], sem)
            if action == "start":
                cp.start()
            else:
                cp.wait()


def _slot_one_hot(sel_ref, off_ref):
    sel = sel_ref[...]
    tt = sel.shape[0]
    r = lax.broadcasted_iota(jnp.int32, (tt, tt), 0)
    c = lax.broadcasted_iota(jnp.int32, (tt, tt), 1)
    earlier = jnp.where(c < r, 1.0, 0.0).astype(BF16)
    rank = jnp.dot(earlier, sel, preferred_element_type=F32)
    pos = off_ref[0] + rank
    chosen = sel > 0
    pa = jnp.min(jnp.where(chosen, pos, float(STAGE_ROWS)), axis=1, keepdims=True)
    pb = jnp.max(jnp.where(chosen, pos, -1.0), axis=1, keepdims=True)
    slot = lax.broadcasted_iota(jnp.int32, (tt, STAGE_ROWS), 1).astype(F32)
    return jnp.where(jnp.logical_or(slot == pa, slot == pb), 1.0, 0.0).astype(BF16)


def _dispatch_kernel(seg_start_ref, seg_chunks_ref, tile_off_ref, tail_start_ref, tail_chunks_ref, n_used_ref,
                     h_ref, wc_ref, sel_ref, off_ref, xs_ref, stage_ref, zero_ref, sem):
    t = pl.program_id(0)
    one_hot = _slot_one_hot(sel_ref, off_ref)
    payload = jnp.concatenate([h_ref[...], wc_ref[...]], axis=1)
    stage_ref[...] = lax.dot_general(one_hot, payload, TN_DIMS, preferred_element_type=F32).astype(BF16)
    max_chunks = MOE_TOKEN_TILE // SLOT_ALIGN
    for action in ("start", "wait"):
        for e in range(N_EXPERTS):
            idx = t * N_EXPERTS + e
            _segment_copies(seg_chunks_ref[idx], tile_off_ref[idx], seg_start_ref[idx],
                            stage_ref, xs_ref, sem, max_chunks, action)

    @pl.when(t == pl.num_programs(0) - 1)
    def _():
        zero_ref[...] = jnp.zeros(zero_ref.shape, BF16)
        for action in ("start", "wait"):
            for e in range(N_EXPERTS):
                _segment_copies(tail_chunks_ref[e], 0, tail_start_ref[e], zero_ref, xs_ref, sem,
                                MOE_ROW_TILE // SLOT_ALIGN - 1, action)

        def clear_tile(j, carry):
            rows = pl.ds(pl.multiple_of(j * MOE_ROW_TILE, MOE_ROW_TILE), MOE_ROW_TILE)
            cp = pltpu.make_async_copy(zero_ref, xs_ref.at[rows, :], sem)
            cp.start()
            cp.wait()
            return carry

        lax.fori_loop(n_used_ref[0], xs_ref.shape[0] // MOE_ROW_TILE, clear_tile, 0)


def moe_dispatch(plan, h2, wcols, sel, n_slots):
    rows, d = h2.shape
    tt = MOE_TOKEN_TILE
    width = d + LANES
    tile = lambda t, *_: (t, 0)
    grid_spec = pltpu.PrefetchScalarGridSpec(
        num_scalar_prefetch=6,
        grid=(rows // tt,),
        in_specs=[
            pl.BlockSpec((tt, d), tile),
            pl.BlockSpec((tt, LANES), tile),
            pl.BlockSpec((tt, LANES), tile),
            pl.BlockSpec((1, 1, LANES), lambda t, *_: (t, 0, 0)),
        ],
        out_specs=pl.BlockSpec(memory_space=pl.ANY),
        scratch_shapes=[pltpu.VMEM((STAGE_ROWS, width), BF16),
                        pltpu.VMEM((MOE_ROW_TILE, width), BF16),
                        pltpu.SemaphoreType.DMA],
    )
    return pl.pallas_call(
        _dispatch_kernel,
        grid_spec=grid_spec,
        out_shape=jax.ShapeDtypeStruct((n_slots, width), BF16),
        compiler_params=_params("arbitrary"),
        name="moe_dispatch",
    )(plan["seg_start"], plan["seg_chunks"], plan["tile_off"], plan["tail_start"], plan["tail_chunks"],
      plan["n_used"], h2, wcols, sel, plan["off_lanes"])


def _moe_kernel(tile_expert_ref, n_used_ref, xs_ref, w1_ref, w3_ref, w2_ref, ys_ref, acc_ref):
    i = pl.program_id(0)
    f = pl.program_id(1)

    @pl.when(i < n_used_ref[0])
    def _():
        @pl.when(f == 0)
        def _():
            acc_ref[...] = jnp.zeros(acc_ref.shape, F32)

        d = w1_ref.shape[1]
        x = xs_ref[:, :d]
        wc = xs_ref[:, d:].astype(F32)
        lane = lax.broadcasted_iota(jnp.int32, wc.shape, 1)
        mine = jnp.logical_and((lane & (N_EXPERTS - 1)) == tile_expert_ref[i], lane < 3 * N_EXPERTS)
        wt = jnp.sum(jnp.where(mine, wc, 0.0), axis=1, keepdims=True)
        a = _silu(jnp.dot(x, w1_ref[0], preferred_element_type=F32)) * jnp.dot(x, w3_ref[0], preferred_element_type=F32)
        acc_ref[...] += jnp.dot((a * wt).astype(BF16), w2_ref[0], preferred_element_type=F32)

        @pl.when(f == pl.num_programs(1) - 1)
        def _():
            ys_ref[...] = acc_ref[...].astype(BF16)

    @pl.when(jnp.logical_and(i >= n_used_ref[0], f == 0))
    def _():
        ys_ref[...] = jnp.zeros(ys_ref.shape, BF16)


def moe_experts(plan, xs, w1, w3, w2):
    n_slots, width = xs.shape
    ne, d, dff = w1.shape
    tm, tf = MOE_ROW_TILE, MOE_FF_TILE
    n_f = dff // tf

    def row(i, f, te, nu):
        return jnp.minimum(i, nu[0] - 1)

    def ff(i, f, te, nu):
        return jnp.where(i < nu[0], f, n_f - 1)

    grid_spec = pltpu.PrefetchScalarGridSpec(
        num_scalar_prefetch=2,
        grid=(n_slots // tm, n_f),
        in_specs=[
            pl.BlockSpec((tm, width), lambda i, f, te, nu: (row(i, f, te, nu), 0)),
            pl.BlockSpec((1, d, tf), lambda i, f, te, nu: (te[row(i, f, te, nu)], 0, ff(i, f, te, nu))),
            pl.BlockSpec((1, d, tf), lambda i, f, te, nu: (te[row(i, f, te, nu)], 0, ff(i, f, te, nu))),
            pl.BlockSpec((1, tf, d), lambda i, f, te, nu: (te[row(i, f, te, nu)], ff(i, f, te, nu), 0)),
        ],
        out_specs=pl.BlockSpec((tm, d), lambda i, f, te, nu: (i, 0)),
        scratch_shapes=[pltpu.VMEM((tm, d), F32)],
    )
    return pl.pallas_call(
        _moe_kernel,
        grid_spec=grid_spec,
        out_shape=jax.ShapeDtypeStruct((n_slots, d), BF16),
        compiler_params=_params("arbitrary", "arbitrary"),
        name="moe_experts",
    )(plan["tile_expert"], plan["n_used"], xs, w1, w3, w2)


def _combine_kernel(seg_start_ref, seg_chunks_ref, tile_off_ref,
                    ys_ref, sel_ref, off_ref, x_ref, mod_ref, o_ref, ybuf_ref, sem):
    t = pl.program_id(0)

    @pl.when(t == 0)
    def _():
        ybuf_ref[...] = jnp.zeros(ybuf_ref.shape, BF16)

    max_chunks = MOE_TOKEN_TILE // SLOT_ALIGN

    def copies(action):
        for e in range(N_EXPERTS):
            idx = t * N_EXPERTS + e
            _segment_copies(seg_chunks_ref[idx], seg_start_ref[idx], tile_off_ref[idx],
                            ys_ref, ybuf_ref, sem, max_chunks, action)

    copies("start")
    one_hot = _slot_one_hot(sel_ref, off_ref)
    copies("wait")
    y = jnp.dot(one_hot, ybuf_ref[...], preferred_element_type=F32)
    o_ref[...] = x_ref[...] + mod_ref[0, 5:6, :] * y


def moe_combine(plan, ys, sel, x1, mods, rows_per_seq):
    rows, d = x1.shape
    tt = MOE_TOKEN_TILE
    tile = lambda t, *_: (t, 0)
    grid_spec = pltpu.PrefetchScalarGridSpec(
        num_scalar_prefetch=3,
        grid=(rows // tt,),
        in_specs=[
            pl.BlockSpec(memory_space=pl.ANY),
            pl.BlockSpec((tt, LANES), tile),
            pl.BlockSpec((1, 1, LANES), lambda t, *_: (t, 0, 0)),
            pl.BlockSpec((tt, d), tile),
            pl.BlockSpec((1, 6, d), lambda t, *_: (1 + (t * tt) // rows_per_seq, 0, 0)),
        ],
        out_specs=pl.BlockSpec((tt, d), tile),
        scratch_shapes=[pltpu.VMEM((STAGE_ROWS, d), BF16), pltpu.SemaphoreType.DMA],
    )
    return pl.pallas_call(
        _combine_kernel,
        grid_spec=grid_spec,
        out_shape=jax.ShapeDtypeStruct((rows, d), F32),
        compiler_params=_params("arbitrary"),
        name="moe_combine",
    )(plan["seg_start"], plan["seg_chunks"], plan["tile_off"], ys, sel, plan["off_lanes"], x1, mods)


def _moe_plan(counts, n_row_tiles):
    counts = counts.astype(jnp.int32)
    seg = (counts + SLOT_ALIGN - 1) // SLOT_ALIGN * SLOT_ALIGN
    tile_off = jnp.cumsum(seg, axis=1) - seg
    total = jnp.sum(seg, axis=0)
    region = (total + MOE_ROW_TILE - 1) // MOE_ROW_TILE * MOE_ROW_TILE
    region_end = jnp.cumsum(region)
    region_start = region_end - region
    seg_start = region_start[None, :] + jnp.cumsum(seg, axis=0) - seg
    first_row = jnp.arange(n_row_tiles, dtype=jnp.int32) * MOE_ROW_TILE
    tile_expert = jnp.minimum(jnp.sum(first_row[:, None] >= region_end[None, :], axis=1), counts.shape[1] - 1)
    off_lanes = jnp.pad(tile_off.astype(F32), ((0, 0), (0, LANES - counts.shape[1])))
    return {
        "seg_start": seg_start.reshape(-1),
        "seg_chunks": (seg // SLOT_ALIGN).reshape(-1),
        "tile_off": tile_off.reshape(-1),
        "tail_start": region_start + total,
        "tail_chunks": (region - total) // SLOT_ALIGN,
        "tile_expert": tile_expert.astype(jnp.int32),
        "n_used": (region_end[-1:] // MOE_ROW_TILE).astype(jnp.int32),
        "off_lanes": off_lanes.reshape(counts.shape[0], 1, LANES),
    }


def _rope_tables(ctx_len, seq):
    rows = seq // GRID_W
    row = jnp.repeat(jnp.arange(rows, dtype=F32), GRID_W)
    col = jnp.tile(jnp.arange(GRID_W, dtype=F32), rows)
    inv = ROPE_BASE ** (-jnp.arange(ROPE_PAIRS_PER_AXIS, dtype=F32) / ROPE_PAIRS_PER_AXIS)
    ang = jnp.concatenate([row[:, None] * inv, col[:, None] * inv], axis=-1)
    cos, sin = jnp.cos(ang), jnp.sin(ang)
    cos_t = jnp.concatenate([cos, cos, cos, cos], axis=-1)
    sin_t = jnp.concatenate([-sin, sin, -sin, sin], axis=-1)
    cos_t = jnp.concatenate([jnp.ones((ctx_len, LANES), F32), cos_t], axis=0)
    sin_t = jnp.concatenate([jnp.zeros((ctx_len, LANES), F32), sin_t], axis=0)
    return cos_t, sin_t


def kernel(x, c, ctx, c_ctx, ada_w, ada_b, norm1_g, norm2_g, w_in, w_out, q_norm_g, k_norm_g, lambda_q1, lambda_k1, lambda_q2, lambda_k2, subln_g, conv_w, conv_b, gate_b, mnorm_g, ffn_w1, ffn_w3, ffn_w2, router_w, moe_w1, moe_w3, moe_w2):
    batch, seq, d = x.shape
    ctx_len = ctx.shape[1]
    depth = ada_w.shape[0]
    assert depth == 2, "layer 0 runs the dense FFN on all rows, layer 1 the experts on the latents"
    n_ctx_rows = batch * ctx_len
    n_ctx_tiles = n_ctx_rows // ROW_TILE
    tiles_per_seq = seq // ROW_TILE
    n_main = w_in.shape[2] - N_GATES
    assert n_main == 7 * A_WIDTH
    m_dh = (n_main - 3 * A_WIDTH) // 4 // M_HEADS
    gate_order = jnp.array(GATE_ORDER, jnp.int32)
    lane_pad = ((0, 0), (0, LANES - N_GATES))

    x_ctx = ctx.reshape(n_ctx_rows, d)
    x_lat = x.reshape(batch * seq, d)
    lat_first_tile = 0

    cc = jnp.zeros((2 * SUBLANES, d), F32).at[:batch].set(c).at[batch].set(c_ctx)
    mod_rows = adaln_all(cc, ada_w, ada_b)
    mods_all = jnp.concatenate([mod_rows[:, batch:batch + 1], mod_rows[:, :batch]], axis=1)
    mods_all = mods_all.reshape(depth, batch + 1, 6, d)

    cos_t, sin_t = _rope_tables(ctx_len, seq)

    for l in range(depth):
        last = l == depth - 1
        lam_init = 0.8 - 0.6 * math.exp(-0.3 * l)
        lam = (jnp.exp(jnp.sum(lambda_q1[l] * lambda_k1[l]).astype(F32))
               - jnp.exp(jnp.sum(lambda_q2[l] * lambda_k2[l]).astype(F32)) + lam_init)
        mods = mods_all[l]
        g_hi, g_lo = _split_bf16(w_in[l][:, n_main:][:, gate_order])
        g_hi, g_lo = jnp.pad(g_hi, lane_pad), jnp.pad(g_lo, lane_pad)
        w_all = jnp.concatenate([w_in[l, :, :n_main].astype(BF16), g_hi, g_lo], axis=1)
        qg = jnp.tile(q_norm_g[l] * (A_DH ** -0.5), 2).reshape(1, LANES)
        kg = jnp.tile(k_norm_g[l], 2).reshape(1, LANES)

        qk, av, mq, mkt, mv, mo, g_row = proj_in(
            x_ctx, x_lat, lat_first_tile, mods, norm1_g[l].reshape(1, d), w_all, g_hi,
            conv_w[l], conv_b[l].reshape(1, -1), cos_t, sin_t, qg, kg, n_ctx_tiles, tiles_per_seq, m_dh ** -0.5)
        a_out = attention(lam.reshape(1, 1), qk, av, subln_g[l].reshape(1, LANES), 1.0 - lam_init,
                          batch, ctx_len, seq, with_ctx_queries=not last)
        m_ctx, m_out = mlstm(mq, mkt, mv, mo, g_row, gate_b[l].reshape(-1)[gate_order].reshape(N_GATES, 1),
                             mnorm_g[l].reshape(1, -1), batch, ctx_len, seq)
        w_out_b = w_out[l].astype(BF16)
        j = l // 2
        if not last:
            x1, h2 = out_proj(a_out, m_out, m_ctx, w_out_b, x_ctx, x_lat, lat_first_tile, mods,
                              norm2_g[l].reshape(1, d), None, n_ctx_tiles, tiles_per_seq)
            xs = ffn(h2, ffn_w1[j].astype(BF16), ffn_w3[j].astype(BF16), ffn_w2[j].astype(BF16), x1, mods,
                     512, 1408, n_ctx_rows, seq)
            x_ctx, x_lat, lat_first_tile = xs, xs, n_ctx_tiles
        else:
            x1, h2, sel, wcols, cnt = out_proj(a_out, m_out, None, w_out_b, None, x_lat, lat_first_tile, mods,
                                               norm2_g[l].reshape(1, d), router_w[j], n_ctx_tiles, tiles_per_seq)
            n_tok_tiles = x1.shape[0] // MOE_TOKEN_TILE
            counts = cnt[:, 0, :N_EXPERTS].reshape(n_tok_tiles, MOE_TOKEN_TILE // ROW_TILE, N_EXPERTS).sum(axis=1)
            worst = 2 * x1.shape[0] + n_tok_tiles * N_EXPERTS * (SLOT_ALIGN - 1) + N_EXPERTS * (MOE_ROW_TILE - SLOT_ALIGN)
            n_row_tiles = -(-worst // MOE_ROW_TILE)
            plan = _moe_plan(counts, n_row_tiles)
            slots = moe_dispatch(plan, h2, wcols, sel, n_row_tiles * MOE_ROW_TILE)
            ys = moe_experts(plan, slots, moe_w1[j].astype(BF16), moe_w3[j].astype(BF16), moe_w2[j].astype(BF16))
            out = moe_combine(plan, ys, sel, x1, mods, seq)
    return out.reshape(batch, seq, d)
```

```python
import functools
import math

import jax
import jax.numpy as jnp
from jax import lax
from jax.experimental import pallas as pl
from jax.experimental.pallas import tpu as pltpu

F32 = jnp.float32
BF16 = jnp.bfloat16
HIGHEST = lax.Precision.HIGHEST

GRID_W = 64
A_HEADS = 4
A_DH = 64
A_DV = 2 * A_DH
A_WIDTH = A_HEADS * A_DV
M_HEADS = 4
N_GATES = 4 * M_HEADS
CONV_K = 5
ROPE_BASE = 10000.0
ROPE_PAIRS_PER_AXIS = A_DH // 4
N_EXPERTS = 8
EPS = 1e-6

LANES = 128
SUBLANES = 8
ROW_TILE = 256
MLSTM_CHUNK = 128
VMEM_LIMIT = 56 * 1024 * 1024

NT_DIMS = (((1,), (1,)), ((), ()))
TN_DIMS = (((0,), (0,)), ((), ()))

GATE_ORDER = tuple(d * 2 * M_HEADS + k * M_HEADS + h for k in range(2) for d in range(2) for h in range(M_HEADS))


def _params(*sem):
    return pltpu.CompilerParams(dimension_semantics=sem, vmem_limit_bytes=VMEM_LIMIT)


def _silu(v):
    return v * jax.nn.sigmoid(v)


def _log_sigmoid(v):
    return jnp.minimum(v, 0.0) - jnp.log1p(jnp.exp(-jnp.abs(v)))


def _rms(v, axis=-1):
    return v * lax.rsqrt(jnp.mean(v * v, axis=axis, keepdims=True) + EPS)


def _split_bf16(w):
    hi = w.astype(BF16)
    return hi, (w - hi.astype(F32)).astype(BF16)


def _adaln_kernel(c_ref, w_ref, b_ref, o_ref):
    s = _silu(c_ref[...])
    o_ref[0] = jnp.dot(s, w_ref[0], precision=HIGHEST, preferred_element_type=F32) + b_ref[0]


def adaln_all(cc, ada_w, ada_b):
    depth, d, d6 = ada_w.shape
    n_col = d6 // d
    return pl.pallas_call(
        _adaln_kernel,
        grid=(depth, n_col),
        in_specs=[
            pl.BlockSpec(cc.shape, lambda l, j: (0, 0)),
            pl.BlockSpec((1, d, d), lambda l, j: (l, 0, j)),
            pl.BlockSpec((1, 1, d), lambda l, j: (l, 0, j)),
        ],
        out_specs=pl.BlockSpec((1, cc.shape[0], d), lambda l, j: (l, 0, j)),
        out_shape=jax.ShapeDtypeStruct((depth, cc.shape[0], d6), F32),
        compiler_params=_params("arbitrary", "arbitrary"),
        name="adaln",
    )(cc, ada_w, ada_b.reshape(depth, 1, d6))


def _proj_in_kernel(xa_ref, xb_ref, prev_ref, next_ref, mod_ref, g_ref, w_ref, wgh_ref, cw_ref, cb_ref,
                    cos_ref, sin_ref, qg_ref, kg_ref,
                    qk_ref, v_ref, mq_ref, mkt_ref, mv_ref, mo_ref, gr_ref, ext_ref,
                    *, n_ctx_tiles, tiles_per_seq, k_scale):
    i = pl.program_id(0)
    is_ctx = i < n_ctx_tiles
    pos = (i - n_ctx_tiles) % tiles_per_seq
    first = jnp.logical_or(is_ctx, pos == 0)
    last = jnp.logical_or(is_ctx, pos == tiles_per_seq - 1)
    tm = xa_ref.shape[0]
    halo = SUBLANES

    x_ext = jnp.concatenate([prev_ref[...], jnp.where(is_ctx, xa_ref[...], xb_ref[...]), next_ref[...]], axis=0)
    h_ext = _rms(x_ext) * g_ref[...] * (1.0 + mod_ref[0, 1:2, :]) + mod_ref[0, 0:1, :]
    h = h_ext[halo:halo + tm]
    hb = h.astype(BF16)

    def proj(j):
        return jnp.dot(hb, w_ref[:, j * A_WIDTH:(j + 1) * A_WIDTH], preferred_element_type=F32)

    r = lax.broadcasted_iota(jnp.int32, (LANES, LANES), 0)
    c = lax.broadcasted_iota(jnp.int32, (LANES, LANES), 1)
    group_ones = jnp.where((r & -A_DH) == (c & -A_DH), 1.0, 0.0).astype(BF16)
    lane = lax.broadcasted_iota(jnp.int32, (1, LANES), 1)
    first_half = (lane & (A_DH - 1)) < (A_DH // 2)
    cos = cos_ref[...]
    sin = sin_ref[...]
    for j, gain_ref in ((0, qg_ref), (1, kg_ref)):
        acc = proj(j)
        for hd in range(A_HEADS):
            t = acc[:, hd * LANES:(hd + 1) * LANES]
            ss = jnp.dot((t * t).astype(BF16), group_ones, preferred_element_type=F32)
            tn = t * lax.rsqrt(ss * (1.0 / A_DH) + EPS) * gain_ref[...]
            rot = jnp.where(first_half, pltpu.roll(tn, LANES - A_DH // 2, 1), pltpu.roll(tn, A_DH // 2, 1))
            qk_ref[:, j * A_WIDTH + hd * LANES: j * A_WIDTH + (hd + 1) * LANES] = (tn * cos + rot * sin).astype(BF16)
    v_ref[...] = proj(2).astype(BF16)

    pre = jnp.dot(h_ext.astype(BF16), w_ref[:, 3 * A_WIDTH:5 * A_WIDTH], preferred_element_type=F32)
    ext_ref[0:halo, :] = jnp.where(first, 0.0, pre[0:halo])
    ext_ref[halo:halo + tm, :] = pre[halo:halo + tm]
    ext_ref[halo + tm:, :] = jnp.where(last, 0.0, pre[halo + tm:])
    acc = jnp.zeros((tm, 2 * A_WIDTH), F32) + cb_ref[...]
    pad = CONV_K // 2
    for j in range(CONV_K):
        acc = acc + cw_ref[j:j + 1, :] * ext_ref[halo - pad + j:halo - pad + j + tm, :]
    y = _silu(acc)
    mq_ref[...] = y[:, :A_WIDTH].astype(BF16)
    for ci in range(tm // MLSTM_CHUNK):
        yk = y[ci * MLSTM_CHUNK:(ci + 1) * MLSTM_CHUNK, A_WIDTH:] * k_scale
        mkt_ref[ci] = yk.T.astype(BF16)

    mv_ref[...] = proj(5).astype(BF16)
    mo_ref[...] = proj(6)

    n_main = 7 * A_WIDTH
    p = jnp.dot(hb, w_ref[:, n_main:n_main + 2 * LANES], preferred_element_type=F32)
    h_lo = (h - hb.astype(F32)).astype(BF16)
    gates = p[:, :LANES] + p[:, LANES:] + jnp.dot(h_lo, wgh_ref[...], preferred_element_type=F32)
    gr_ref[...] = gates.T[:N_GATES, :]


def proj_in(xa, xb, b_first_tile, mods, norm_g, w_all, w_gate_hi, conv_w, conv_b, cos_t, sin_t, qg, kg,
            n_ctx_tiles, tiles_per_seq, k_scale):
    d = xa.shape[1]
    n_lat_tiles = xb.shape[0] // ROW_TILE - b_first_tile
    n_tiles = n_ctx_tiles + n_lat_tiles
    rows = n_tiles * ROW_TILE
    per = ROW_TILE // SUBLANES

    def seg(i):
        return jnp.where(i < n_ctx_tiles, 0, 1 + (i - n_ctx_tiles) // tiles_per_seq)

    def rope_blk(i):
        return jnp.where(i < n_ctx_tiles, 0, 1 + (i - n_ctx_tiles) % tiles_per_seq)

    def lat(i):
        return jnp.maximum(i - n_ctx_tiles, 0)

    row = lambda i: (i, 0)
    const = lambda i: (0, 0)
    kern = functools.partial(_proj_in_kernel, n_ctx_tiles=n_ctx_tiles, tiles_per_seq=tiles_per_seq, k_scale=k_scale)
    return pl.pallas_call(
        kern,
        grid=(n_tiles,),
        in_specs=[
            pl.BlockSpec((ROW_TILE, d), lambda i: (jnp.minimum(i, n_ctx_tiles - 1), 0)),
            pl.BlockSpec((ROW_TILE, d), lambda i: (lat(i) + b_first_tile, 0)),
            pl.BlockSpec((SUBLANES, d), lambda i: (jnp.maximum(lat(i) * per - 1, 0) + b_first_tile * per, 0)),
            pl.BlockSpec((SUBLANES, d),
                         lambda i: (jnp.minimum((lat(i) + 1) * per, n_lat_tiles * per - 1) + b_first_tile * per, 0)),
            pl.BlockSpec((1, 6, d), lambda i: (seg(i), 0, 0)),
            pl.BlockSpec((1, d), const),
            pl.BlockSpec(w_all.shape, const),
            pl.BlockSpec(w_gate_hi.shape, const),
            pl.BlockSpec(conv_w.shape, const),
            pl.BlockSpec(conv_b.shape, const),
            pl.BlockSpec((ROW_TILE, LANES), lambda i: (rope_blk(i), 0)),
            pl.BlockSpec((ROW_TILE, LANES), lambda i: (rope_blk(i), 0)),
            pl.BlockSpec((1, LANES), const),
            pl.BlockSpec((1, LANES), const),
        ],
        out_specs=[
            pl.BlockSpec((ROW_TILE, 2 * A_WIDTH), row),
            pl.BlockSpec((ROW_TILE, A_WIDTH), row),
            pl.BlockSpec((ROW_TILE, A_WIDTH), row),
            pl.BlockSpec((ROW_TILE // MLSTM_CHUNK, A_WIDTH, MLSTM_CHUNK), lambda i: (i, 0, 0)),
            pl.BlockSpec((ROW_TILE, A_WIDTH), row),
            pl.BlockSpec((ROW_TILE, A_WIDTH), row),
            pl.BlockSpec((N_GATES, ROW_TILE), lambda i: (0, i)),
        ],
        out_shape=[
            jax.ShapeDtypeStruct((rows, 2 * A_WIDTH), BF16),
            jax.ShapeDtypeStruct((rows, A_WIDTH), BF16),
            jax.ShapeDtypeStruct((rows, A_WIDTH), BF16),
            jax.ShapeDtypeStruct((rows // MLSTM_CHUNK, A_WIDTH, MLSTM_CHUNK), BF16),
            jax.ShapeDtypeStruct((rows, A_WIDTH), BF16),
            jax.ShapeDtypeStruct((rows, A_WIDTH), F32),
            jax.ShapeDtypeStruct((N_GATES, rows), F32),
        ],
        scratch_shapes=[pltpu.VMEM((ROW_TILE + 2 * SUBLANES, 2 * A_WIDTH), F32)],
        compiler_params=_params("arbitrary"),
        name="proj_in",
    )(xa, xb, xb, xb, mods, norm_g, w_all, w_gate_hi, conv_w, conv_b, cos_t, sin_t, qg, kg)


ATTN_KEY_BLOCK = 256


ATTN_Q_TILE = 256


def _attn_kernel(lam_ref, q_ref, *refs, with_latent_keys, post_scale):
    if with_latent_keys:
        kc_ref, vc_ref, kl_ref, vl_ref, g_ref, o_ref, s_ref = refs
    else:
        kc_ref, vc_ref, g_ref, o_ref, s_ref = refs
    lam = lam_ref[0, 0]
    q = q_ref[...]
    lane = lax.broadcasted_iota(jnp.int32, (1, LANES), 1)
    q1 = jnp.where(lane < A_DH, q, jnp.zeros_like(q))
    q2 = jnp.where(lane >= A_DH, q, jnp.zeros_like(q))

    tq = q.shape[0]
    kb = ATTN_KEY_BLOCK

    def attend(blocks):
        qs = (q1, q2)
        run_max = [jnp.full((tq, LANES), -jnp.inf, F32) for _ in range(2)]
        part = [jnp.zeros((tq, LANES), F32) for _ in range(2)]
        acc = [jnp.zeros((tq, LANES), F32) for _ in range(2)]

        def pass1(m, j):
            k_ref, _, r0 = blocks[j]
            s = lax.dot_general(qs[m], k_ref[r0:r0 + kb, :], NT_DIMS, preferred_element_type=F32)
            s_ref[m, :, j * kb:(j + 1) * kb] = s
            for c0 in range(0, kb, LANES):
                run_max[m] = jnp.maximum(run_max[m], s[:, c0:c0 + LANES])

        def row_max(m):
            return jnp.broadcast_to(jnp.max(run_max[m], axis=1, keepdims=True), (tq, kb))

        def pass2(m, j, mx):
            _, v_ref, r0 = blocks[j]
            e = jnp.exp(s_ref[m, :, j * kb:(j + 1) * kb] - mx)
            for c0 in range(0, kb, LANES):
                part[m] = part[m] + e[:, c0:c0 + LANES]
            acc[m] = acc[m] + jnp.dot(e.astype(BF16), v_ref[r0:r0 + kb, :], preferred_element_type=F32)

        for j in range(len(blocks)):
            pass1(0, j)
        mx0 = row_max(0)
        for j in range(len(blocks)):
            pass1(1, j)
            pass2(0, j, mx0)
        mx1 = row_max(1)
        for j in range(len(blocks)):
            pass2(1, j, mx1)
        z = [jnp.sum(p, axis=1, keepdims=True) for p in part]
        out = acc[0] * (1.0 / z[0]) - acc[1] * (lam / z[1])
        o_ref[...] = (_rms(out) * g_ref[...] * post_scale).astype(BF16)

    blocks = [(kc_ref, vc_ref, r0) for r0 in range(0, kc_ref.shape[0], kb)]
    if with_latent_keys:
        blocks += [(kl_ref, vl_ref, r0) for r0 in range(0, kl_ref.shape[0], kb)]
    attend(blocks)


def attention(lam, qk, v, subln_g, post_scale, batch, ctx_len, seq, latent_queries):
    lat_blk0 = batch * ctx_len // seq
    tq = ATTN_Q_TILE if latent_queries else ctx_len
    nq = seq // tq if latent_queries else 1
    q0 = batch * ctx_len // tq if latent_queries else 0
    q_per_batch = nq
    n_keys = ctx_len + (seq if latent_queries else 0)
    in_specs = [
        pl.BlockSpec(memory_space=pltpu.SMEM),
        pl.BlockSpec((tq, LANES), lambda b, h, t: (q0 + b * q_per_batch + t, h)),
        pl.BlockSpec((ctx_len, LANES), lambda b, h, t: (b, A_HEADS + h)),
        pl.BlockSpec((ctx_len, LANES), lambda b, h, t: (b, h)),
    ]
    args = [lam, qk, qk, v]
    if latent_queries:
        in_specs += [pl.BlockSpec((seq, LANES), lambda b, h, t: (lat_blk0 + b, A_HEADS + h)),
                     pl.BlockSpec((seq, LANES), lambda b, h, t: (lat_blk0 + b, h))]
        args += [qk, v]
    in_specs.append(pl.BlockSpec((1, LANES), lambda b, h, t: (0, 0)))
    args.append(subln_g)
    kern = functools.partial(_attn_kernel, with_latent_keys=latent_queries, post_scale=post_scale)
    return pl.pallas_call(
        kern,
        grid=(batch, A_HEADS, nq),
        in_specs=in_specs,
        out_specs=pl.BlockSpec((tq, LANES), lambda b, h, t: (b * q_per_batch + t, h)),
        out_shape=jax.ShapeDtypeStruct((batch * nq * tq, A_WIDTH), BF16),
        scratch_shapes=[pltpu.VMEM((2, tq, n_keys), F32)],
        compiler_params=_params("arbitrary", "arbitrary", "arbitrary"),
        name="diff_attention",
    )(*args)


def _mlstm_kernel(qc_ref, ql_ref, ktc_ref, ktl_ref, vc_ref, vl_ref, oc_ref, ol_ref, grc_ref, grl_ref, gb_ref, mg_ref,
                  outc_ref, outl_ref, row_ref, col_ref, s_ref, hf_ref, hb_ref):
    L = MLSTM_CHUNK
    dh = LANES
    n_ch = 2 * M_HEADS
    n_ctx = qc_ref.shape[0] // L
    n_lat = ql_ref.shape[0] // L
    n_chunks = n_ctx + n_lat
    nt = n_chunks * L

    g = jnp.concatenate([grc_ref[...], grl_ref[...]], axis=1) + gb_ref[...]
    log_i = g[:n_ch]
    log_f = _log_sigmoid(g[n_ch:])
    sub = lax.broadcasted_iota(jnp.int32, (n_ch, nt), 0)
    lane_in_chunk = lax.broadcasted_iota(jnp.int32, (n_ch, nt), 1) & (L - 1)
    fwd_rows = sub < M_HEADS

    def chunk_scan(x, op, ident):
        xf, xb, k = x, x, 1
        while k < L:
            xf = op(xf, jnp.where(lane_in_chunk >= k, pltpu.roll(xf, k, 1), ident))
            xb = op(xb, jnp.where(lane_in_chunk < L - k, pltpu.roll(xb, nt - k, 1), ident))
            k *= 2
        return jnp.where(fwd_rows, xf, xb)

    b = chunk_scan(log_f, jnp.add, 0.0)
    a = log_i - b
    a_run = chunk_scan(a, jnp.maximum, -jnp.inf)

    fwd_col = lax.broadcasted_iota(jnp.int32, (n_ch, 1), 0) < M_HEADS

    def at_chunk_end(x, ci):
        xc = x[:, ci * L:(ci + 1) * L]
        return jnp.where(fwd_col, xc[:, L - 1:L], xc[:, 0:1])

    b_last = [at_chunk_end(b, ci) for ci in range(n_chunks)]
    a_last = [at_chunk_end(a_run, ci) for ci in range(n_chunks)]
    order_f = list(range(n_chunks))
    order_b = list(range(n_ctx - 1, -1, -1)) + list(range(n_chunks - 1, n_ctx - 1, -1))

    def carried(order):
        m, out = jnp.zeros((n_ch, 1), F32), {}
        for ci in order:
            out[ci] = m
            m = b_last[ci] + jnp.maximum(m, a_last[ci])
        return out

    m_f, m_b = carried(order_f), carried(order_b)
    for ci in range(n_chunks):
        m_prev = jnp.where(fwd_col, m_f[ci], m_b[ci])
        sl = slice(ci * L, (ci + 1) * L)
        mu = jnp.maximum(m_prev, a_run[:, sl])
        mu_last = jnp.maximum(m_prev, a_last[ci])
        row_ref[ci] = jnp.concatenate([a[:, sl], jnp.exp(a[:, sl] - mu_last), jnp.broadcast_to(m_prev, (n_ch, L)),
                                       jnp.broadcast_to(jnp.exp(m_prev - mu_last), (n_ch, L))], axis=0)
        tile = jnp.concatenate([mu, b[:, sl] + mu, jnp.zeros((L - 2 * n_ch, L), F32)], axis=0)
        col_ref[ci] = tile.T

    s_ref[...] = jnp.zeros(s_ref.shape, F32)
    rr = lax.broadcasted_iota(jnp.int32, (L, L), 0)
    cc = lax.broadcasted_iota(jnp.int32, (L, L), 1)
    visible = (cc <= rr, cc >= rr)
    ones = jnp.ones((L, dh), BF16)

    def chunk_rows(ci):
        return pl.ds(ci * L if isinstance(ci, int) else pl.multiple_of(ci * L, L), L)

    def twice(t):
        return jnp.concatenate([t, t], axis=1)

    def step(local, glob, q_ref, kt_ref, v_ref):
        for d in range(2):
            rows = chunk_rows(local[d])
            grow = chunk_rows(glob[d])
            cs = col_ref[glob[d]]
            rs = row_ref[glob[d]]
            h_ref = hf_ref if d == 0 else hb_ref
            for hd in range(M_HEADS):
                r = d * M_HEADS + hd
                mu = jnp.broadcast_to(cs[:, r:r + 1], (L, L))
                m_t = jnp.broadcast_to(cs[:, n_ch + r:n_ch + r + 1], (L, dh))
                a_row, w_row, m_prev, decay = (rs[j * n_ch + r:j * n_ch + r + 1, :] for j in range(4))
                w_intra = jnp.exp(jnp.where(visible[d], a_row - mu, -jnp.inf))
                w_inter = jnp.exp(m_prev - mu)
                cols = slice(hd * dh, (hd + 1) * dh)
                q = q_ref[rows, cols]
                kt = kt_ref[local[d], cols, :]
                v_ext = jnp.concatenate([v_ref[rows, cols], ones], axis=1)
                s = (jnp.dot(q, kt, preferred_element_type=F32) * w_intra).astype(BF16)
                state = s_ref[r]
                t = (jnp.dot(s, v_ext, preferred_element_type=F32)
                     + twice(w_inter) * jnp.dot(q, state.astype(BF16), preferred_element_type=F32))
                h_ref[grow, cols] = t[:, :dh] / jnp.maximum(jnp.abs(t[:, dh:]), jnp.exp(-m_t))
                ktw = (kt.astype(F32) * w_row).astype(BF16)
                s_ref[r] = twice(jnp.broadcast_to(decay, (dh, L))) * state + jnp.dot(ktw, v_ext, preferred_element_type=F32)

    for i in range(n_ctx):
        step((i, n_ctx - 1 - i), (i, n_ctx - 1 - i), qc_ref, ktc_ref, vc_ref)

    def lat_step(i, carry):
        step((i, n_lat - 1 - i), (n_ctx + i, n_chunks - 1 - i), ql_ref, ktl_ref, vl_ref)
        return carry

    lax.fori_loop(0, n_lat, lat_step, 0)

    def finish(local, glob, o_ref, out_ref):
        rows = chunk_rows(local)
        grow = chunk_rows(glob)
        for hd in range(M_HEADS):
            cols = slice(hd * dh, (hd + 1) * dh)
            hn = _rms(hf_ref[grow, cols] + hb_ref[grow, cols]) * mg_ref[:, cols]
            out_ref[rows, cols] = (jax.nn.sigmoid(o_ref[rows, cols]) * hn).astype(BF16)

    for i in range(n_ctx):
        finish(i, i, oc_ref, outc_ref)

    def lat_finish(i, carry):
        finish(i, n_ctx + i, ol_ref, outl_ref)
        return carry

    lax.fori_loop(0, n_lat, lat_finish, 0)


def mlstm(mq, mkt, mv, mo, g_row, gate_b, mnorm_g, batch, ctx_len, seq):
    rows, width = mq.shape
    assert MLSTM_CHUNK == LANES and width == M_HEADS * LANES
    lat0 = batch * ctx_len // seq
    n_chunks = (ctx_len + seq) // MLSTM_CHUNK
    ctx = lambda b: (b, 0)
    lat = lambda b: (lat0 + b, 0)
    const = lambda b: (0, 0)
    out_c, out_l = pl.pallas_call(
        _mlstm_kernel,
        grid=(batch,),
        in_specs=[
            pl.BlockSpec((ctx_len, width), ctx), pl.BlockSpec((seq, width), lat),
            pl.BlockSpec((ctx_len // MLSTM_CHUNK, width, MLSTM_CHUNK), lambda b: (b, 0, 0)),
            pl.BlockSpec((seq // MLSTM_CHUNK, width, MLSTM_CHUNK), lambda b: (lat0 + b, 0, 0)),
            pl.BlockSpec((ctx_len, width), ctx), pl.BlockSpec((seq, width), lat),
            pl.BlockSpec((ctx_len, width), ctx), pl.BlockSpec((seq, width), lat),
            pl.BlockSpec((N_GATES, ctx_len), lambda b: (0, b)), pl.BlockSpec((N_GATES, seq), lambda b: (0, lat0 + b)),
            pl.BlockSpec((N_GATES, 1), const),
            pl.BlockSpec((1, width), const),
        ],
        out_specs=[pl.BlockSpec((ctx_len, width), ctx), pl.BlockSpec((seq, width), ctx)],
        out_shape=[jax.ShapeDtypeStruct((batch * ctx_len, width), BF16),
                   jax.ShapeDtypeStruct((batch * seq, width), BF16)],
        scratch_shapes=[
            pltpu.VMEM((n_chunks, 8 * M_HEADS, MLSTM_CHUNK), F32),
            pltpu.VMEM((n_chunks, MLSTM_CHUNK, MLSTM_CHUNK), F32),
            pltpu.VMEM((2 * M_HEADS, LANES, 2 * LANES), F32),
            pltpu.VMEM((ctx_len + seq, width), F32),
            pltpu.VMEM((ctx_len + seq, width), F32),
        ],
        compiler_params=_params("arbitrary"),
        name="mlstm",
    )(mq, mq, mkt, mkt, mv, mv, mo, mo, g_row, g_row, gate_b, mnorm_g)
    return out_c, out_l


def _router_top2(logits):
    lane = lax.broadcasted_iota(jnp.int32, logits.shape, 1)
    logits = jnp.where(lane < N_EXPERTS, logits, -jnp.inf)
    v1 = jnp.max(logits, axis=1, keepdims=True)
    i1 = jnp.min(jnp.where(logits == v1, lane, LANES), axis=1, keepdims=True)
    rest = jnp.where(lane == i1, -jnp.inf, logits)
    v2 = jnp.max(rest, axis=1, keepdims=True)
    i2 = jnp.min(jnp.where(rest == v2, lane, LANES), axis=1, keepdims=True)
    e = jnp.exp(v2 - v1)
    w1 = 1.0 / (1.0 + e)
    w2 = e / (1.0 + e)
    mask = jnp.where(jnp.logical_or(lane == i1, lane == i2), 1.0, 0.0)
    expert = lane & (N_EXPERTS - 1)
    comb = jnp.where(lane < 3 * N_EXPERTS,
                     jnp.where(expert == i1, w1, 0.0) + jnp.where(expert == i2, w2, 0.0), 0.0)
    hi = comb.astype(BF16)
    rem = comb - hi.astype(F32)
    mid = rem.astype(BF16)
    lo = (rem - mid.astype(F32)).astype(BF16)
    wcols = jnp.where(lane < N_EXPERTS, hi, jnp.where(lane < 2 * N_EXPERTS, mid, lo))
    return mask, wcols


def _out_proj_kernel(*refs, with_router, n_ctx_tiles):
    if with_router:
        a_ref, m_ref, w_ref, x_ref, mod_ref, g_ref, rw_ref, x1_ref, h2_ref, sel_ref, wc_ref, cnt_ref = refs
        a = a_ref[...]
        m = m_ref[...]
        x = x_ref[...]
    else:
        a_ref, ac_ref, m_ref, mc_ref, w_ref, xc_ref, x_ref, mod_ref, g_ref, x1_ref, h2_ref = refs
        is_ctx = pl.program_id(0) < n_ctx_tiles
        a = jnp.where(is_ctx, ac_ref[...], a_ref[...])
        m = jnp.where(is_ctx, mc_ref[...], m_ref[...])
        x = jnp.where(is_ctx, xc_ref[...], x_ref[...])
    half = a_ref.shape[1]
    mix = (jnp.dot(a, w_ref[:half, :], preferred_element_type=F32)
           + jnp.dot(m, w_ref[half:, :], preferred_element_type=F32))
    x1 = x + mod_ref[0, 2:3, :] * mix
    x1_ref[...] = x1
    h2 = _rms(x1) * g_ref[...] * (1.0 + mod_ref[0, 4:5, :]) + mod_ref[0, 3:4, :]
    hb = h2.astype(BF16)
    h2_ref[...] = hb
    if with_router:
        p = jnp.dot(hb, rw_ref[...], preferred_element_type=F32)
        h_lo = (h2 - hb.astype(F32)).astype(BF16)
        logits = p[:, :LANES] + p[:, LANES:] + jnp.dot(h_lo, rw_ref[:, :LANES], preferred_element_type=F32)
        mask, wcols = _router_top2(logits)
        sel_ref[...] = mask.astype(BF16)
        wc_ref[...] = wcols
        cnt_ref[0] = jnp.sum(mask, axis=0, keepdims=True)


def out_proj(a_lat, a_ctx, m_lat, m_ctx, w_out, x_ctx, x_lat, lat_first_tile, mods, norm_g, router_w,
             n_ctx_tiles, tiles_per_seq):
    d = x_lat.shape[1]
    half = a_lat.shape[1]
    with_router = router_w is not None
    assert with_router == (m_ctx is None) == (x_ctx is None) == (a_ctx is None)
    first_tile = n_ctx_tiles if m_ctx is None else 0
    n_lat_tiles = x_lat.shape[0] // ROW_TILE - lat_first_tile
    n_tiles = n_lat_tiles + n_ctx_tiles - first_tile
    out_rows = n_tiles * ROW_TILE

    def seg(i):
        t = i + first_tile
        return jnp.where(t < n_ctx_tiles, 0, 1 + (t - n_ctx_tiles) // tiles_per_seq)

    dst = lambda i: (i, 0)
    lat = lambda i: (jnp.maximum(i + first_tile - n_ctx_tiles, 0), 0)
    ctx = lambda i: (jnp.minimum(i, n_ctx_tiles - 1), 0)
    const = lambda i: (0, 0)
    in_specs, args = [], []
    for t_lat, t_ctx in ((a_lat, a_ctx), (m_lat, m_ctx)):
        in_specs.append(pl.BlockSpec((ROW_TILE, half), lat))
        args.append(t_lat)
        if t_ctx is not None:
            in_specs.append(pl.BlockSpec((ROW_TILE, half), ctx))
            args.append(t_ctx)
    in_specs.append(pl.BlockSpec(w_out.shape, const))
    args.append(w_out)
    if x_ctx is not None:
        in_specs.append(pl.BlockSpec((ROW_TILE, d), ctx))
        args.append(x_ctx)
    in_specs += [
        pl.BlockSpec((ROW_TILE, d), lambda i: (lat(i)[0] + lat_first_tile, 0)),
        pl.BlockSpec((1, 6, d), lambda i: (seg(i), 0, 0)),
        pl.BlockSpec((1, d), const),
    ]
    args += [x_lat, mods, norm_g]
    out_specs = [pl.BlockSpec((ROW_TILE, d), dst), pl.BlockSpec((ROW_TILE, d), dst)]
    out_shape = [jax.ShapeDtypeStruct((out_rows, d), F32), jax.ShapeDtypeStruct((out_rows, d), BF16)]
    if with_router:
        pad = ((0, 0), (0, LANES - router_w.shape[1]))
        rw = jnp.concatenate([jnp.pad(t, pad) for t in _split_bf16(router_w)], axis=1)
        in_specs.append(pl.BlockSpec(rw.shape, const))
        out_specs += [pl.BlockSpec((ROW_TILE, LANES), dst), pl.BlockSpec((ROW_TILE, LANES), dst),
                      pl.BlockSpec((1, 1, LANES), lambda i: (i, 0, 0))]
        out_shape += [jax.ShapeDtypeStruct((out_rows, LANES), BF16),
                      jax.ShapeDtypeStruct((out_rows, LANES), BF16),
                      jax.ShapeDtypeStruct((n_tiles, 1, LANES), F32)]
        args.append(rw)
    return pl.pallas_call(
        functools.partial(_out_proj_kernel, with_router=with_router, n_ctx_tiles=n_ctx_tiles),
        grid=(n_tiles,),
        in_specs=in_specs, out_specs=out_specs, out_shape=out_shape,
        compiler_params=_params("arbitrary"),
        name="out_proj",
    )(*args)


def _ffn_kernel(h_ref, w1_ref, w3_ref, w2_ref, x_ref, mod_ref, o_ref, acc_ref):
    f = pl.program_id(1)

    @pl.when(f == 0)
    def _():
        acc_ref[...] = jnp.zeros(acc_ref.shape, F32)

    h = h_ref[...]
    a = _silu(jnp.dot(h, w1_ref[...], preferred_element_type=F32)) * jnp.dot(h, w3_ref[...], preferred_element_type=F32)
    acc_ref[...] += jnp.dot(a.astype(BF16), w2_ref[...], preferred_element_type=F32)

    @pl.when(f == pl.num_programs(1) - 1)
    def _():
        o_ref[...] = x_ref[...] + mod_ref[0, 5:6, :] * acc_ref[...]


def ffn(h2, w1, w3, w2, x1, mods, tm, tf, n_ctx_rows, rows_per_seq):
    rows, d = x1.shape
    dff = w1.shape[1]

    def seg(i):
        r0 = i * tm
        return jnp.where(r0 < n_ctx_rows, 0, 1 + (r0 - n_ctx_rows) // rows_per_seq)

    return pl.pallas_call(
        _ffn_kernel,
        grid=(rows // tm, dff // tf),
        in_specs=[
            pl.BlockSpec((tm, d), lambda i, f: (i, 0)),
            pl.BlockSpec((d, tf), lambda i, f: (0, f)),
            pl.BlockSpec((d, tf), lambda i, f: (0, f)),
            pl.BlockSpec((tf, d), lambda i, f: (f, 0)),
            pl.BlockSpec((tm, d), lambda i, f: (i, 0)),
            pl.BlockSpec((1, 6, d), lambda i, f: (seg(i), 0, 0)),
        ],
        out_specs=pl.BlockSpec((tm, d), lambda i, f: (i, 0)),
        out_shape=jax.ShapeDtypeStruct((rows, d), F32),
        scratch_shapes=[pltpu.VMEM((tm, d), F32)],
        compiler_params=_params("arbitrary", "arbitrary"),
        name="ffn",
    )(h2, w1, w3, w2, x1, mods)


MOE_TOKEN_TILE = 512
MOE_ROW_TILE = 512
MOE_FF_TILE = 1792
SLOT_ALIGN = 16
STAGE_ROWS = 2 * MOE_TOKEN_TILE + LANES


def _pow2_chunks(max_chunks):
    k = 1
    while k * 2 <= max_chunks:
        k *= 2
    while k >= 1:
        yield k
        k //= 2


def _segment_copies(n_chunks, src, dst, src_ref, dst_ref, sem, max_chunks, action):
    for k in _pow2_chunks(max_chunks):
        rows = k * SLOT_ALIGN
        taken = (n_chunks & ~(2 * k - 1)) * SLOT_ALIGN

        @pl.when((n_chunks & k) != 0)
        def _(rows=rows, taken=taken):
            s = pl.multiple_of(src + taken, SLOT_ALIGN)
            t = pl.multiple_of(dst + taken, SLOT_ALIGN)
            cp = pltpu.make_async_copy(src_ref.at[pl.ds(s, rows), :], dst_ref.at[pl.ds(t, rows), :], sem)
            if action == "start":
                cp.start()
            else:
                cp.wait()


def _slot_one_hot(sel_ref, off_ref):
    sel = sel_ref[...]
    tt = sel.shape[0]
    r = lax.broadcasted_iota(jnp.int32, (tt, tt), 0)
    c = lax.broadcasted_iota(jnp.int32, (tt, tt), 1)
    earlier = jnp.where(c < r, 1.0, 0.0).astype(BF16)
    rank = jnp.dot(earlier, sel, preferred_element_type=F32)
    pos = off_ref[0] + rank
    chosen = sel > 0
    pa = jnp.min(jnp.where(chosen, pos, float(STAGE_ROWS)), axis=1, keepdims=True)
    pb = jnp.max(jnp.where(chosen, pos, -1.0), axis=1, keepdims=True)
    slot = lax.broadcasted_iota(jnp.int32, (tt, STAGE_ROWS), 1).astype(F32)
    return jnp.where(jnp.logical_or(slot == pa, slot == pb), 1.0, 0.0).astype(BF16)


def _dispatch_kernel(seg_start_ref, seg_chunks_ref, tile_off_ref, tail_start_ref, tail_chunks_ref, n_used_ref,
                     h_ref, wc_ref, sel_ref, off_ref, xs_ref, stage_ref, zero_ref, sem):
    t = pl.program_id(0)
    one_hot = _slot_one_hot(sel_ref, off_ref)
    payload = jnp.concatenate([h_ref[...], wc_ref[...]], axis=1)
    stage_ref[...] = lax.dot_general(one_hot, payload, TN_DIMS, preferred_element_type=F32).astype(BF16)
    max_chunks = MOE_TOKEN_TILE // SLOT_ALIGN
    for action in ("start", "wait"):
        for e in range(N_EXPERTS):
            idx = t * N_EXPERTS + e
            _segment_copies(seg_chunks_ref[idx], tile_off_ref[idx], seg_start_ref[idx],
                            stage_ref, xs_ref, sem, max_chunks, action)

    @pl.when(t == pl.num_programs(0) - 1)
    def _():
        zero_ref[...] = jnp.zeros(zero_ref.shape, BF16)
        for action in ("start", "wait"):
            for e in range(N_EXPERTS):
                _segment_copies(tail_chunks_ref[e], 0, tail_start_ref[e], zero_ref, xs_ref, sem,
                                MOE_ROW_TILE // SLOT_ALIGN - 1, action)

        def clear_tile(j, carry):
            rows = pl.ds(pl.multiple_of(j * MOE_ROW_TILE, MOE_ROW_TILE), MOE_ROW_TILE)
            cp = pltpu.make_async_copy(zero_ref, xs_ref.at[rows, :], sem)
            cp.start()
            cp.wait()
            return carry

        lax.fori_loop(n_used_ref[0], xs_ref.shape[0] // MOE_ROW_TILE, clear_tile, 0)


def moe_dispatch(plan, h2, wcols, sel, n_slots):
    rows, d = h2.shape
    tt = MOE_TOKEN_TILE
    width = d + LANES
    tile = lambda t, *_: (t, 0)
    grid_spec = pltpu.PrefetchScalarGridSpec(
        num_scalar_prefetch=6,
        grid=(rows // tt,),
        in_specs=[
            pl.BlockSpec((tt, d), tile),
            pl.BlockSpec((tt, LANES), tile),
            pl.BlockSpec((tt, LANES), tile),
            pl.BlockSpec((1, 1, LANES), lambda t, *_: (t, 0, 0)),
        ],
        out_specs=pl.BlockSpec(memory_space=pl.ANY),
        scratch_shapes=[pltpu.VMEM((STAGE_ROWS, width), BF16),
                        pltpu.VMEM((MOE_ROW_TILE, width), BF16),
                        pltpu.SemaphoreType.DMA],
    )
    return pl.pallas_call(
        _dispatch_kernel,
        grid_spec=grid_spec,
        out_shape=jax.ShapeDtypeStruct((n_slots, width), BF16),
        compiler_params=_params("arbitrary"),
        name="moe_dispatch",
    )(plan["seg_start"], plan["seg_chunks"], plan["tile_off"], plan["tail_start"], plan["tail_chunks"],
      plan["n_used"], h2, wcols, sel, plan["off_lanes"])


def _moe_kernel(tile_expert_ref, n_used_ref, xs_ref, w1_ref, w3_ref, w2_ref, ys_ref, acc_ref):
    i = pl.program_id(0)
    f = pl.program_id(1)

    @pl.when(i < n_used_ref[0])
    def _():
        @pl.when(f == 0)
        def _():
            acc_ref[...] = jnp.zeros(acc_ref.shape, F32)

        d = w1_ref.shape[1]
        x = xs_ref[:, :d]
        wc = xs_ref[:, d:].astype(F32)
        lane = lax.broadcasted_iota(jnp.int32, wc.shape, 1)
        mine = jnp.logical_and((lane & (N_EXPERTS - 1)) == tile_expert_ref[i], lane < 3 * N_EXPERTS)
        wt = jnp.sum(jnp.where(mine, wc, 0.0), axis=1, keepdims=True)
        a = _silu(jnp.dot(x, w1_ref[0], preferred_element_type=F32)) * jnp.dot(x, w3_ref[0], preferred_element_type=F32)
        acc_ref[...] += jnp.dot((a * wt).astype(BF16), w2_ref[0], preferred_element_type=F32)

        @pl.when(f == pl.num_programs(1) - 1)
        def _():
            ys_ref[...] = acc_ref[...].astype(BF16)

    @pl.when(jnp.logical_and(i >= n_used_ref[0], f == 0))
    def _():
        ys_ref[...] = jnp.zeros(ys_ref.shape, BF16)


def moe_experts(plan, xs, w1, w3, w2):
    n_slots, width = xs.shape
    ne, d, dff = w1.shape
    tm, tf = MOE_ROW_TILE, MOE_FF_TILE
    n_f = dff // tf

    def row(i, f, te, nu):
        return jnp.minimum(i, nu[0] - 1)

    def ff(i, f, te, nu):
        return jnp.where(i < nu[0], f, n_f - 1)

    grid_spec = pltpu.PrefetchScalarGridSpec(
        num_scalar_prefetch=2,
        grid=(n_slots // tm, n_f),
        in_specs=[
            pl.BlockSpec((tm, width), lambda i, f, te, nu: (row(i, f, te, nu), 0)),
            pl.BlockSpec((1, d, tf), lambda i, f, te, nu: (te[row(i, f, te, nu)], 0, ff(i, f, te, nu))),
            pl.BlockSpec((1, d, tf), lambda i, f, te, nu: (te[row(i, f, te, nu)], 0, ff(i, f, te, nu))),
            pl.BlockSpec((1, tf, d), lambda i, f, te, nu: (te[row(i, f, te, nu)], ff(i, f, te, nu), 0)),
        ],
        out_specs=pl.BlockSpec((tm, d), lambda i, f, te, nu: (i, 0)),
        scratch_shapes=[pltpu.VMEM((tm, d), F32)],
    )
    return pl.pallas_call(
        _moe_kernel,
        grid_spec=grid_spec,
        out_shape=jax.ShapeDtypeStruct((n_slots, d), BF16),
        compiler_params=_params("arbitrary", "arbitrary"),
        name="moe_experts",
    )(plan["tile_expert"], plan["n_used"], xs, w1, w3, w2)


def _combine_kernel(seg_start_ref, seg_chunks_ref, tile_off_ref,
                    ys_ref, sel_ref, off_ref, x_ref, mod_ref, o_ref, ybuf_ref, sem):
    t = pl.program_id(0)

    @pl.when(t == 0)
    def _():
        ybuf_ref[...] = jnp.zeros(ybuf_ref.shape, BF16)

    max_chunks = MOE_TOKEN_TILE // SLOT_ALIGN

    def copies(action):
        for e in range(N_EXPERTS):
            idx = t * N_EXPERTS + e
            _segment_copies(seg_chunks_ref[idx], seg_start_ref[idx], tile_off_ref[idx],
                            ys_ref, ybuf_ref, sem, max_chunks, action)

    copies("start")
    one_hot = _slot_one_hot(sel_ref, off_ref)
    copies("wait")
    y = jnp.dot(one_hot, ybuf_ref[...], preferred_element_type=F32)
    o_ref[...] = x_ref[...] + mod_ref[0, 5:6, :] * y


def moe_combine(plan, ys, sel, x1, mods, rows_per_seq):
    rows, d = x1.shape
    tt = MOE_TOKEN_TILE
    tile = lambda t, *_: (t, 0)
    grid_spec = pltpu.PrefetchScalarGridSpec(
        num_scalar_prefetch=3,
        grid=(rows // tt,),
        in_specs=[
            pl.BlockSpec(memory_space=pl.ANY),
            pl.BlockSpec((tt, LANES), tile),
            pl.BlockSpec((1, 1, LANES), lambda t, *_: (t, 0, 0)),
            pl.BlockSpec((tt, d), tile),
            pl.BlockSpec((1, 6, d), lambda t, *_: (1 + (t * tt) // rows_per_seq, 0, 0)),
        ],
        out_specs=pl.BlockSpec((tt, d), tile),
        scratch_shapes=[pltpu.VMEM((STAGE_ROWS, d), BF16), pltpu.SemaphoreType.DMA],
    )
    return pl.pallas_call(
        _combine_kernel,
        grid_spec=grid_spec,
        out_shape=jax.ShapeDtypeStruct((rows, d), F32),
        compiler_params=_params("arbitrary"),
        name="moe_combine",
    )(plan["seg_start"], plan["seg_chunks"], plan["tile_off"], ys, sel, plan["off_lanes"], x1, mods)


def _moe_plan(counts, n_row_tiles):
    counts = counts.astype(jnp.int32)
    seg = (counts + SLOT_ALIGN - 1) // SLOT_ALIGN * SLOT_ALIGN
    tile_off = jnp.cumsum(seg, axis=1) - seg
    total = jnp.sum(seg, axis=0)
    region = (total + MOE_ROW_TILE - 1) // MOE_ROW_TILE * MOE_ROW_TILE
    region_end = jnp.cumsum(region)
    region_start = region_end - region
    seg_start = region_start[None, :] + jnp.cumsum(seg, axis=0) - seg
    first_row = jnp.arange(n_row_tiles, dtype=jnp.int32) * MOE_ROW_TILE
    tile_expert = jnp.minimum(jnp.sum(first_row[:, None] >= region_end[None, :], axis=1), counts.shape[1] - 1)
    off_lanes = jnp.pad(tile_off.astype(F32), ((0, 0), (0, LANES - counts.shape[1])))
    return {
        "seg_start": seg_start.reshape(-1),
        "seg_chunks": (seg // SLOT_ALIGN).reshape(-1),
        "tile_off": tile_off.reshape(-1),
        "tail_start": region_start + total,
        "tail_chunks": (region - total) // SLOT_ALIGN,
        "tile_expert": tile_expert.astype(jnp.int32),
        "n_used": (region_end[-1:] // MOE_ROW_TILE).astype(jnp.int32),
        "off_lanes": off_lanes.reshape(counts.shape[0], 1, LANES),
    }


def _rope_tables(ctx_len, seq):
    rows = seq // GRID_W
    row = jnp.repeat(jnp.arange(rows, dtype=F32), GRID_W)
    col = jnp.tile(jnp.arange(GRID_W, dtype=F32), rows)
    inv = ROPE_BASE ** (-jnp.arange(ROPE_PAIRS_PER_AXIS, dtype=F32) / ROPE_PAIRS_PER_AXIS)
    ang = jnp.concatenate([row[:, None] * inv, col[:, None] * inv], axis=-1)
    cos, sin = jnp.cos(ang), jnp.sin(ang)
    cos_t = jnp.concatenate([cos, cos, cos, cos], axis=-1)
    sin_t = jnp.concatenate([-sin, sin, -sin, sin], axis=-1)
    cos_t = jnp.concatenate([jnp.ones((ctx_len, LANES), F32), cos_t], axis=0)
    sin_t = jnp.concatenate([jnp.zeros((ctx_len, LANES), F32), sin_t], axis=0)
    return cos_t, sin_t


def kernel(x, c, ctx, c_ctx, ada_w, ada_b, norm1_g, norm2_g, w_in, w_out, q_norm_g, k_norm_g, lambda_q1, lambda_k1, lambda_q2, lambda_k2, subln_g, conv_w, conv_b, gate_b, mnorm_g, ffn_w1, ffn_w3, ffn_w2, router_w, moe_w1, moe_w3, moe_w2):
    batch, seq, d = x.shape
    ctx_len = ctx.shape[1]
    depth = ada_w.shape[0]
    assert depth == 2, "layer 0 runs the dense FFN on all rows, layer 1 the experts on the latents"
    n_ctx_rows = batch * ctx_len
    n_ctx_tiles = n_ctx_rows // ROW_TILE
    tiles_per_seq = seq // ROW_TILE
    n_main = w_in.shape[2] - N_GATES
    assert n_main == 7 * A_WIDTH
    m_dh = (n_main - 3 * A_WIDTH) // 4 // M_HEADS
    gate_order = jnp.array(GATE_ORDER, jnp.int32)
    lane_pad = ((0, 0), (0, LANES - N_GATES))

    x_ctx = ctx.reshape(n_ctx_rows, d)
    x_lat = x.reshape(batch * seq, d)
    lat_first_tile = 0

    cc = jnp.zeros((2 * SUBLANES, d), F32).at[:batch].set(c).at[batch].set(c_ctx)
    mod_rows = adaln_all(cc, ada_w, ada_b)
    mods_all = jnp.concatenate([mod_rows[:, batch:batch + 1], mod_rows[:, :batch]], axis=1)
    mods_all = mods_all.reshape(depth, batch + 1, 6, d)

    cos_t, sin_t = _rope_tables(ctx_len, seq)

    for l in range(depth):
        last = l == depth - 1
        lam_init = 0.8 - 0.6 * math.exp(-0.3 * l)
        lam = (jnp.exp(jnp.sum(lambda_q1[l] * lambda_k1[l]).astype(F32))
               - jnp.exp(jnp.sum(lambda_q2[l] * lambda_k2[l]).astype(F32)) + lam_init)
        mods = mods_all[l]
        g_hi, g_lo = _split_bf16(w_in[l][:, n_main:][:, gate_order])
        g_hi, g_lo = jnp.pad(g_hi, lane_pad), jnp.pad(g_lo, lane_pad)
        w_all = jnp.concatenate([w_in[l, :, :n_main].astype(BF16), g_hi, g_lo], axis=1)
        qg = jnp.tile(q_norm_g[l] * (A_DH ** -0.5), 2).reshape(1, LANES)
        kg = jnp.tile(k_norm_g[l], 2).reshape(1, LANES)

        qk, av, mq, mkt, mv, mo, g_row = proj_in(
            x_ctx, x_lat, lat_first_tile, mods, norm1_g[l].reshape(1, d), w_all, g_hi,
            conv_w[l], conv_b[l].reshape(1, -1), cos_t, sin_t, qg, kg, n_ctx_tiles, tiles_per_seq, m_dh ** -0.5)
        attn = functools.partial(attention, lam.reshape(1, 1), qk, av, subln_g[l].reshape(1, LANES),
                                 1.0 - lam_init, batch, ctx_len, seq)
        a_out = attn(latent_queries=True)
        m_ctx, m_out = mlstm(mq, mkt, mv, mo, g_row, gate_b[l].reshape(-1)[gate_order].reshape(N_GATES, 1),
                             mnorm_g[l].reshape(1, -1), batch, ctx_len, seq)
        w_out_b = w_out[l].astype(BF16)
        j = l // 2
        if not last:
            x1, h2 = out_proj(a_out, attn(latent_queries=False), m_out, m_ctx, w_out_b, x_ctx, x_lat, lat_first_tile,
                              mods, norm2_g[l].reshape(1, d), None, n_ctx_tiles, tiles_per_seq)
            xs = ffn(h2, ffn_w1[j].astype(BF16), ffn_w3[j].astype(BF16), ffn_w2[j].astype(BF16), x1, mods,
                     512, 1408, n_ctx_rows, seq)
            x_ctx, x_lat, lat_first_tile = xs, xs, n_ctx_tiles
        else:
            x1, h2, sel, wcols, cnt = out_proj(a_out, None, m_out, None, w_out_b, None, x_lat, lat_first_tile, mods,
                                               norm2_g[l].reshape(1, d), router_w[j], n_ctx_tiles, tiles_per_seq)
            n_tok_tiles = x1.shape[0] // MOE_TOKEN_TILE
            counts = cnt[:, 0, :N_EXPERTS].reshape(n_tok_tiles, MOE_TOKEN_TILE // ROW_TILE, N_EXPERTS).sum(axis=1)
            worst = 2 * x1.shape[0] + n_tok_tiles * N_EXPERTS * (SLOT_ALIGN - 1) + N_EXPERTS * (MOE_ROW_TILE - SLOT_ALIGN)
            n_row_tiles = -(-worst // MOE_ROW_TILE)
            plan = _moe_plan(counts, n_row_tiles)
            slots = moe_dispatch(plan, h2, wcols, sel, n_row_tiles * MOE_ROW_TILE)
            ys = moe_experts(plan, slots, moe_w1[j].astype(BF16), moe_w3[j].astype(BF16), moe_w2[j].astype(BF16))
            out = moe_combine(plan, ys, sel, x1, mods, seq)
    return out.reshape(batch, seq, d)
```

```python
import functools
import math

import jax
import jax.numpy as jnp
from jax import lax
from jax.experimental import pallas as pl
from jax.experimental.pallas import tpu as pltpu

F32 = jnp.float32
BF16 = jnp.bfloat16
HIGHEST = lax.Precision.HIGHEST

GRID_W = 64
A_HEADS = 4
A_DH = 64
A_DV = 2 * A_DH
A_WIDTH = A_HEADS * A_DV
M_HEADS = 4
N_GATES = 4 * M_HEADS
CONV_K = 5
ROPE_BASE = 10000.0
ROPE_PAIRS_PER_AXIS = A_DH // 4
N_EXPERTS = 8
EPS = 1e-6

LANES = 128
SUBLANES = 8
ROW_TILE = 256
MLSTM_CHUNK = 128
VMEM_LIMIT = 56 * 1024 * 1024

NT_DIMS = (((1,), (1,)), ((), ()))
TN_DIMS = (((0,), (0,)), ((), ()))

GATE_ORDER = tuple(d * 2 * M_HEADS + k * M_HEADS + h for k in range(2) for d in range(2) for h in range(M_HEADS))


def _params(*sem):
    return pltpu.CompilerParams(dimension_semantics=sem, vmem_limit_bytes=VMEM_LIMIT)


def _silu(v):
    return v * jax.nn.sigmoid(v)


def _log_sigmoid(v):
    return jnp.minimum(v, 0.0) - jnp.log1p(jnp.exp(-jnp.abs(v)))


def _rms(v, axis=-1):
    return v * lax.rsqrt(jnp.mean(v * v, axis=axis, keepdims=True) + EPS)


def _split_bf16(w):
    hi = w.astype(BF16)
    return hi, (w - hi.astype(F32)).astype(BF16)


def _adaln_kernel(c_ref, w_ref, b_ref, o_ref):
    s = _silu(c_ref[...])
    o_ref[0] = jnp.dot(s, w_ref[0], precision=HIGHEST, preferred_element_type=F32) + b_ref[0]


def adaln_all(cc, ada_w, ada_b):
    depth, d, d6 = ada_w.shape
    n_col = d6 // d
    return pl.pallas_call(
        _adaln_kernel,
        grid=(depth, n_col),
        in_specs=[
            pl.BlockSpec(cc.shape, lambda l, j: (0, 0)),
            pl.BlockSpec((1, d, d), lambda l, j: (l, 0, j)),
            pl.BlockSpec((1, 1, d), lambda l, j: (l, 0, j)),
        ],
        out_specs=pl.BlockSpec((1, cc.shape[0], d), lambda l, j: (l, 0, j)),
        out_shape=jax.ShapeDtypeStruct((depth, cc.shape[0], d6), F32),
        compiler_params=_params("arbitrary", "arbitrary"),
        name="adaln",
    )(cc, ada_w, ada_b.reshape(depth, 1, d6))


def _proj_in_kernel(xa_ref, xb_ref, prev_ref, next_ref, mod_ref, g_ref, w_ref, wgh_ref, cw_ref, cb_ref,
                    cos_ref, sin_ref, qg_ref, kg_ref,
                    qk_ref, v_ref, mq_ref, mkt_ref, mv_ref, mo_ref, gr_ref, ext_ref,
                    *, n_ctx_tiles, tiles_per_seq, k_scale):
    i = pl.program_id(0)
    is_ctx = i < n_ctx_tiles
    pos = (i - n_ctx_tiles) % tiles_per_seq
    first = jnp.logical_or(is_ctx, pos == 0)
    last = jnp.logical_or(is_ctx, pos == tiles_per_seq - 1)
    tm = xa_ref.shape[0]
    halo = SUBLANES

    x_ext = jnp.concatenate([prev_ref[...], jnp.where(is_ctx, xa_ref[...], xb_ref[...]), next_ref[...]], axis=0)
    h_ext = _rms(x_ext) * g_ref[...] * (1.0 + mod_ref[0, 1:2, :]) + mod_ref[0, 0:1, :]
    h = h_ext[halo:halo + tm]
    hb = h.astype(BF16)

    def proj(j):
        return jnp.dot(hb, w_ref[:, j * A_WIDTH:(j + 1) * A_WIDTH], preferred_element_type=F32)

    r = lax.broadcasted_iota(jnp.int32, (LANES, LANES), 0)
    c = lax.broadcasted_iota(jnp.int32, (LANES, LANES), 1)
    group_ones = jnp.where((r & -A_DH) == (c & -A_DH), 1.0, 0.0).astype(BF16)
    lane = lax.broadcasted_iota(jnp.int32, (1, LANES), 1)
    first_half = (lane & (A_DH - 1)) < (A_DH // 2)
    cos = cos_ref[...]
    sin = sin_ref[...]
    for j, gain_ref in ((0, qg_ref), (1, kg_ref)):
        acc = proj(j)
        for hd in range(A_HEADS):
            t = acc[:, hd * LANES:(hd + 1) * LANES]
            ss = jnp.dot((t * t).astype(BF16), group_ones, preferred_element_type=F32)
            tn = t * lax.rsqrt(ss * (1.0 / A_DH) + EPS) * gain_ref[...]
            rot = jnp.where(first_half, pltpu.roll(tn, LANES - A_DH // 2, 1), pltpu.roll(tn, A_DH // 2, 1))
            qk_ref[:, j * A_WIDTH + hd * LANES: j * A_WIDTH + (hd + 1) * LANES] = (tn * cos + rot * sin).astype(BF16)
    v_ref[...] = proj(2).astype(BF16)

    pre = jnp.dot(h_ext.astype(BF16), w_ref[:, 3 * A_WIDTH:5 * A_WIDTH], preferred_element_type=F32)
    ext_ref[0:halo, :] = jnp.where(first, 0.0, pre[0:halo])
    ext_ref[halo:halo + tm, :] = pre[halo:halo + tm]
    ext_ref[halo + tm:, :] = jnp.where(last, 0.0, pre[halo + tm:])
    acc = jnp.zeros((tm, 2 * A_WIDTH), F32) + cb_ref[...]
    pad = CONV_K // 2
    for j in range(CONV_K):
        acc = acc + cw_ref[j:j + 1, :] * ext_ref[halo - pad + j:halo - pad + j + tm, :]
    y = _silu(acc)
    mq_ref[...] = y[:, :A_WIDTH].astype(BF16)
    for ci in range(tm // MLSTM_CHUNK):
        yk = y[ci * MLSTM_CHUNK:(ci + 1) * MLSTM_CHUNK, A_WIDTH:] * k_scale
        mkt_ref[ci] = yk.T.astype(BF16)

    mv_ref[...] = proj(5).astype(BF16)
    mo_ref[...] = proj(6)

    n_main = 7 * A_WIDTH
    p = jnp.dot(hb, w_ref[:, n_main:n_main + 2 * LANES], preferred_element_type=F32)
    h_lo = (h - hb.astype(F32)).astype(BF16)
    gates = p[:, :LANES] + p[:, LANES:] + jnp.dot(h_lo, wgh_ref[...], preferred_element_type=F32)
    gr_ref[...] = gates.T[:N_GATES, :]


def proj_in(xa, xb, b_first_tile, mods, norm_g, w_all, w_gate_hi, conv_w, conv_b, cos_t, sin_t, qg, kg,
            n_ctx_tiles, tiles_per_seq, k_scale):
    d = xa.shape[1]
    n_lat_tiles = xb.shape[0] // ROW_TILE - b_first_tile
    n_tiles = n_ctx_tiles + n_lat_tiles
    rows = n_tiles * ROW_TILE
    per = ROW_TILE // SUBLANES

    def seg(i):
        return jnp.where(i < n_ctx_tiles, 0, 1 + (i - n_ctx_tiles) // tiles_per_seq)

    def rope_blk(i):
        return jnp.where(i < n_ctx_tiles, 0, 1 + (i - n_ctx_tiles) % tiles_per_seq)

    def lat(i):
        return jnp.maximum(i - n_ctx_tiles, 0)

    row = lambda i: (i, 0)
    const = lambda i: (0, 0)
    kern = functools.partial(_proj_in_kernel, n_ctx_tiles=n_ctx_tiles, tiles_per_seq=tiles_per_seq, k_scale=k_scale)
    return pl.pallas_call(
        kern,
        grid=(n_tiles,),
        in_specs=[
            pl.BlockSpec((ROW_TILE, d), lambda i: (jnp.minimum(i, n_ctx_tiles - 1), 0)),
            pl.BlockSpec((ROW_TILE, d), lambda i: (lat(i) + b_first_tile, 0)),
            pl.BlockSpec((SUBLANES, d), lambda i: (jnp.maximum(lat(i) * per - 1, 0) + b_first_tile * per, 0)),
            pl.BlockSpec((SUBLANES, d),
                         lambda i: (jnp.minimum((lat(i) + 1) * per, n_lat_tiles * per - 1) + b_first_tile * per, 0)),
            pl.BlockSpec((1, 6, d), lambda i: (seg(i), 0, 0)),
            pl.BlockSpec((1, d), const),
            pl.BlockSpec(w_all.shape, const),
            pl.BlockSpec(w_gate_hi.shape, const),
            pl.BlockSpec(conv_w.shape, const),
            pl.BlockSpec(conv_b.shape, const),
            pl.BlockSpec((ROW_TILE, LANES), lambda i: (rope_blk(i), 0)),
            pl.BlockSpec((ROW_TILE, LANES), lambda i: (rope_blk(i), 0)),
            pl.BlockSpec((1, LANES), const),
            pl.BlockSpec((1, LANES), const),
        ],
        out_specs=[
            pl.BlockSpec((ROW_TILE, 2 * A_WIDTH), row),
            pl.BlockSpec((ROW_TILE, A_WIDTH), row),
            pl.BlockSpec((ROW_TILE, A_WIDTH), row),
            pl.BlockSpec((ROW_TILE // MLSTM_CHUNK, A_WIDTH, MLSTM_CHUNK), lambda i: (i, 0, 0)),
            pl.BlockSpec((ROW_TILE, A_WIDTH), row),
            pl.BlockSpec((ROW_TILE, A_WIDTH), row),
            pl.BlockSpec((N_GATES, ROW_TILE), lambda i: (0, i)),
        ],
        out_shape=[
            jax.ShapeDtypeStruct((rows, 2 * A_WIDTH), BF16),
            jax.ShapeDtypeStruct((rows, A_WIDTH), BF16),
            jax.ShapeDtypeStruct((rows, A_WIDTH), BF16),
            jax.ShapeDtypeStruct((rows // MLSTM_CHUNK, A_WIDTH, MLSTM_CHUNK), BF16),
            jax.ShapeDtypeStruct((rows, A_WIDTH), BF16),
            jax.ShapeDtypeStruct((rows, A_WIDTH), F32),
            jax.ShapeDtypeStruct((N_GATES, rows), F32),
        ],
        scratch_shapes=[pltpu.VMEM((ROW_TILE + 2 * SUBLANES, 2 * A_WIDTH), F32)],
        compiler_params=_params("arbitrary"),
        name="proj_in",
    )(xa, xb, xb, xb, mods, norm_g, w_all, w_gate_hi, conv_w, conv_b, cos_t, sin_t, qg, kg)


ATTN_KEY_BLOCK = 256


ATTN_SUB_TILE = 256
ATTN_Q_TILE = 1024


def _attn_kernel(lam_ref, q_ref, *refs, with_latent_keys, post_scale):
    if with_latent_keys:
        kc_ref, vc_ref, kl_ref, vl_ref, g_ref, o_ref, s_ref = refs
    else:
        kc_ref, vc_ref, g_ref, o_ref, s_ref = refs
    lam = lam_ref[0, 0]
    lane = lax.broadcasted_iota(jnp.int32, (1, LANES), 1)
    kb = ATTN_KEY_BLOCK
    ts = s_ref.shape[1]
    blocks = [(kc_ref, vc_ref, r0) for r0 in range(0, kc_ref.shape[0], kb)]
    if with_latent_keys:
        blocks += [(kl_ref, vl_ref, r0) for r0 in range(0, kl_ref.shape[0], kb)]

    units = [(u, m) for u in range(q_ref.shape[0] // ts) for m in range(2)]
    state = {}

    def pass1(n, j):
        u, m = units[n]
        if j == 0:
            q = q_ref[u * ts:(u + 1) * ts, :]
            keep = (lane < A_DH) if m == 0 else (lane >= A_DH)
            state[n] = {"q": jnp.where(keep, q, jnp.zeros_like(q)), "max": jnp.full((ts, LANES), -jnp.inf, F32)}
        k_ref, _, r0 = blocks[j]
        s = lax.dot_general(state[n]["q"], k_ref[r0:r0 + kb, :], NT_DIMS, preferred_element_type=F32)
        s_ref[n % 2, :, j * kb:(j + 1) * kb] = s
        for c0 in range(0, kb, LANES):
            state[n]["max"] = jnp.maximum(state[n]["max"], s[:, c0:c0 + LANES])

    def pass2(n, j):
        st = state[n]
        if j == 0:
            st["mx"] = jnp.broadcast_to(jnp.max(st["max"], axis=1, keepdims=True), (ts, kb))
            st["part"] = jnp.zeros((ts, LANES), F32)
            st["acc"] = jnp.zeros((ts, LANES), F32)
        _, v_ref, r0 = blocks[j]
        e = jnp.exp(s_ref[n % 2, :, j * kb:(j + 1) * kb] - st["mx"])
        for c0 in range(0, kb, LANES):
            st["part"] = st["part"] + e[:, c0:c0 + LANES]
        st["acc"] = st["acc"] + jnp.dot(e.astype(BF16), v_ref[r0:r0 + kb, :], preferred_element_type=F32)

    def finish(u):
        first, second = state.pop(2 * u), state.pop(2 * u + 1)
        z1 = jnp.sum(first["part"], axis=1, keepdims=True)
        z2 = jnp.sum(second["part"], axis=1, keepdims=True)
        out = first["acc"] * (1.0 / z1) - second["acc"] * (lam / z2)
        o_ref[u * ts:(u + 1) * ts, :] = (_rms(out) * g_ref[...] * post_scale).astype(BF16)

    for n in range(len(units) + 1):
        for j in range(len(blocks)):
            if n < len(units):
                pass1(n, j)
            if n >= 1:
                pass2(n - 1, j)
        if n >= 1 and units[n - 1][1] == 1:
            finish(units[n - 1][0])


def attention(lam, qk, v, subln_g, post_scale, batch, ctx_len, seq, latent_queries):
    lat_blk0 = batch * ctx_len // seq
    tq = ATTN_Q_TILE if latent_queries else ctx_len
    nq = seq // tq if latent_queries else 1
    q0 = batch * ctx_len // tq if latent_queries else 0
    q_per_batch = nq
    n_keys = ctx_len + (seq if latent_queries else 0)
    in_specs = [
        pl.BlockSpec(memory_space=pltpu.SMEM),
        pl.BlockSpec((tq, LANES), lambda b, h, t: (q0 + b * q_per_batch + t, h)),
        pl.BlockSpec((ctx_len, LANES), lambda b, h, t: (b, A_HEADS + h)),
        pl.BlockSpec((ctx_len, LANES), lambda b, h, t: (b, h)),
    ]
    args = [lam, qk, qk, v]
    if latent_queries:
        in_specs += [pl.BlockSpec((seq, LANES), lambda b, h, t: (lat_blk0 + b, A_HEADS + h)),
                     pl.BlockSpec((seq, LANES), lambda b, h, t: (lat_blk0 + b, h))]
        args += [qk, v]
    in_specs.append(pl.BlockSpec((1, LANES), lambda b, h, t: (0, 0)))
    args.append(subln_g)
    kern = functools.partial(_attn_kernel, with_latent_keys=latent_queries, post_scale=post_scale)
    return pl.pallas_call(
        kern,
        grid=(batch, A_HEADS, nq),
        in_specs=in_specs,
        out_specs=pl.BlockSpec((tq, LANES), lambda b, h, t: (b * q_per_batch + t, h)),
        out_shape=jax.ShapeDtypeStruct((batch * nq * tq, A_WIDTH), BF16),
        scratch_shapes=[pltpu.VMEM((2, min(tq, ATTN_SUB_TILE), n_keys), F32)],
        compiler_params=_params("arbitrary", "arbitrary", "arbitrary"),
        name="diff_attention",
    )(*args)


def _mlstm_kernel(qc_ref, ql_ref, ktc_ref, ktl_ref, vc_ref, vl_ref, oc_ref, ol_ref, grc_ref, grl_ref, gb_ref, mg_ref,
                  outc_ref, outl_ref, row_ref, col_ref, s_ref, hf_ref, hb_ref):
    L = MLSTM_CHUNK
    dh = LANES
    n_ch = 2 * M_HEADS
    n_ctx = qc_ref.shape[0] // L
    n_lat = ql_ref.shape[0] // L
    n_chunks = n_ctx + n_lat
    nt = n_chunks * L

    g = jnp.concatenate([grc_ref[...], grl_ref[...]], axis=1) + gb_ref[...]
    log_i = g[:n_ch]
    log_f = _log_sigmoid(g[n_ch:])
    sub = lax.broadcasted_iota(jnp.int32, (n_ch, nt), 0)
    lane_in_chunk = lax.broadcasted_iota(jnp.int32, (n_ch, nt), 1) & (L - 1)
    fwd_rows = sub < M_HEADS

    def chunk_scan(x, op, ident):
        xf, xb, k = x, x, 1
        while k < L:
            xf = op(xf, jnp.where(lane_in_chunk >= k, pltpu.roll(xf, k, 1), ident))
            xb = op(xb, jnp.where(lane_in_chunk < L - k, pltpu.roll(xb, nt - k, 1), ident))
            k *= 2
        return jnp.where(fwd_rows, xf, xb)

    b = chunk_scan(log_f, jnp.add, 0.0)
    a = log_i - b
    a_run = chunk_scan(a, jnp.maximum, -jnp.inf)

    fwd_col = lax.broadcasted_iota(jnp.int32, (n_ch, 1), 0) < M_HEADS

    def at_chunk_end(x, ci):
        xc = x[:, ci * L:(ci + 1) * L]
        return jnp.where(fwd_col, xc[:, L - 1:L], xc[:, 0:1])

    b_last = [at_chunk_end(b, ci) for ci in range(n_chunks)]
    a_last = [at_chunk_end(a_run, ci) for ci in range(n_chunks)]
    order_f = list(range(n_chunks))
    order_b = list(range(n_ctx - 1, -1, -1)) + list(range(n_chunks - 1, n_ctx - 1, -1))

    def carried(order):
        m, out = jnp.zeros((n_ch, 1), F32), {}
        for ci in order:
            out[ci] = m
            m = b_last[ci] + jnp.maximum(m, a_last[ci])
        return out

    m_f, m_b = carried(order_f), carried(order_b)
    for ci in range(n_chunks):
        m_prev = jnp.where(fwd_col, m_f[ci], m_b[ci])
        sl = slice(ci * L, (ci + 1) * L)
        mu = jnp.maximum(m_prev, a_run[:, sl])
        mu_last = jnp.maximum(m_prev, a_last[ci])
        row_ref[ci] = jnp.concatenate([a[:, sl], jnp.exp(a[:, sl] - mu_last), jnp.broadcast_to(m_prev, (n_ch, L)),
                                       jnp.broadcast_to(jnp.exp(m_prev - mu_last), (n_ch, L))], axis=0)
        tile = jnp.concatenate([mu, b[:, sl] + mu, jnp.zeros((L - 2 * n_ch, L), F32)], axis=0)
        col_ref[ci] = tile.T

    s_ref[...] = jnp.zeros(s_ref.shape, F32)
    rr = lax.broadcasted_iota(jnp.int32, (L, L), 0)
    cc = lax.broadcasted_iota(jnp.int32, (L, L), 1)
    visible = (cc <= rr, cc >= rr)
    ones = jnp.ones((L, dh), BF16)

    def chunk_rows(ci):
        return pl.ds(ci * L if isinstance(ci, int) else pl.multiple_of(ci * L, L), L)

    def twice(t):
        return jnp.concatenate([t, t], axis=1)

    def step(local, glob, q_ref, kt_ref, v_ref):
        for d in range(2):
            rows = chunk_rows(local[d])
            grow = chunk_rows(glob[d])
            cs = col_ref[glob[d]]
            rs = row_ref[glob[d]]
            h_ref = hf_ref if d == 0 else hb_ref
            for hd in range(M_HEADS):
                r = d * M_HEADS + hd
                mu = jnp.broadcast_to(cs[:, r:r + 1], (L, L))
                m_t = jnp.broadcast_to(cs[:, n_ch + r:n_ch + r + 1], (L, dh))
                a_row, w_row, m_prev, decay = (rs[j * n_ch + r:j * n_ch + r + 1, :] for j in range(4))
                w_intra = jnp.exp(jnp.where(visible[d], a_row - mu, -jnp.inf))
                w_inter = jnp.exp(m_prev - mu)
                cols = slice(hd * dh, (hd + 1) * dh)
                q = q_ref[rows, cols]
                kt = kt_ref[local[d], cols, :]
                v_ext = jnp.concatenate([v_ref[rows, cols], ones], axis=1)
                s = (jnp.dot(q, kt, preferred_element_type=F32) * w_intra).astype(BF16)
                state = s_ref[r]
                t = (jnp.dot(s, v_ext, preferred_element_type=F32)
                     + twice(w_inter) * jnp.dot(q, state.astype(BF16), preferred_element_type=F32))
                h_ref[grow, cols] = t[:, :dh] / jnp.maximum(jnp.abs(t[:, dh:]), jnp.exp(-m_t))
                ktw = (kt.astype(F32) * w_row).astype(BF16)
                s_ref[r] = twice(jnp.broadcast_to(decay, (dh, L))) * state + jnp.dot(ktw, v_ext, preferred_element_type=F32)

    for i in range(n_ctx):
        step((i, n_ctx - 1 - i), (i, n_ctx - 1 - i), qc_ref, ktc_ref, vc_ref)

    def lat_step(i, carry):
        step((i, n_lat - 1 - i), (n_ctx + i, n_chunks - 1 - i), ql_ref, ktl_ref, vl_ref)
        return carry

    lax.fori_loop(0, n_lat, lat_step, 0)

    def finish(local, glob, o_ref, out_ref):
        rows = chunk_rows(local)
        grow = chunk_rows(glob)
        for hd in range(M_HEADS):
            cols = slice(hd * dh, (hd + 1) * dh)
            hn = _rms(hf_ref[grow, cols] + hb_ref[grow, cols]) * mg_ref[:, cols]
            out_ref[rows, cols] = (jax.nn.sigmoid(o_ref[rows, cols]) * hn).astype(BF16)

    for i in range(n_ctx):
        finish(i, i, oc_ref, outc_ref)

    def lat_finish(i, carry):
        finish(i, n_ctx + i, ol_ref, outl_ref)
        return carry

    lax.fori_loop(0, n_lat, lat_finish, 0)


def mlstm(mq, mkt, mv, mo, g_row, gate_b, mnorm_g, batch, ctx_len, seq):
    rows, width = mq.shape
    assert MLSTM_CHUNK == LANES and width == M_HEADS * LANES
    lat0 = batch * ctx_len // seq
    n_chunks = (ctx_len + seq) // MLSTM_CHUNK
    ctx = lambda b: (b, 0)
    lat = lambda b: (lat0 + b, 0)
    const = lambda b: (0, 0)
    out_c, out_l = pl.pallas_call(
        _mlstm_kernel,
        grid=(batch,),
        in_specs=[
            pl.BlockSpec((ctx_len, width), ctx), pl.BlockSpec((seq, width), lat),
            pl.BlockSpec((ctx_len // MLSTM_CHUNK, width, MLSTM_CHUNK), lambda b: (b, 0, 0)),
            pl.BlockSpec((seq // MLSTM_CHUNK, width, MLSTM_CHUNK), lambda b: (lat0 + b, 0, 0)),
            pl.BlockSpec((ctx_len, width), ctx), pl.BlockSpec((seq, width), lat),
            pl.BlockSpec((ctx_len, width), ctx), pl.BlockSpec((seq, width), lat),
            pl.BlockSpec((N_GATES, ctx_len), lambda b: (0, b)), pl.BlockSpec((N_GATES, seq), lambda b: (0, lat0 + b)),
            pl.BlockSpec((N_GATES, 1), const),
            pl.BlockSpec((1, width), const),
        ],
        out_specs=[pl.BlockSpec((ctx_len, width), ctx), pl.BlockSpec((seq, width), ctx)],
        out_shape=[jax.ShapeDtypeStruct((batch * ctx_len, width), BF16),
                   jax.ShapeDtypeStruct((batch * seq, width), BF16)],
        scratch_shapes=[
            pltpu.VMEM((n_chunks, 8 * M_HEADS, MLSTM_CHUNK), F32),
            pltpu.VMEM((n_chunks, MLSTM_CHUNK, MLSTM_CHUNK), F32),
            pltpu.VMEM((2 * M_HEADS, LANES, 2 * LANES), F32),
            pltpu.VMEM((ctx_len + seq, width), F32),
            pltpu.VMEM((ctx_len + seq, width), F32),
        ],
        compiler_params=_params("arbitrary"),
        name="mlstm",
    )(mq, mq, mkt, mkt, mv, mv, mo, mo, g_row, g_row, gate_b, mnorm_g)
    return out_c, out_l


def _router_top2(logits):
    lane = lax.broadcasted_iota(jnp.int32, logits.shape, 1)
    logits = jnp.where(lane < N_EXPERTS, logits, -jnp.inf)
    v1 = jnp.max(logits, axis=1, keepdims=True)
    i1 = jnp.min(jnp.where(logits == v1, lane, LANES), axis=1, keepdims=True)
    rest = jnp.where(lane == i1, -jnp.inf, logits)
    v2 = jnp.max(rest, axis=1, keepdims=True)
    i2 = jnp.min(jnp.where(rest == v2, lane, LANES), axis=1, keepdims=True)
    e = jnp.exp(v2 - v1)
    w1 = 1.0 / (1.0 + e)
    w2 = e / (1.0 + e)
    mask = jnp.where(jnp.logical_or(lane == i1, lane == i2), 1.0, 0.0)
    expert = lane & (N_EXPERTS - 1)
    comb = jnp.where(lane < 3 * N_EXPERTS,
                     jnp.where(expert == i1, w1, 0.0) + jnp.where(expert == i2, w2, 0.0), 0.0)
    hi = comb.astype(BF16)
    rem = comb - hi.astype(F32)
    mid = rem.astype(BF16)
    lo = (rem - mid.astype(F32)).astype(BF16)
    wcols = jnp.where(lane < N_EXPERTS, hi, jnp.where(lane < 2 * N_EXPERTS, mid, lo))
    return mask, wcols


def _mixer_router_kernel(a_ref, m_ref, w_ref, x_ref, mod_ref, g_ref, rw_ref, x1_ref, h2_ref, sel_ref, wc_ref, cnt_ref):
    half = a_ref.shape[1]
    mix = (jnp.dot(a_ref[...], w_ref[:half, :], preferred_element_type=F32)
           + jnp.dot(m_ref[...], w_ref[half:, :], preferred_element_type=F32))
    x1 = x_ref[...] + mod_ref[0, 2:3, :] * mix
    x1_ref[...] = x1
    h2 = _rms(x1) * g_ref[...] * (1.0 + mod_ref[0, 4:5, :]) + mod_ref[0, 3:4, :]
    hb = h2.astype(BF16)
    h2_ref[...] = hb
    p = jnp.dot(hb, rw_ref[...], preferred_element_type=F32)
    h_lo = (h2 - hb.astype(F32)).astype(BF16)
    logits = p[:, :LANES] + p[:, LANES:] + jnp.dot(h_lo, rw_ref[:, :LANES], preferred_element_type=F32)
    mask, wcols = _router_top2(logits)
    sel_ref[...] = mask.astype(BF16)
    wc_ref[...] = wcols
    cnt_ref[0] = jnp.sum(mask, axis=0, keepdims=True)


def mixer_router(a_lat, m_lat, w_out, xs, first_row, mods, norm_g, router_w, rows_per_seq):
    d = xs.shape[1]
    half = a_lat.shape[1]
    tm = MOE_TOKEN_TILE
    n_tiles = a_lat.shape[0] // tm
    first_tile = first_row // tm
    dst = lambda i: (i, 0)
    const = lambda i: (0, 0)
    pad = ((0, 0), (0, LANES - router_w.shape[1]))
    rw = jnp.concatenate([jnp.pad(t, pad) for t in _split_bf16(router_w)], axis=1)
    return pl.pallas_call(
        _mixer_router_kernel,
        grid=(n_tiles,),
        in_specs=[
            pl.BlockSpec((tm, half), dst), pl.BlockSpec((tm, half), dst),
            pl.BlockSpec(w_out.shape, const),
            pl.BlockSpec((tm, d), lambda i: (i + first_tile, 0)),
            pl.BlockSpec((1, 6, d), lambda i: (1 + (i * tm) // rows_per_seq, 0, 0)),
            pl.BlockSpec((1, d), const),
            pl.BlockSpec(rw.shape, const),
        ],
        out_specs=[pl.BlockSpec((tm, d), dst), pl.BlockSpec((tm, d), dst),
                   pl.BlockSpec((tm, LANES), dst), pl.BlockSpec((tm, LANES), dst),
                   pl.BlockSpec((1, 1, LANES), lambda i: (i, 0, 0))],
        out_shape=[jax.ShapeDtypeStruct((n_tiles * tm, d), F32),
                   jax.ShapeDtypeStruct((n_tiles * tm, d), BF16),
                   jax.ShapeDtypeStruct((n_tiles * tm, LANES), BF16),
                   jax.ShapeDtypeStruct((n_tiles * tm, LANES), BF16),
                   jax.ShapeDtypeStruct((n_tiles, 1, LANES), F32)],
        compiler_params=_params("arbitrary"),
        name="mixer_router",
    )(a_lat, m_lat, w_out, xs, mods, norm_g, rw)


FFN_ROW_TILE = 512
FFN_FF_TILE = 1408


def _mixer_ffn_kernel(a_ref, ac_ref, m_ref, mc_ref, wo_ref, xc_ref, x_ref, mod_ref, g_ref, w1_ref, w3_ref, w2_ref,
                      o_ref, x1_ref, h2_ref, acc_ref, *, n_ctx_tiles):
    f = pl.program_id(1)

    @pl.when(f == 0)
    def _():
        is_ctx = pl.program_id(0) < n_ctx_tiles
        a = jnp.where(is_ctx, ac_ref[...], a_ref[...])
        m = jnp.where(is_ctx, mc_ref[...], m_ref[...])
        x = jnp.where(is_ctx, xc_ref[...], x_ref[...])
        half = a.shape[1]
        mix = (jnp.dot(a, wo_ref[:half, :], preferred_element_type=F32)
               + jnp.dot(m, wo_ref[half:, :], preferred_element_type=F32))
        x1 = x + mod_ref[0, 2:3, :] * mix
        x1_ref[...] = x1
        h2_ref[...] = (_rms(x1) * g_ref[...] * (1.0 + mod_ref[0, 4:5, :]) + mod_ref[0, 3:4, :]).astype(BF16)
        acc_ref[...] = jnp.zeros(acc_ref.shape, F32)

    h = h2_ref[...]
    a = _silu(jnp.dot(h, w1_ref[...], preferred_element_type=F32)) * jnp.dot(h, w3_ref[...], preferred_element_type=F32)
    acc_ref[...] += jnp.dot(a.astype(BF16), w2_ref[...], preferred_element_type=F32)

    @pl.when(f == pl.num_programs(1) - 1)
    def _():
        o_ref[...] = x1_ref[...] + mod_ref[0, 5:6, :] * acc_ref[...]


def mixer_ffn(a_lat, a_ctx, m_lat, m_ctx, w_out, x_ctx, x_lat, mods, norm_g, w1, w3, w2, rows_per_seq):
    d = x_lat.shape[1]
    half = a_lat.shape[1]
    dff = w1.shape[1]
    tm, tf = FFN_ROW_TILE, FFN_FF_TILE
    n_ctx_tiles = x_ctx.shape[0] // tm
    n_tiles = n_ctx_tiles + x_lat.shape[0] // tm
    per_seq = rows_per_seq // tm
    ctx = lambda i, f: (jnp.minimum(i, n_ctx_tiles - 1), 0)
    lat = lambda i, f: (jnp.maximum(i - n_ctx_tiles, 0), 0)
    const = lambda i, f: (0, 0)

    def seg(i, f):
        return (jnp.where(i < n_ctx_tiles, 0, 1 + (i - n_ctx_tiles) // per_seq), 0, 0)

    return pl.pallas_call(
        functools.partial(_mixer_ffn_kernel, n_ctx_tiles=n_ctx_tiles),
        grid=(n_tiles, dff // tf),
        in_specs=[
            pl.BlockSpec((tm, half), lat), pl.BlockSpec((tm, half), ctx),
            pl.BlockSpec((tm, half), lat), pl.BlockSpec((tm, half), ctx),
            pl.BlockSpec(w_out.shape, const),
            pl.BlockSpec((tm, d), ctx), pl.BlockSpec((tm, d), lat),
            pl.BlockSpec((1, 6, d), seg),
            pl.BlockSpec((1, d), const),
            pl.BlockSpec((d, tf), lambda i, f: (0, f)),
            pl.BlockSpec((d, tf), lambda i, f: (0, f)),
            pl.BlockSpec((tf, d), lambda i, f: (f, 0)),
        ],
        out_specs=pl.BlockSpec((tm, d), lambda i, f: (i, 0)),
        out_shape=jax.ShapeDtypeStruct((n_tiles * tm, d), F32),
        scratch_shapes=[pltpu.VMEM((tm, d), F32), pltpu.VMEM((tm, d), BF16), pltpu.VMEM((tm, d), F32)],
        compiler_params=_params("arbitrary", "arbitrary"),
        name="mixer_ffn",
    )(a_lat, a_ctx, m_lat, m_ctx, w_out, x_ctx, x_lat, mods, norm_g, w1, w3, w2)


MOE_TOKEN_TILE = 512
MOE_ROW_TILE = 512
MOE_FF_TILE = 1792
SLOT_ALIGN = 16
STAGE_ROWS = 2 * MOE_TOKEN_TILE + LANES


def _pow2_chunks(max_chunks):
    k = 1
    while k * 2 <= max_chunks:
        k *= 2
    while k >= 1:
        yield k
        k //= 2


def _segment_copies(n_chunks, src, dst, src_ref, dst_ref, sem, max_chunks, action):
    for k in _pow2_chunks(max_chunks):
        rows = k * SLOT_ALIGN
        taken = (n_chunks & ~(2 * k - 1)) * SLOT_ALIGN

        @pl.when((n_chunks & k) != 0)
        def _(rows=rows, taken=taken):
            s = pl.multiple_of(src + taken, SLOT_ALIGN)
            t = pl.multiple_of(dst + taken, SLOT_ALIGN)
            cp = pltpu.make_async_copy(src_ref.at[pl.ds(s, rows), :], dst_ref.at[pl.ds(t, rows), :], sem)
            if action == "start":
                cp.start()
            else:
                cp.wait()


def _slot_one_hot(sel_ref, off_ref):
    sel = sel_ref[...]
    tt = sel.shape[0]
    r = lax.broadcasted_iota(jnp.int32, (tt, tt), 0)
    c = lax.broadcasted_iota(jnp.int32, (tt, tt), 1)
    earlier = jnp.where(c < r, 1.0, 0.0).astype(BF16)
    rank = jnp.dot(earlier, sel, preferred_element_type=F32)
    pos = off_ref[0] + rank
    chosen = sel > 0
    pa = jnp.min(jnp.where(chosen, pos, float(STAGE_ROWS)), axis=1, keepdims=True)
    pb = jnp.max(jnp.where(chosen, pos, -1.0), axis=1, keepdims=True)
    slot = lax.broadcasted_iota(jnp.int32, (tt, STAGE_ROWS), 1).astype(F32)
    return jnp.where(jnp.logical_or(slot == pa, slot == pb), 1.0, 0.0).astype(BF16)


def _dispatch_kernel(seg_start_ref, seg_chunks_ref, tile_off_ref, tail_start_ref, tail_chunks_ref, n_used_ref,
                     h_ref, wc_ref, sel_ref, off_ref, xs_ref, stage_ref, zero_ref, sem):
    t = pl.program_id(0)
    last = pl.num_programs(0) - 1
    buf = t % 2
    max_chunks = MOE_TOKEN_TILE // SLOT_ALIGN

    def tile_copies(tile, which, action):
        for e in range(N_EXPERTS):
            idx = tile * N_EXPERTS + e
            _segment_copies(seg_chunks_ref[idx], tile_off_ref[idx], seg_start_ref[idx],
                            stage_ref.at[which], xs_ref, sem.at[which], max_chunks, action)

    one_hot = _slot_one_hot(sel_ref, off_ref)
    payload = jnp.concatenate([h_ref[...], wc_ref[...]], axis=1)
    stage_ref[buf] = lax.dot_general(one_hot, payload, TN_DIMS, preferred_element_type=F32).astype(BF16)
    tile_copies(t, buf, "start")

    @pl.when(t > 0)
    def _():
        tile_copies(t - 1, 1 - buf, "wait")

    @pl.when(t == last)
    def _():
        tile_copies(t, buf, "wait")
        zero_ref[...] = jnp.zeros(zero_ref.shape, BF16)
        for action in ("start", "wait"):
            for e in range(N_EXPERTS):
                _segment_copies(tail_chunks_ref[e], 0, tail_start_ref[e], zero_ref, xs_ref, sem.at[0],
                                MOE_ROW_TILE // SLOT_ALIGN - 1, action)

        def clear_tile(j, carry):
            rows = pl.ds(pl.multiple_of(j * MOE_ROW_TILE, MOE_ROW_TILE), MOE_ROW_TILE)
            cp = pltpu.make_async_copy(zero_ref, xs_ref.at[rows, :], sem.at[0])
            cp.start()
            cp.wait()
            return carry

        lax.fori_loop(n_used_ref[0], xs_ref.shape[0] // MOE_ROW_TILE, clear_tile, 0)


def moe_dispatch(plan, h2, wcols, sel, n_slots):
    rows, d = h2.shape
    tt = MOE_TOKEN_TILE
    width = d + LANES
    tile = lambda t, *_: (t, 0)
    grid_spec = pltpu.PrefetchScalarGridSpec(
        num_scalar_prefetch=6,
        grid=(rows // tt,),
        in_specs=[
            pl.BlockSpec((tt, d), tile),
            pl.BlockSpec((tt, LANES), tile),
            pl.BlockSpec((tt, LANES), tile),
            pl.BlockSpec((1, 1, LANES), lambda t, *_: (t, 0, 0)),
        ],
        out_specs=pl.BlockSpec(memory_space=pl.ANY),
        scratch_shapes=[pltpu.VMEM((2, STAGE_ROWS, width), BF16),
                        pltpu.VMEM((MOE_ROW_TILE, width), BF16),
                        pltpu.SemaphoreType.DMA((2,))],
    )
    return pl.pallas_call(
        _dispatch_kernel,
        grid_spec=grid_spec,
        out_shape=jax.ShapeDtypeStruct((n_slots, width), BF16),
        compiler_params=_params("arbitrary"),
        name="moe_dispatch",
    )(plan["seg_start"], plan["seg_chunks"], plan["tile_off"], plan["tail_start"], plan["tail_chunks"],
      plan["n_used"], h2, wcols, sel, plan["off_lanes"])


def _moe_kernel(tile_expert_ref, n_used_ref, xs_ref, w1_ref, w3_ref, w2_ref, ys_ref, acc_ref):
    i = pl.program_id(0)
    f = pl.program_id(1)

    @pl.when(i < n_used_ref[0])
    def _():
        @pl.when(f == 0)
        def _():
            acc_ref[...] = jnp.zeros(acc_ref.shape, F32)

        d = w1_ref.shape[1]
        x = xs_ref[:, :d]
        wc = xs_ref[:, d:].astype(F32)
        lane = lax.broadcasted_iota(jnp.int32, wc.shape, 1)
        mine = jnp.logical_and((lane & (N_EXPERTS - 1)) == tile_expert_ref[i], lane < 3 * N_EXPERTS)
        wt = jnp.sum(jnp.where(mine, wc, 0.0), axis=1, keepdims=True)
        a = _silu(jnp.dot(x, w1_ref[0], preferred_element_type=F32)) * jnp.dot(x, w3_ref[0], preferred_element_type=F32)
        acc_ref[...] += jnp.dot((a * wt).astype(BF16), w2_ref[0], preferred_element_type=F32)

        @pl.when(f == pl.num_programs(1) - 1)
        def _():
            ys_ref[...] = acc_ref[...].astype(BF16)

    @pl.when(jnp.logical_and(i >= n_used_ref[0], f == 0))
    def _():
        ys_ref[...] = jnp.zeros(ys_ref.shape, BF16)


def moe_experts(plan, xs, w1, w3, w2):
    n_slots, width = xs.shape
    ne, d, dff = w1.shape
    tm, tf = MOE_ROW_TILE, MOE_FF_TILE
    n_f = dff // tf

    def row(i, f, te, nu):
        return jnp.minimum(i, nu[0] - 1)

    def ff(i, f, te, nu):
        return jnp.where(i < nu[0], f, n_f - 1)

    grid_spec = pltpu.PrefetchScalarGridSpec(
        num_scalar_prefetch=2,
        grid=(n_slots // tm, n_f),
        in_specs=[
            pl.BlockSpec((tm, width), lambda i, f, te, nu: (row(i, f, te, nu), 0)),
            pl.BlockSpec((1, d, tf), lambda i, f, te, nu: (te[row(i, f, te, nu)], 0, ff(i, f, te, nu))),
            pl.BlockSpec((1, d, tf), lambda i, f, te, nu: (te[row(i, f, te, nu)], 0, ff(i, f, te, nu))),
            pl.BlockSpec((1, tf, d), lambda i, f, te, nu: (te[row(i, f, te, nu)], ff(i, f, te, nu), 0)),
        ],
        out_specs=pl.BlockSpec((tm, d), lambda i, f, te, nu: (i, 0)),
        scratch_shapes=[pltpu.VMEM((tm, d), F32)],
    )
    return pl.pallas_call(
        _moe_kernel,
        grid_spec=grid_spec,
        out_shape=jax.ShapeDtypeStruct((n_slots, d), BF16),
        compiler_params=_params("arbitrary", "arbitrary"),
        name="moe_experts",
    )(plan["tile_expert"], plan["n_used"], xs, w1, w3, w2)


def _combine_kernel(seg_start_ref, seg_chunks_ref, tile_off_ref,
                    ys_ref, sel_ref, off_ref, x_ref, mod_ref, o_ref, ybuf_ref, sem):
    t = pl.program_id(0)
    buf = t % 2
    max_chunks = MOE_TOKEN_TILE // SLOT_ALIGN

    def tile_copies(tile, which, action):
        for e in range(N_EXPERTS):
            idx = tile * N_EXPERTS + e
            _segment_copies(seg_chunks_ref[idx], seg_start_ref[idx], tile_off_ref[idx],
                            ys_ref, ybuf_ref.at[which], sem.at[which], max_chunks, action)

    @pl.when(t == 0)
    def _():
        ybuf_ref[...] = jnp.zeros(ybuf_ref.shape, BF16)
        tile_copies(t, buf, "start")

    @pl.when(t + 1 < pl.num_programs(0))
    def _():
        tile_copies(t + 1, 1 - buf, "start")

    one_hot = _slot_one_hot(sel_ref, off_ref)
    tile_copies(t, buf, "wait")
    y = jnp.dot(one_hot, ybuf_ref[buf], preferred_element_type=F32)
    o_ref[...] = x_ref[...] + mod_ref[0, 5:6, :] * y


def moe_combine(plan, ys, sel, x1, mods, rows_per_seq):
    rows, d = x1.shape
    tt = MOE_TOKEN_TILE
    tile = lambda t, *_: (t, 0)
    grid_spec = pltpu.PrefetchScalarGridSpec(
        num_scalar_prefetch=3,
        grid=(rows // tt,),
        in_specs=[
            pl.BlockSpec(memory_space=pl.ANY),
            pl.BlockSpec((tt, LANES), tile),
            pl.BlockSpec((1, 1, LANES), lambda t, *_: (t, 0, 0)),
            pl.BlockSpec((tt, d), tile),
            pl.BlockSpec((1, 6, d), lambda t, *_: (1 + (t * tt) // rows_per_seq, 0, 0)),
        ],
        out_specs=pl.BlockSpec((tt, d), tile),
        scratch_shapes=[pltpu.VMEM((2, STAGE_ROWS, d), BF16), pltpu.SemaphoreType.DMA((2,))],
    )
    return pl.pallas_call(
        _combine_kernel,
        grid_spec=grid_spec,
        out_shape=jax.ShapeDtypeStruct((rows, d), F32),
        compiler_params=_params("arbitrary"),
        name="moe_combine",
    )(plan["seg_start"], plan["seg_chunks"], plan["tile_off"], ys, sel, plan["off_lanes"], x1, mods)


def _moe_plan(counts, n_row_tiles):
    counts = counts.astype(jnp.int32)
    seg = (counts + SLOT_ALIGN - 1) // SLOT_ALIGN * SLOT_ALIGN
    tile_off = jnp.cumsum(seg, axis=1) - seg
    total = jnp.sum(seg, axis=0)
    region = (total + MOE_ROW_TILE - 1) // MOE_ROW_TILE * MOE_ROW_TILE
    region_end = jnp.cumsum(region)
    region_start = region_end - region
    seg_start = region_start[None, :] + jnp.cumsum(seg, axis=0) - seg
    first_row = jnp.arange(n_row_tiles, dtype=jnp.int32) * MOE_ROW_TILE
    tile_expert = jnp.minimum(jnp.sum(first_row[:, None] >= region_end[None, :], axis=1), counts.shape[1] - 1)
    off_lanes = jnp.pad(tile_off.astype(F32), ((0, 0), (0, LANES - counts.shape[1])))
    return {
        "seg_start": seg_start.reshape(-1),
        "seg_chunks": (seg // SLOT_ALIGN).reshape(-1),
        "tile_off": tile_off.reshape(-1),
        "tail_start": region_start + total,
        "tail_chunks": (region - total) // SLOT_ALIGN,
        "tile_expert": tile_expert.astype(jnp.int32),
        "n_used": (region_end[-1:] // MOE_ROW_TILE).astype(jnp.int32),
        "off_lanes": off_lanes.reshape(counts.shape[0], 1, LANES),
    }


def _rope_tables(ctx_len, seq):
    rows = seq // GRID_W
    row = jnp.repeat(jnp.arange(rows, dtype=F32), GRID_W)
    col = jnp.tile(jnp.arange(GRID_W, dtype=F32), rows)
    inv = ROPE_BASE ** (-jnp.arange(ROPE_PAIRS_PER_AXIS, dtype=F32) / ROPE_PAIRS_PER_AXIS)
    ang = jnp.concatenate([row[:, None] * inv, col[:, None] * inv], axis=-1)
    cos, sin = jnp.cos(ang), jnp.sin(ang)
    cos_t = jnp.concatenate([cos, cos, cos, cos], axis=-1)
    sin_t = jnp.concatenate([-sin, sin, -sin, sin], axis=-1)
    cos_t = jnp.concatenate([jnp.ones((ctx_len, LANES), F32), cos_t], axis=0)
    sin_t = jnp.concatenate([jnp.zeros((ctx_len, LANES), F32), sin_t], axis=0)
    return cos_t, sin_t


def kernel(x, c, ctx, c_ctx, ada_w, ada_b, norm1_g, norm2_g, w_in, w_out, q_norm_g, k_norm_g, lambda_q1, lambda_k1, lambda_q2, lambda_k2, subln_g, conv_w, conv_b, gate_b, mnorm_g, ffn_w1, ffn_w3, ffn_w2, router_w, moe_w1, moe_w3, moe_w2):
    batch, seq, d = x.shape
    ctx_len = ctx.shape[1]
    depth = ada_w.shape[0]
    assert depth == 2, "layer 0 runs the dense FFN on all rows, layer 1 the experts on the latents"
    n_ctx_rows = batch * ctx_len
    n_ctx_tiles = n_ctx_rows // ROW_TILE
    tiles_per_seq = seq // ROW_TILE
    n_main = w_in.shape[2] - N_GATES
    assert n_main == 7 * A_WIDTH
    m_dh = (n_main - 3 * A_WIDTH) // 4 // M_HEADS
    gate_order = jnp.array(GATE_ORDER, jnp.int32)
    lane_pad = ((0, 0), (0, LANES - N_GATES))

    x_ctx = ctx.reshape(n_ctx_rows, d)
    x_lat = x.reshape(batch * seq, d)
    lat_first_tile = 0

    cc = jnp.zeros((2 * SUBLANES, d), F32).at[:batch].set(c).at[batch].set(c_ctx)
    mod_rows = adaln_all(cc, ada_w, ada_b)
    mods_all = jnp.concatenate([mod_rows[:, batch:batch + 1], mod_rows[:, :batch]], axis=1)
    mods_all = mods_all.reshape(depth, batch + 1, 6, d)

    cos_t, sin_t = _rope_tables(ctx_len, seq)

    for l in range(depth):
        last = l == depth - 1
        lam_init = 0.8 - 0.6 * math.exp(-0.3 * l)
        lam = (jnp.exp(jnp.sum(lambda_q1[l] * lambda_k1[l]).astype(F32))
               - jnp.exp(jnp.sum(lambda_q2[l] * lambda_k2[l]).astype(F32)) + lam_init)
        mods = mods_all[l]
        g_hi, g_lo = _split_bf16(w_in[l][:, n_main:][:, gate_order])
        g_hi, g_lo = jnp.pad(g_hi, lane_pad), jnp.pad(g_lo, lane_pad)
        w_all = jnp.concatenate([w_in[l, :, :n_main].astype(BF16), g_hi, g_lo], axis=1)
        qg = jnp.tile(q_norm_g[l] * (A_DH ** -0.5), 2).reshape(1, LANES)
        kg = jnp.tile(k_norm_g[l], 2).reshape(1, LANES)

        qk, av, mq, mkt, mv, mo, g_row = proj_in(
            x_ctx, x_lat, lat_first_tile, mods, norm1_g[l].reshape(1, d), w_all, g_hi,
            conv_w[l], conv_b[l].reshape(1, -1), cos_t, sin_t, qg, kg, n_ctx_tiles, tiles_per_seq, m_dh ** -0.5)
        attn = functools.partial(attention, lam.reshape(1, 1), qk, av, subln_g[l].reshape(1, LANES),
                                 1.0 - lam_init, batch, ctx_len, seq)
        a_out = attn(latent_queries=True)
        m_ctx, m_out = mlstm(mq, mkt, mv, mo, g_row, gate_b[l].reshape(-1)[gate_order].reshape(N_GATES, 1),
                             mnorm_g[l].reshape(1, -1), batch, ctx_len, seq)
        w_out_b = w_out[l].astype(BF16)
        j = l // 2
        if not last:
            assert lat_first_tile == 0
            xs = mixer_ffn(a_out, attn(latent_queries=False), m_out, m_ctx, w_out_b, x_ctx, x_lat, mods,
                           norm2_g[l].reshape(1, d), ffn_w1[j].astype(BF16), ffn_w3[j].astype(BF16),
                           ffn_w2[j].astype(BF16), seq)
            x_ctx, x_lat, lat_first_tile = xs, xs, n_ctx_tiles
        else:
            x1, h2, sel, wcols, cnt = mixer_router(a_out, m_out, w_out_b, x_lat, lat_first_tile * ROW_TILE, mods,
                                                   norm2_g[l].reshape(1, d), router_w[j], seq)
            n_tok_tiles = x1.shape[0] // MOE_TOKEN_TILE
            counts = cnt[:, 0, :N_EXPERTS]
            worst = 2 * x1.shape[0] + n_tok_tiles * N_EXPERTS * (SLOT_ALIGN - 1) + N_EXPERTS * (MOE_ROW_TILE - SLOT_ALIGN)
            n_row_tiles = -(-worst // MOE_ROW_TILE)
            plan = _moe_plan(counts, n_row_tiles)
            slots = moe_dispatch(plan, h2, wcols, sel, n_row_tiles * MOE_ROW_TILE)
            ys = moe_experts(plan, slots, moe_w1[j].astype(BF16), moe_w3[j].astype(BF16), moe_w2[j].astype(BF16))
            out = moe_combine(plan, ys, sel, x1, mods, seq)
    return out.reshape(batch, seq, d)
```

```python
import functools
import math

import jax
import jax.numpy as jnp
from jax import lax
from jax.experimental import pallas as pl
from jax.experimental.pallas import tpu as pltpu

F32 = jnp.float32
BF16 = jnp.bfloat16
HIGHEST = lax.Precision.HIGHEST

GRID_W = 64
A_HEADS = 4
A_DH = 64
A_DV = 2 * A_DH
A_WIDTH = A_HEADS * A_DV
M_HEADS = 4
N_GATES = 4 * M_HEADS
CONV_K = 5
ROPE_BASE = 10000.0
ROPE_PAIRS_PER_AXIS = A_DH // 4
N_EXPERTS = 8
EPS = 1e-6

LANES = 128
SUBLANES = 8
ROW_TILE = 256
MLSTM_CHUNK = 128
VMEM_LIMIT = 56 * 1024 * 1024

NT_DIMS = (((1,), (1,)), ((), ()))
TN_DIMS = (((0,), (0,)), ((), ()))

GATE_ORDER = tuple(d * 2 * M_HEADS + k * M_HEADS + h for k in range(2) for d in range(2) for h in range(M_HEADS))


def _params(*sem):
    return pltpu.CompilerParams(dimension_semantics=sem, vmem_limit_bytes=VMEM_LIMIT)


def _silu(v):
    return v * jax.nn.sigmoid(v)


def _log_sigmoid(v):
    return jnp.minimum(v, 0.0) - jnp.log1p(jnp.exp(-jnp.abs(v)))


def _rms(v, axis=-1):
    return v * lax.rsqrt(jnp.mean(v * v, axis=axis, keepdims=True) + EPS)


def _split_bf16(w):
    hi = w.astype(BF16)
    return hi, (w - hi.astype(F32)).astype(BF16)


def _adaln_kernel(c_ref, w_ref, b_ref, o_ref):
    s = _silu(c_ref[...])
    o_ref[0] = jnp.dot(s, w_ref[0], precision=HIGHEST, preferred_element_type=F32) + b_ref[0]


def adaln_all(cc, ada_w, ada_b):
    depth, d, d6 = ada_w.shape
    n_col = d6 // d
    return pl.pallas_call(
        _adaln_kernel,
        grid=(depth, n_col),
        in_specs=[
            pl.BlockSpec(cc.shape, lambda l, j: (0, 0)),
            pl.BlockSpec((1, d, d), lambda l, j: (l, 0, j)),
            pl.BlockSpec((1, 1, d), lambda l, j: (l, 0, j)),
        ],
        out_specs=pl.BlockSpec((1, cc.shape[0], d), lambda l, j: (l, 0, j)),
        out_shape=jax.ShapeDtypeStruct((depth, cc.shape[0], d6), F32),
        compiler_params=_params("arbitrary", "arbitrary"),
        name="adaln",
    )(cc, ada_w, ada_b.reshape(depth, 1, d6))


def _proj_in_kernel(xa_ref, xb_ref, prev_ref, next_ref, mod_ref, g_ref, w_ref, wgh_ref, cw_ref, cb_ref,
                    cos_ref, sin_ref, qg_ref, kg_ref,
                    qk_ref, v_ref, mq_ref, mkt_ref, mv_ref, mo_ref, gr_ref, ext_ref,
                    *, n_ctx_tiles, tiles_per_seq, k_scale):
    i = pl.program_id(0)
    is_ctx = i < n_ctx_tiles
    pos = (i - n_ctx_tiles) % tiles_per_seq
    first = jnp.logical_or(is_ctx, pos == 0)
    last = jnp.logical_or(is_ctx, pos == tiles_per_seq - 1)
    tm = xa_ref.shape[0]
    halo = SUBLANES

    x_ext = jnp.concatenate([prev_ref[...], jnp.where(is_ctx, xa_ref[...], xb_ref[...]), next_ref[...]], axis=0)
    h_ext = _rms(x_ext) * g_ref[...] * (1.0 + mod_ref[0, 1:2, :]) + mod_ref[0, 0:1, :]
    h = h_ext[halo:halo + tm]
    hb = h.astype(BF16)

    def proj(j):
        return jnp.dot(hb, w_ref[:, j * A_WIDTH:(j + 1) * A_WIDTH], preferred_element_type=F32)

    pre = jnp.dot(h_ext.astype(BF16), w_ref[:, 3 * A_WIDTH:5 * A_WIDTH], preferred_element_type=F32)
    ext_ref[0:halo, :] = jnp.where(first, 0.0, pre[0:halo])
    ext_ref[halo:halo + tm, :] = pre[halo:halo + tm]
    ext_ref[halo + tm:, :] = jnp.where(last, 0.0, pre[halo + tm:])

    r = lax.broadcasted_iota(jnp.int32, (LANES, LANES), 0)
    c = lax.broadcasted_iota(jnp.int32, (LANES, LANES), 1)
    group_ones = jnp.where((r & -A_DH) == (c & -A_DH), 1.0, 0.0).astype(BF16)
    lane = lax.broadcasted_iota(jnp.int32, (1, LANES), 1)
    first_half = (lane & (A_DH - 1)) < (A_DH // 2)
    cos = cos_ref[...]
    sin = sin_ref[...]
    for j, gain_ref in ((0, qg_ref), (1, kg_ref)):
        acc = proj(j)
        for hd in range(A_HEADS):
            t = acc[:, hd * LANES:(hd + 1) * LANES]
            ss = jnp.dot((t * t).astype(BF16), group_ones, preferred_element_type=F32)
            tn = t * lax.rsqrt(ss * (1.0 / A_DH) + EPS) * gain_ref[...]
            rot = jnp.where(first_half, pltpu.roll(tn, LANES - A_DH // 2, 1), pltpu.roll(tn, A_DH // 2, 1))
            qk_ref[:, j * A_WIDTH + hd * LANES: j * A_WIDTH + (hd + 1) * LANES] = (tn * cos + rot * sin).astype(BF16)
    def conv_silu(c0, c1):
        acc = jnp.zeros((tm, c1 - c0), F32) + cb_ref[:, c0:c1]
        pad = CONV_K // 2
        for j in range(CONV_K):
            acc = acc + cw_ref[j:j + 1, c0:c1] * ext_ref[halo - pad + j:halo - pad + j + tm, c0:c1]
        return _silu(acc)

    def store_k(c0, c1):
        yk = conv_silu(A_WIDTH + c0, A_WIDTH + c1) * k_scale
        for ci in range(tm // MLSTM_CHUNK):
            mkt_ref[ci, c0:c1, :] = yk[ci * MLSTM_CHUNK:(ci + 1) * MLSTM_CHUNK].T.astype(BF16)

    half = A_WIDTH // 2
    v_ref[...] = proj(2).astype(BF16)
    mq_ref[:, :half] = conv_silu(0, half).astype(BF16)
    mv_ref[...] = proj(5).astype(BF16)
    mq_ref[:, half:] = conv_silu(half, A_WIDTH).astype(BF16)
    mo_ref[...] = proj(6)
    store_k(0, half)

    n_main = 7 * A_WIDTH
    p = jnp.dot(hb, w_ref[:, n_main:n_main + 2 * LANES], preferred_element_type=F32)
    h_lo = (h - hb.astype(F32)).astype(BF16)
    gates = p[:, :LANES] + p[:, LANES:] + jnp.dot(h_lo, wgh_ref[...], preferred_element_type=F32)
    gr_ref[...] = gates.T[:N_GATES, :]
    store_k(half, A_WIDTH)


def proj_in(xa, xb, b_first_tile, mods, norm_g, w_all, w_gate_hi, conv_w, conv_b, cos_t, sin_t, qg, kg,
            n_ctx_tiles, tiles_per_seq, k_scale):
    d = xa.shape[1]
    n_lat_tiles = xb.shape[0] // ROW_TILE - b_first_tile
    n_tiles = n_ctx_tiles + n_lat_tiles
    rows = n_tiles * ROW_TILE
    per = ROW_TILE // SUBLANES

    def seg(i):
        return jnp.where(i < n_ctx_tiles, 0, 1 + (i - n_ctx_tiles) // tiles_per_seq)

    def rope_blk(i):
        return jnp.where(i < n_ctx_tiles, 0, 1 + (i - n_ctx_tiles) % tiles_per_seq)

    def lat(i):
        return jnp.maximum(i - n_ctx_tiles, 0)

    row = lambda i: (i, 0)
    const = lambda i: (0, 0)
    kern = functools.partial(_proj_in_kernel, n_ctx_tiles=n_ctx_tiles, tiles_per_seq=tiles_per_seq, k_scale=k_scale)
    return pl.pallas_call(
        kern,
        grid=(n_tiles,),
        in_specs=[
            pl.BlockSpec((ROW_TILE, d), lambda i: (jnp.minimum(i, n_ctx_tiles - 1), 0)),
            pl.BlockSpec((ROW_TILE, d), lambda i: (lat(i) + b_first_tile, 0)),
            pl.BlockSpec((SUBLANES, d), lambda i: (jnp.maximum(lat(i) * per - 1, 0) + b_first_tile * per, 0)),
            pl.BlockSpec((SUBLANES, d),
                         lambda i: (jnp.minimum((lat(i) + 1) * per, n_lat_tiles * per - 1) + b_first_tile * per, 0)),
            pl.BlockSpec((1, 6, d), lambda i: (seg(i), 0, 0)),
            pl.BlockSpec((1, d), const),
            pl.BlockSpec(w_all.shape, const),
            pl.BlockSpec(w_gate_hi.shape, const),
            pl.BlockSpec(conv_w.shape, const),
            pl.BlockSpec(conv_b.shape, const),
            pl.BlockSpec((ROW_TILE, LANES), lambda i: (rope_blk(i), 0)),
            pl.BlockSpec((ROW_TILE, LANES), lambda i: (rope_blk(i), 0)),
            pl.BlockSpec((1, LANES), const),
            pl.BlockSpec((1, LANES), const),
        ],
        out_specs=[
            pl.BlockSpec((ROW_TILE, 2 * A_WIDTH), row),
            pl.BlockSpec((ROW_TILE, A_WIDTH), row),
            pl.BlockSpec((ROW_TILE, A_WIDTH), row),
            pl.BlockSpec((ROW_TILE // MLSTM_CHUNK, A_WIDTH, MLSTM_CHUNK), lambda i: (i, 0, 0)),
            pl.BlockSpec((ROW_TILE, A_WIDTH), row),
            pl.BlockSpec((ROW_TILE, A_WIDTH), row),
            pl.BlockSpec((N_GATES, ROW_TILE), lambda i: (0, i)),
        ],
        out_shape=[
            jax.ShapeDtypeStruct((rows, 2 * A_WIDTH), BF16),
            jax.ShapeDtypeStruct((rows, A_WIDTH), BF16),
            jax.ShapeDtypeStruct((rows, A_WIDTH), BF16),
            jax.ShapeDtypeStruct((rows // MLSTM_CHUNK, A_WIDTH, MLSTM_CHUNK), BF16),
            jax.ShapeDtypeStruct((rows, A_WIDTH), BF16),
            jax.ShapeDtypeStruct((rows, A_WIDTH), F32),
            jax.ShapeDtypeStruct((N_GATES, rows), F32),
        ],
        scratch_shapes=[pltpu.VMEM((ROW_TILE + 2 * SUBLANES, 2 * A_WIDTH), F32)],
        compiler_params=_params("arbitrary"),
        name="proj_in",
    )(xa, xb, xb, xb, mods, norm_g, w_all, w_gate_hi, conv_w, conv_b, cos_t, sin_t, qg, kg)


ATTN_KEY_BLOCK = 256


ATTN_SUB_TILE = 256
ATTN_Q_TILE = 1024


def _attn_kernel(lam_ref, q_ref, *refs, with_latent_keys, post_scale):
    if with_latent_keys:
        kc_ref, vc_ref, kl_ref, vl_ref, g_ref, o_ref, s_ref = refs
    else:
        kc_ref, vc_ref, g_ref, o_ref, s_ref = refs
    lam = lam_ref[0, 0]
    lane = lax.broadcasted_iota(jnp.int32, (1, LANES), 1)
    kb = ATTN_KEY_BLOCK
    ts = s_ref.shape[1]
    blocks = [(kc_ref, vc_ref, r0) for r0 in range(0, kc_ref.shape[0], kb)]
    if with_latent_keys:
        blocks += [(kl_ref, vl_ref, r0) for r0 in range(0, kl_ref.shape[0], kb)]

    units = [(u, m) for u in range(q_ref.shape[0] // ts) for m in range(2)]
    state = {}

    def pass1(n, j):
        u, m = units[n]
        if j == 0:
            q = q_ref[u * ts:(u + 1) * ts, :]
            keep = (lane < A_DH) if m == 0 else (lane >= A_DH)
            state[n] = {"q": jnp.where(keep, q, jnp.zeros_like(q)), "max": jnp.full((ts, LANES), -jnp.inf, F32)}
        k_ref, _, r0 = blocks[j]
        s = lax.dot_general(state[n]["q"], k_ref[r0:r0 + kb, :], NT_DIMS, preferred_element_type=F32)
        s_ref[n % 2, :, j * kb:(j + 1) * kb] = s
        for c0 in range(0, kb, LANES):
            state[n]["max"] = jnp.maximum(state[n]["max"], s[:, c0:c0 + LANES])

    ones = jnp.ones((kb, LANES), BF16)

    def pass2(n, j):
        st = state[n]
        if j == 0:
            st["mx"] = jnp.broadcast_to(jnp.max(st["max"], axis=1, keepdims=True), (ts, kb))
            st["acc"] = jnp.zeros((ts, 2 * LANES), F32)
        _, v_ref, r0 = blocks[j]
        e = jnp.exp2(s_ref[n % 2, :, j * kb:(j + 1) * kb] - st["mx"])
        v_ext = jnp.concatenate([v_ref[r0:r0 + kb, :], ones], axis=1)
        st["acc"] = st["acc"] + jnp.dot(e.astype(BF16), v_ext, preferred_element_type=F32)

    def finish(u):
        first, second = state.pop(2 * u)["acc"], state.pop(2 * u + 1)["acc"]
        out = first[:, :LANES] / first[:, LANES:] - second[:, :LANES] * (lam / second[:, LANES:])
        o_ref[u * ts:(u + 1) * ts, :] = (_rms(out) * g_ref[...] * post_scale).astype(BF16)

    for n in range(len(units) + 1):
        for j in range(len(blocks)):
            if n < len(units):
                pass1(n, j)
            if n >= 1:
                pass2(n - 1, j)
        if n >= 1 and units[n - 1][1] == 1:
            finish(units[n - 1][0])


def attention(lam, qk, v, subln_g, post_scale, batch, ctx_len, seq, latent_queries):
    lat_blk0 = batch * ctx_len // seq
    tq = ATTN_Q_TILE if latent_queries else ctx_len
    nq = seq // tq if latent_queries else 1
    q0 = batch * ctx_len // tq if latent_queries else 0
    q_per_batch = nq
    n_keys = ctx_len + (seq if latent_queries else 0)
    in_specs = [
        pl.BlockSpec(memory_space=pltpu.SMEM),
        pl.BlockSpec((tq, LANES), lambda b, h, t: (q0 + b * q_per_batch + t, h)),
        pl.BlockSpec((ctx_len, LANES), lambda b, h, t: (b, A_HEADS + h)),
        pl.BlockSpec((ctx_len, LANES), lambda b, h, t: (b, h)),
    ]
    args = [lam, qk, qk, v]
    if latent_queries:
        in_specs += [pl.BlockSpec((seq, LANES), lambda b, h, t: (lat_blk0 + b, A_HEADS + h)),
                     pl.BlockSpec((seq, LANES), lambda b, h, t: (lat_blk0 + b, h))]
        args += [qk, v]
    in_specs.append(pl.BlockSpec((1, LANES), lambda b, h, t: (0, 0)))
    args.append(subln_g)
    kern = functools.partial(_attn_kernel, with_latent_keys=latent_queries, post_scale=post_scale)
    return pl.pallas_call(
        kern,
        grid=(batch, A_HEADS, nq),
        in_specs=in_specs,
        out_specs=pl.BlockSpec((tq, LANES), lambda b, h, t: (b * q_per_batch + t, h)),
        out_shape=jax.ShapeDtypeStruct((batch * nq * tq, A_WIDTH), BF16),
        scratch_shapes=[pltpu.VMEM((2, min(tq, ATTN_SUB_TILE), n_keys), F32)],
        compiler_params=_params("arbitrary", "arbitrary", "arbitrary"),
        name="diff_attention",
    )(*args)


def _mlstm_kernel(qc_ref, ql_ref, ktc_ref, ktl_ref, vc_ref, vl_ref, oc_ref, ol_ref, grc_ref, grl_ref, gb_ref, mg_ref,
                  outc_ref, outl_ref, row_ref, col_ref, s_ref, hf_ref, hb_ref):
    L = MLSTM_CHUNK
    dh = LANES
    n_ch = 2 * M_HEADS
    n_ctx = qc_ref.shape[0] // L
    n_lat = ql_ref.shape[0] // L
    n_chunks = n_ctx + n_lat
    nt = n_chunks * L

    g = jnp.concatenate([grc_ref[...], grl_ref[...]], axis=1) + gb_ref[...]
    log_i = g[:n_ch]
    log_f = _log_sigmoid(g[n_ch:])
    sub = lax.broadcasted_iota(jnp.int32, (n_ch, nt), 0)
    lane_in_chunk = lax.broadcasted_iota(jnp.int32, (n_ch, nt), 1) & (L - 1)
    fwd_rows = sub < M_HEADS

    def chunk_scan(x, op, ident):
        xf, xb, k = x, x, 1
        while k < L:
            xf = op(xf, jnp.where(lane_in_chunk >= k, pltpu.roll(xf, k, 1), ident))
            xb = op(xb, jnp.where(lane_in_chunk < L - k, pltpu.roll(xb, nt - k, 1), ident))
            k *= 2
        return jnp.where(fwd_rows, xf, xb)

    b = chunk_scan(log_f, jnp.add, 0.0)
    a = log_i - b
    a_run = chunk_scan(a, jnp.maximum, -jnp.inf)

    fwd_col = lax.broadcasted_iota(jnp.int32, (n_ch, 1), 0) < M_HEADS

    def at_chunk_end(x, ci):
        xc = x[:, ci * L:(ci + 1) * L]
        return jnp.where(fwd_col, xc[:, L - 1:L], xc[:, 0:1])

    b_last = [at_chunk_end(b, ci) for ci in range(n_chunks)]
    a_last = [at_chunk_end(a_run, ci) for ci in range(n_chunks)]
    order_f = list(range(n_chunks))
    order_b = list(range(n_ctx - 1, -1, -1)) + list(range(n_chunks - 1, n_ctx - 1, -1))

    def carried(order):
        m, out = jnp.zeros((n_ch, 1), F32), {}
        for ci in order:
            out[ci] = m
            m = b_last[ci] + jnp.maximum(m, a_last[ci])
        return out

    m_f, m_b = carried(order_f), carried(order_b)
    for ci in range(n_chunks):
        m_prev = jnp.where(fwd_col, m_f[ci], m_b[ci])
        sl = slice(ci * L, (ci + 1) * L)
        mu = jnp.maximum(m_prev, a_run[:, sl])
        mu_last = jnp.maximum(m_prev, a_last[ci])
        row_ref[ci] = jnp.concatenate([a[:, sl], jnp.exp(a[:, sl] - mu_last), jnp.broadcast_to(m_prev, (n_ch, L)),
                                       jnp.broadcast_to(jnp.exp(m_prev - mu_last), (n_ch, L))], axis=0)
        tile = jnp.concatenate([mu, b[:, sl] + mu, jnp.zeros((L - 2 * n_ch, L), F32)], axis=0)
        col_ref[ci] = tile.T

    s_ref[...] = jnp.zeros(s_ref.shape, F32)
    rr = lax.broadcasted_iota(jnp.int32, (L, L), 0)
    cc = lax.broadcasted_iota(jnp.int32, (L, L), 1)
    visible = (cc <= rr, cc >= rr)
    ones = jnp.ones((L, dh), BF16)

    def chunk_rows(ci):
        return pl.ds(ci * L if isinstance(ci, int) else pl.multiple_of(ci * L, L), L)

    def twice(t):
        return jnp.concatenate([t, t], axis=1)

    def step(local, glob, q_ref, kt_ref, v_ref):
        for d in range(2):
            rows = chunk_rows(local[d])
            grow = chunk_rows(glob[d])
            cs = col_ref[glob[d]]
            rs = row_ref[glob[d]]
            h_ref = hf_ref if d == 0 else hb_ref
            for hd in range(M_HEADS):
                r = d * M_HEADS + hd
                mu = jnp.broadcast_to(cs[:, r:r + 1], (L, L))
                m_t = jnp.broadcast_to(cs[:, n_ch + r:n_ch + r + 1], (L, dh))
                a_row, w_row, m_prev, decay = (rs[j * n_ch + r:j * n_ch + r + 1, :] for j in range(4))
                w_intra = jnp.exp(jnp.where(visible[d], a_row - mu, -jnp.inf))
                w_inter = jnp.exp(m_prev - mu)
                cols = slice(hd * dh, (hd + 1) * dh)
                q = q_ref[rows, cols]
                kt = kt_ref[local[d], cols, :]
                v_ext = jnp.concatenate([v_ref[rows, cols], ones], axis=1)
                state = s_ref[r]
                qa = jnp.dot(q, jnp.concatenate([kt, state.astype(BF16)], axis=1), preferred_element_type=F32)
                s = (qa[:, :L] * w_intra).astype(BF16)
                ktw = (kt.astype(F32) * w_row).astype(BF16)
                sv = jnp.dot(jnp.concatenate([s, ktw], axis=0), v_ext, preferred_element_type=F32)
                t = sv[:L] + twice(w_inter) * qa[:, L:]
                h_ref[grow, cols] = t[:, :dh] / jnp.maximum(jnp.abs(t[:, dh:]), jnp.exp(-m_t))
                s_ref[r] = twice(jnp.broadcast_to(decay, (dh, L))) * state + sv[L:]

    for i in range(n_ctx):
        step((i, n_ctx - 1 - i), (i, n_ctx - 1 - i), qc_ref, ktc_ref, vc_ref)

    def lat_step(i, carry):
        step((i, n_lat - 1 - i), (n_ctx + i, n_chunks - 1 - i), ql_ref, ktl_ref, vl_ref)
        return carry

    lax.fori_loop(0, n_lat, lat_step, 0)

    def finish(local, glob, o_ref, out_ref):
        rows = chunk_rows(local)
        grow = chunk_rows(glob)
        for hd in range(M_HEADS):
            cols = slice(hd * dh, (hd + 1) * dh)
            hn = _rms(hf_ref[grow, cols] + hb_ref[grow, cols]) * mg_ref[:, cols]
            out_ref[rows, cols] = (jax.nn.sigmoid(o_ref[rows, cols]) * hn).astype(BF16)

    for i in range(n_ctx):
        finish(i, i, oc_ref, outc_ref)

    def lat_finish(i, carry):
        finish(i, n_ctx + i, ol_ref, outl_ref)
        return carry

    lax.fori_loop(0, n_lat, lat_finish, 0)


def mlstm(mq, mkt, mv, mo, g_row, gate_b, mnorm_g, batch, ctx_len, seq):
    rows, width = mq.shape
    assert MLSTM_CHUNK == LANES and width == M_HEADS * LANES
    lat0 = batch * ctx_len // seq
    n_chunks = (ctx_len + seq) // MLSTM_CHUNK
    ctx = lambda b: (b, 0)
    lat = lambda b: (lat0 + b, 0)
    const = lambda b: (0, 0)
    out_c, out_l = pl.pallas_call(
        _mlstm_kernel,
        grid=(batch,),
        in_specs=[
            pl.BlockSpec((ctx_len, width), ctx), pl.BlockSpec((seq, width), lat),
            pl.BlockSpec((ctx_len // MLSTM_CHUNK, width, MLSTM_CHUNK), lambda b: (b, 0, 0)),
            pl.BlockSpec((seq // MLSTM_CHUNK, width, MLSTM_CHUNK), lambda b: (lat0 + b, 0, 0)),
            pl.BlockSpec((ctx_len, width), ctx), pl.BlockSpec((seq, width), lat),
            pl.BlockSpec((ctx_len, width), ctx), pl.BlockSpec((seq, width), lat),
            pl.BlockSpec((N_GATES, ctx_len), lambda b: (0, b)), pl.BlockSpec((N_GATES, seq), lambda b: (0, lat0 + b)),
            pl.BlockSpec((N_GATES, 1), const),
            pl.BlockSpec((1, width), const),
        ],
        out_specs=[pl.BlockSpec((ctx_len, width), ctx), pl.BlockSpec((seq, width), ctx)],
        out_shape=[jax.ShapeDtypeStruct((batch * ctx_len, width), BF16),
                   jax.ShapeDtypeStruct((batch * seq, width), BF16)],
        scratch_shapes=[
            pltpu.VMEM((n_chunks, 8 * M_HEADS, MLSTM_CHUNK), F32),
            pltpu.VMEM((n_chunks, MLSTM_CHUNK, MLSTM_CHUNK), F32),
            pltpu.VMEM((2 * M_HEADS, LANES, 2 * LANES), F32),
            pltpu.VMEM((ctx_len + seq, width), F32),
            pltpu.VMEM((ctx_len + seq, width), F32),
        ],
        compiler_params=_params("arbitrary"),
        name="mlstm",
    )(mq, mq, mkt, mkt, mv, mv, mo, mo, g_row, g_row, gate_b, mnorm_g)
    return out_c, out_l


def _router_top2(logits):
    lane = lax.broadcasted_iota(jnp.int32, logits.shape, 1)
    logits = jnp.where(lane < N_EXPERTS, logits, -jnp.inf)
    v1 = jnp.max(logits, axis=1, keepdims=True)
    i1 = jnp.min(jnp.where(logits == v1, lane, LANES), axis=1, keepdims=True)
    rest = jnp.where(lane == i1, -jnp.inf, logits)
    v2 = jnp.max(rest, axis=1, keepdims=True)
    i2 = jnp.min(jnp.where(rest == v2, lane, LANES), axis=1, keepdims=True)
    e = jnp.exp(v2 - v1)
    w1 = 1.0 / (1.0 + e)
    w2 = e / (1.0 + e)
    mask = jnp.where(jnp.logical_or(lane == i1, lane == i2), 1.0, 0.0)
    expert = lane & (N_EXPERTS - 1)
    comb = jnp.where(lane < 3 * N_EXPERTS,
                     jnp.where(expert == i1, w1, 0.0) + jnp.where(expert == i2, w2, 0.0), 0.0)
    hi = comb.astype(BF16)
    rem = comb - hi.astype(F32)
    mid = rem.astype(BF16)
    lo = (rem - mid.astype(F32)).astype(BF16)
    wcols = jnp.where(lane < N_EXPERTS, hi, jnp.where(lane < 2 * N_EXPERTS, mid, lo))
    return mask, wcols


def _mixer_router_kernel(a_ref, m_ref, w_ref, x_ref, mod_ref, g_ref, rw_ref, x1_ref, h2_ref, sel_ref, wc_ref, cnt_ref):
    half = a_ref.shape[1]
    mix = (jnp.dot(a_ref[...], w_ref[:half, :], preferred_element_type=F32)
           + jnp.dot(m_ref[...], w_ref[half:, :], preferred_element_type=F32))
    x1 = x_ref[...] + mod_ref[0, 2:3, :] * mix
    x1_ref[...] = x1
    h2 = _rms(x1) * g_ref[...] * (1.0 + mod_ref[0, 4:5, :]) + mod_ref[0, 3:4, :]
    hb = h2.astype(BF16)
    h2_ref[...] = hb
    p = jnp.dot(hb, rw_ref[...], preferred_element_type=F32)
    h_lo = (h2 - hb.astype(F32)).astype(BF16)
    logits = p[:, :LANES] + p[:, LANES:] + jnp.dot(h_lo, rw_ref[:, :LANES], preferred_element_type=F32)
    mask, wcols = _router_top2(logits)
    sel_ref[...] = mask.astype(BF16)
    wc_ref[...] = wcols
    cnt_ref[0] = jnp.sum(mask, axis=0, keepdims=True)


def mixer_router(a_lat, m_lat, w_out, xs, first_row, mods, norm_g, router_w, rows_per_seq):
    d = xs.shape[1]
    half = a_lat.shape[1]
    tm = MOE_TOKEN_TILE
    n_tiles = a_lat.shape[0] // tm
    first_tile = first_row // tm
    dst = lambda i: (i, 0)
    const = lambda i: (0, 0)
    pad = ((0, 0), (0, LANES - router_w.shape[1]))
    rw = jnp.concatenate([jnp.pad(t, pad) for t in _split_bf16(router_w)], axis=1)
    return pl.pallas_call(
        _mixer_router_kernel,
        grid=(n_tiles,),
        in_specs=[
            pl.BlockSpec((tm, half), dst), pl.BlockSpec((tm, half), dst),
            pl.BlockSpec(w_out.shape, const),
            pl.BlockSpec((tm, d), lambda i: (i + first_tile, 0)),
            pl.BlockSpec((1, 6, d), lambda i: (1 + (i * tm) // rows_per_seq, 0, 0)),
            pl.BlockSpec((1, d), const),
            pl.BlockSpec(rw.shape, const),
        ],
        out_specs=[pl.BlockSpec((tm, d), dst), pl.BlockSpec((tm, d), dst),
                   pl.BlockSpec((tm, LANES), dst), pl.BlockSpec((tm, LANES), dst),
                   pl.BlockSpec((1, 1, LANES), lambda i: (i, 0, 0))],
        out_shape=[jax.ShapeDtypeStruct((n_tiles * tm, d), F32),
                   jax.ShapeDtypeStruct((n_tiles * tm, d), BF16),
                   jax.ShapeDtypeStruct((n_tiles * tm, LANES), BF16),
                   jax.ShapeDtypeStruct((n_tiles * tm, LANES), BF16),
                   jax.ShapeDtypeStruct((n_tiles, 1, LANES), F32)],
        compiler_params=_params("arbitrary"),
        name="mixer_router",
    )(a_lat, m_lat, w_out, xs, mods, norm_g, rw)


FFN_ROW_TILE = 512
FFN_FF_TILE = 1408


def _mixer_ffn_kernel(a_ref, ac_ref, m_ref, mc_ref, wo_ref, xc_ref, x_ref, mod_ref, g_ref, w1_ref, w3_ref, w2_ref,
                      o_ref, x1_ref, h2_ref, acc_ref, *, n_ctx_tiles):
    f = pl.program_id(1)

    @pl.when(f == 0)
    def _():
        is_ctx = pl.program_id(0) < n_ctx_tiles
        a = jnp.where(is_ctx, ac_ref[...], a_ref[...])
        m = jnp.where(is_ctx, mc_ref[...], m_ref[...])
        x = jnp.where(is_ctx, xc_ref[...], x_ref[...])
        half = a.shape[1]
        mix = (jnp.dot(a, wo_ref[:half, :], preferred_element_type=F32)
               + jnp.dot(m, wo_ref[half:, :], preferred_element_type=F32))
        x1 = x + mod_ref[0, 2:3, :] * mix
        x1_ref[...] = x1
        h2_ref[...] = (_rms(x1) * g_ref[...] * (1.0 + mod_ref[0, 4:5, :]) + mod_ref[0, 3:4, :]).astype(BF16)
        acc_ref[...] = jnp.zeros(acc_ref.shape, F32)

    h = h2_ref[...]
    a = _silu(jnp.dot(h, w1_ref[...], preferred_element_type=F32)) * jnp.dot(h, w3_ref[...], preferred_element_type=F32)
    acc_ref[...] += jnp.dot(a.astype(BF16), w2_ref[...], preferred_element_type=F32)

    @pl.when(f == pl.num_programs(1) - 1)
    def _():
        o_ref[...] = x1_ref[...] + mod_ref[0, 5:6, :] * acc_ref[...]


def mixer_ffn(a_lat, a_ctx, m_lat, m_ctx, w_out, x_ctx, x_lat, mods, norm_g, w1, w3, w2, rows_per_seq):
    d = x_lat.shape[1]
    half = a_lat.shape[1]
    dff = w1.shape[1]
    tm, tf = FFN_ROW_TILE, FFN_FF_TILE
    n_ctx_tiles = x_ctx.shape[0] // tm
    n_tiles = n_ctx_tiles + x_lat.shape[0] // tm
    per_seq = rows_per_seq // tm
    ctx = lambda i, f: (jnp.minimum(i, n_ctx_tiles - 1), 0)
    lat = lambda i, f: (jnp.maximum(i - n_ctx_tiles, 0), 0)
    const = lambda i, f: (0, 0)

    def seg(i, f):
        return (jnp.where(i < n_ctx_tiles, 0, 1 + (i - n_ctx_tiles) // per_seq), 0, 0)

    return pl.pallas_call(
        functools.partial(_mixer_ffn_kernel, n_ctx_tiles=n_ctx_tiles),
        grid=(n_tiles, dff // tf),
        in_specs=[
            pl.BlockSpec((tm, half), lat), pl.BlockSpec((tm, half), ctx),
            pl.BlockSpec((tm, half), lat), pl.BlockSpec((tm, half), ctx),
            pl.BlockSpec(w_out.shape, const),
            pl.BlockSpec((tm, d), ctx), pl.BlockSpec((tm, d), lat),
            pl.BlockSpec((1, 6, d), seg),
            pl.BlockSpec((1, d), const),
            pl.BlockSpec((d, tf), lambda i, f: (0, f)),
            pl.BlockSpec((d, tf), lambda i, f: (0, f)),
            pl.BlockSpec((tf, d), lambda i, f: (f, 0)),
        ],
        out_specs=pl.BlockSpec((tm, d), lambda i, f: (i, 0)),
        out_shape=jax.ShapeDtypeStruct((n_tiles * tm, d), F32),
        scratch_shapes=[pltpu.VMEM((tm, d), F32), pltpu.VMEM((tm, d), BF16), pltpu.VMEM((tm, d), F32)],
        compiler_params=_params("arbitrary", "arbitrary"),
        name="mixer_ffn",
    )(a_lat, a_ctx, m_lat, m_ctx, w_out, x_ctx, x_lat, mods, norm_g, w1, w3, w2)


MOE_TOKEN_TILE = 512
MOE_ROW_TILE = 512
MOE_FF_TILE = 1792
SLOT_ALIGN = 16
STAGE_ROWS = 2 * MOE_TOKEN_TILE + LANES


def _pow2_chunks(max_chunks):
    k = 1
    while k * 2 <= max_chunks:
        k *= 2
    while k >= 1:
        yield k
        k //= 2


def _segment_copies(n_chunks, src, dst, src_ref, dst_ref, sem, max_chunks, action):
    for k in _pow2_chunks(max_chunks):
        rows = k * SLOT_ALIGN
        taken = (n_chunks & ~(2 * k - 1)) * SLOT_ALIGN

        @pl.when((n_chunks & k) != 0)
        def _(rows=rows, taken=taken):
            s = pl.multiple_of(src + taken, SLOT_ALIGN)
            t = pl.multiple_of(dst + taken, SLOT_ALIGN)
            cp = pltpu.make_async_copy(src_ref.at[pl.ds(s, rows), :], dst_ref.at[pl.ds(t, rows), :], sem)
            if action == "start":
                cp.start()
            else:
                cp.wait()


def _slot_one_hot(sel_ref, off_ref):
    sel = sel_ref[...]
    tt = sel.shape[0]
    r = lax.broadcasted_iota(jnp.int32, (tt, tt), 0)
    c = lax.broadcasted_iota(jnp.int32, (tt, tt), 1)
    earlier = jnp.where(c < r, 1.0, 0.0).astype(BF16)
    rank = jnp.dot(earlier, sel, preferred_element_type=F32)
    pos = off_ref[0] + rank
    chosen = sel > 0
    pa = jnp.min(jnp.where(chosen, pos, float(STAGE_ROWS)), axis=1, keepdims=True)
    pb = jnp.max(jnp.where(chosen, pos, -1.0), axis=1, keepdims=True)
    slot = lax.broadcasted_iota(jnp.int32, (tt, STAGE_ROWS), 1).astype(F32)
    return jnp.where(jnp.logical_or(slot == pa, slot == pb), 1.0, 0.0).astype(BF16)


def _dispatch_kernel(seg_start_ref, seg_chunks_ref, tile_off_ref, tail_start_ref, tail_chunks_ref, n_used_ref,
                     h_ref, wc_ref, sel_ref, off_ref, xs_ref, stage_ref, zero_ref, sem):
    t = pl.program_id(0)
    last = pl.num_programs(0) - 1
    buf = t % 2
    max_chunks = MOE_TOKEN_TILE // SLOT_ALIGN

    def tile_copies(tile, which, action):
        for e in range(N_EXPERTS):
            idx = tile * N_EXPERTS + e
            _segment_copies(seg_chunks_ref[idx], tile_off_ref[idx], seg_start_ref[idx],
                            stage_ref.at[which], xs_ref, sem.at[which], max_chunks, action)

    one_hot = _slot_one_hot(sel_ref, off_ref)
    payload = jnp.concatenate([h_ref[...], wc_ref[...]], axis=1)
    stage_ref[buf] = lax.dot_general(one_hot, payload, TN_DIMS, preferred_element_type=F32).astype(BF16)
    tile_copies(t, buf, "start")

    @pl.when(t > 0)
    def _():
        tile_copies(t - 1, 1 - buf, "wait")

    @pl.when(t == last)
    def _():
        tile_copies(t, buf, "wait")
        zero_ref[...] = jnp.zeros(zero_ref.shape, BF16)
        for action in ("start", "wait"):
            for e in range(N_EXPERTS):
                _segment_copies(tail_chunks_ref[e], 0, tail_start_ref[e], zero_ref, xs_ref, sem.at[0],
                                MOE_ROW_TILE // SLOT_ALIGN - 1, action)

        def clear_tile(j, carry):
            rows = pl.ds(pl.multiple_of(j * MOE_ROW_TILE, MOE_ROW_TILE), MOE_ROW_TILE)
            cp = pltpu.make_async_copy(zero_ref, xs_ref.at[rows, :], sem.at[0])
            cp.start()
            cp.wait()
            return carry

        lax.fori_loop(n_used_ref[0], xs_ref.shape[0] // MOE_ROW_TILE, clear_tile, 0)


def moe_dispatch(plan, h2, wcols, sel, n_slots):
    rows, d = h2.shape
    tt = MOE_TOKEN_TILE
    width = d + LANES
    tile = lambda t, *_: (t, 0)
    grid_spec = pltpu.PrefetchScalarGridSpec(
        num_scalar_prefetch=6,
        grid=(rows // tt,),
        in_specs=[
            pl.BlockSpec((tt, d), tile),
            pl.BlockSpec((tt, LANES), tile),
            pl.BlockSpec((tt, LANES), tile),
            pl.BlockSpec((1, 1, LANES), lambda t, *_: (t, 0, 0)),
        ],
        out_specs=pl.BlockSpec(memory_space=pl.ANY),
        scratch_shapes=[pltpu.VMEM((2, STAGE_ROWS, width), BF16),
                        pltpu.VMEM((MOE_ROW_TILE, width), BF16),
                        pltpu.SemaphoreType.DMA((2,))],
    )
    return pl.pallas_call(
        _dispatch_kernel,
        grid_spec=grid_spec,
        out_shape=jax.ShapeDtypeStruct((n_slots, width), BF16),
        compiler_params=_params("arbitrary"),
        name="moe_dispatch",
    )(plan["seg_start"], plan["seg_chunks"], plan["tile_off"], plan["tail_start"], plan["tail_chunks"],
      plan["n_used"], h2, wcols, sel, plan["off_lanes"])


def _moe_kernel(tile_expert_ref, n_used_ref, xs_ref, w1_ref, w3_ref, w2_ref, ys_ref, acc_ref):
    i = pl.program_id(0)
    f = pl.program_id(1)

    @pl.when(i < n_used_ref[0])
    def _():
        @pl.when(f == 0)
        def _():
            acc_ref[...] = jnp.zeros(acc_ref.shape, F32)

        d = w1_ref.shape[1]
        x = xs_ref[:, :d]
        wc = xs_ref[:, d:].astype(F32)
        lane = lax.broadcasted_iota(jnp.int32, wc.shape, 1)
        mine = jnp.logical_and((lane & (N_EXPERTS - 1)) == tile_expert_ref[i], lane < 3 * N_EXPERTS)
        wt = jnp.sum(jnp.where(mine, wc, 0.0), axis=1, keepdims=True)
        a = _silu(jnp.dot(x, w1_ref[0], preferred_element_type=F32)) * jnp.dot(x, w3_ref[0], preferred_element_type=F32)
        acc_ref[...] += jnp.dot((a * wt).astype(BF16), w2_ref[0], preferred_element_type=F32)

        @pl.when(f == pl.num_programs(1) - 1)
        def _():
            ys_ref[...] = acc_ref[...].astype(BF16)

    @pl.when(jnp.logical_and(i >= n_used_ref[0], f == 0))
    def _():
        ys_ref[...] = jnp.zeros(ys_ref.shape, BF16)


def moe_experts(plan, xs, w1, w3, w2):
    n_slots, width = xs.shape
    ne, d, dff = w1.shape
    tm, tf = MOE_ROW_TILE, MOE_FF_TILE
    n_f = dff // tf

    def row(i, f, te, nu):
        return jnp.minimum(i, nu[0] - 1)

    def ff(i, f, te, nu):
        return jnp.where(i < nu[0], f, n_f - 1)

    grid_spec = pltpu.PrefetchScalarGridSpec(
        num_scalar_prefetch=2,
        grid=(n_slots // tm, n_f),
        in_specs=[
            pl.BlockSpec((tm, width), lambda i, f, te, nu: (row(i, f, te, nu), 0)),
            pl.BlockSpec((1, d, tf), lambda i, f, te, nu: (te[row(i, f, te, nu)], 0, ff(i, f, te, nu))),
            pl.BlockSpec((1, d, tf), lambda i, f, te, nu: (te[row(i, f, te, nu)], 0, ff(i, f, te, nu))),
            pl.BlockSpec((1, tf, d), lambda i, f, te, nu: (te[row(i, f, te, nu)], ff(i, f, te, nu), 0)),
        ],
        out_specs=pl.BlockSpec((tm, d), lambda i, f, te, nu: (i, 0)),
        scratch_shapes=[pltpu.VMEM((tm, d), F32)],
    )
    return pl.pallas_call(
        _moe_kernel,
        grid_spec=grid_spec,
        out_shape=jax.ShapeDtypeStruct((n_slots, d), BF16),
        compiler_params=_params("arbitrary", "arbitrary"),
        name="moe_experts",
    )(plan["tile_expert"], plan["n_used"], xs, w1, w3, w2)


def _combine_kernel(seg_start_ref, seg_chunks_ref, tile_off_ref,
                    ys_ref, sel_ref, off_ref, x_ref, mod_ref, o_ref, ybuf_ref, sem):
    t = pl.program_id(0)
    buf = t % 2
    max_chunks = MOE_TOKEN_TILE // SLOT_ALIGN

    def tile_copies(tile, which, action):
        for e in range(N_EXPERTS):
            idx = tile * N_EXPERTS + e
            _segment_copies(seg_chunks_ref[idx], seg_start_ref[idx], tile_off_ref[idx],
                            ys_ref, ybuf_ref.at[which], sem.at[which], max_chunks, action)

    @pl.when(t == 0)
    def _():
        ybuf_ref[...] = jnp.zeros(ybuf_ref.shape, BF16)
        tile_copies(t, buf, "start")

    @pl.when(t + 1 < pl.num_programs(0))
    def _():
        tile_copies(t + 1, 1 - buf, "start")

    one_hot = _slot_one_hot(sel_ref, off_ref)
    tile_copies(t, buf, "wait")
    y = jnp.dot(one_hot, ybuf_ref[buf], preferred_element_type=F32)
    o_ref[...] = x_ref[...] + mod_ref[0, 5:6, :] * y


def moe_combine(plan, ys, sel, x1, mods, rows_per_seq):
    rows, d = x1.shape
    tt = MOE_TOKEN_TILE
    tile = lambda t, *_: (t, 0)
    grid_spec = pltpu.PrefetchScalarGridSpec(
        num_scalar_prefetch=3,
        grid=(rows // tt,),
        in_specs=[
            pl.BlockSpec(memory_space=pl.ANY),
            pl.BlockSpec((tt, LANES), tile),
            pl.BlockSpec((1, 1, LANES), lambda t, *_: (t, 0, 0)),
            pl.BlockSpec((tt, d), tile),
            pl.BlockSpec((1, 6, d), lambda t, *_: (1 + (t * tt) // rows_per_seq, 0, 0)),
        ],
        out_specs=pl.BlockSpec((tt, d), tile),
        scratch_shapes=[pltpu.VMEM((2, STAGE_ROWS, d), BF16), pltpu.SemaphoreType.DMA((2,))],
    )
    return pl.pallas_call(
        _combine_kernel,
        grid_spec=grid_spec,
        out_shape=jax.ShapeDtypeStruct((rows, d), F32),
        compiler_params=_params("arbitrary"),
        name="moe_combine",
    )(plan["seg_start"], plan["seg_chunks"], plan["tile_off"], ys, sel, plan["off_lanes"], x1, mods)


def _moe_plan(counts, n_row_tiles):
    counts = counts.astype(jnp.int32)
    seg = (counts + SLOT_ALIGN - 1) // SLOT_ALIGN * SLOT_ALIGN
    tile_off = jnp.cumsum(seg, axis=1) - seg
    total = jnp.sum(seg, axis=0)
    region = (total + MOE_ROW_TILE - 1) // MOE_ROW_TILE * MOE_ROW_TILE
    region_end = jnp.cumsum(region)
    region_start = region_end - region
    seg_start = region_start[None, :] + jnp.cumsum(seg, axis=0) - seg
    first_row = jnp.arange(n_row_tiles, dtype=jnp.int32) * MOE_ROW_TILE
    tile_expert = jnp.minimum(jnp.sum(first_row[:, None] >= region_end[None, :], axis=1), counts.shape[1] - 1)
    off_lanes = jnp.pad(tile_off.astype(F32), ((0, 0), (0, LANES - counts.shape[1])))
    return {
        "seg_start": seg_start.reshape(-1),
        "seg_chunks": (seg // SLOT_ALIGN).reshape(-1),
        "tile_off": tile_off.reshape(-1),
        "tail_start": region_start + total,
        "tail_chunks": (region - total) // SLOT_ALIGN,
        "tile_expert": tile_expert.astype(jnp.int32),
        "n_used": (region_end[-1:] // MOE_ROW_TILE).astype(jnp.int32),
        "off_lanes": off_lanes.reshape(counts.shape[0], 1, LANES),
    }


def _rope_tables(ctx_len, seq):
    rows = seq // GRID_W
    row = jnp.repeat(jnp.arange(rows, dtype=F32), GRID_W)
    col = jnp.tile(jnp.arange(GRID_W, dtype=F32), rows)
    inv = ROPE_BASE ** (-jnp.arange(ROPE_PAIRS_PER_AXIS, dtype=F32) / ROPE_PAIRS_PER_AXIS)
    ang = jnp.concatenate([row[:, None] * inv, col[:, None] * inv], axis=-1)
    cos, sin = jnp.cos(ang), jnp.sin(ang)
    cos_t = jnp.concatenate([cos, cos, cos, cos], axis=-1)
    sin_t = jnp.concatenate([-sin, sin, -sin, sin], axis=-1)
    cos_t = jnp.concatenate([jnp.ones((ctx_len, LANES), F32), cos_t], axis=0)
    sin_t = jnp.concatenate([jnp.zeros((ctx_len, LANES), F32), sin_t], axis=0)
    return cos_t, sin_t


def kernel(x, c, ctx, c_ctx, ada_w, ada_b, norm1_g, norm2_g, w_in, w_out, q_norm_g, k_norm_g, lambda_q1, lambda_k1, lambda_q2, lambda_k2, subln_g, conv_w, conv_b, gate_b, mnorm_g, ffn_w1, ffn_w3, ffn_w2, router_w, moe_w1, moe_w3, moe_w2):
    batch, seq, d = x.shape
    ctx_len = ctx.shape[1]
    depth = ada_w.shape[0]
    assert depth == 2, "layer 0 runs the dense FFN on all rows, layer 1 the experts on the latents"
    n_ctx_rows = batch * ctx_len
    n_ctx_tiles = n_ctx_rows // ROW_TILE
    tiles_per_seq = seq // ROW_TILE
    n_main = w_in.shape[2] - N_GATES
    assert n_main == 7 * A_WIDTH
    m_dh = (n_main - 3 * A_WIDTH) // 4 // M_HEADS
    gate_order = jnp.array(GATE_ORDER, jnp.int32)
    lane_pad = ((0, 0), (0, LANES - N_GATES))

    x_ctx = ctx.reshape(n_ctx_rows, d)
    x_lat = x.reshape(batch * seq, d)
    lat_first_tile = 0

    cc = jnp.zeros((2 * SUBLANES, d), F32).at[:batch].set(c).at[batch].set(c_ctx)
    mod_rows = adaln_all(cc, ada_w, ada_b)
    mods_all = jnp.concatenate([mod_rows[:, batch:batch + 1], mod_rows[:, :batch]], axis=1)
    mods_all = mods_all.reshape(depth, batch + 1, 6, d)

    cos_t, sin_t = _rope_tables(ctx_len, seq)

    for l in range(depth):
        last = l == depth - 1
        lam_init = 0.8 - 0.6 * math.exp(-0.3 * l)
        lam = (jnp.exp(jnp.sum(lambda_q1[l] * lambda_k1[l]).astype(F32))
               - jnp.exp(jnp.sum(lambda_q2[l] * lambda_k2[l]).astype(F32)) + lam_init)
        mods = mods_all[l]
        g_hi, g_lo = _split_bf16(w_in[l][:, n_main:][:, gate_order])
        g_hi, g_lo = jnp.pad(g_hi, lane_pad), jnp.pad(g_lo, lane_pad)
        w_all = jnp.concatenate([w_in[l, :, :n_main].astype(BF16), g_hi, g_lo], axis=1)
        qg = jnp.tile(q_norm_g[l] * (A_DH ** -0.5 * math.log2(math.e)), 2).reshape(1, LANES)
        kg = jnp.tile(k_norm_g[l], 2).reshape(1, LANES)

        qk, av, mq, mkt, mv, mo, g_row = proj_in(
            x_ctx, x_lat, lat_first_tile, mods, norm1_g[l].reshape(1, d), w_all, g_hi,
            conv_w[l], conv_b[l].reshape(1, -1), cos_t, sin_t, qg, kg, n_ctx_tiles, tiles_per_seq, m_dh ** -0.5)
        attn = functools.partial(attention, lam.reshape(1, 1), qk, av, subln_g[l].reshape(1, LANES),
                                 1.0 - lam_init, batch, ctx_len, seq)
        a_out = attn(latent_queries=True)
        m_ctx, m_out = mlstm(mq, mkt, mv, mo, g_row, gate_b[l].reshape(-1)[gate_order].reshape(N_GATES, 1),
                             mnorm_g[l].reshape(1, -1), batch, ctx_len, seq)
        w_out_b = w_out[l].astype(BF16)
        j = l // 2
        if not last:
            assert lat_first_tile == 0
            xs = mixer_ffn(a_out, attn(latent_queries=False), m_out, m_ctx, w_out_b, x_ctx, x_lat, mods,
                           norm2_g[l].reshape(1, d), ffn_w1[j].astype(BF16), ffn_w3[j].astype(BF16),
                           ffn_w2[j].astype(BF16), seq)
            x_ctx, x_lat, lat_first_tile = xs, xs, n_ctx_tiles
        else:
            x1, h2, sel, wcols, cnt = mixer_router(a_out, m_out, w_out_b, x_lat, lat_first_tile * ROW_TILE, mods,
                                                   norm2_g[l].reshape(1, d), router_w[j], seq)
            n_tok_tiles = x1.shape[0] // MOE_TOKEN_TILE
            counts = cnt[:, 0, :N_EXPERTS]
            worst = 2 * x1.shape[0] + n_tok_tiles * N_EXPERTS * (SLOT_ALIGN - 1) + N_EXPERTS * (MOE_ROW_TILE - SLOT_ALIGN)
            n_row_tiles = -(-worst // MOE_ROW_TILE)
            plan = _moe_plan(counts, n_row_tiles)
            slots = moe_dispatch(plan, h2, wcols, sel, n_row_tiles * MOE_ROW_TILE)
            ys = moe_experts(plan, slots, moe_w1[j].astype(BF16), moe_w3[j].astype(BF16), moe_w2[j].astype(BF16))
            out = moe_combine(plan, ys, sel, x1, mods, seq)
    return out.reshape(batch, seq, d)
```

```python
import functools
import math

import jax
import jax.numpy as jnp
from jax import lax
from jax.experimental import pallas as pl
from jax.experimental.pallas import tpu as pltpu

F32 = jnp.float32
BF16 = jnp.bfloat16
HIGHEST = lax.Precision.HIGHEST

GRID_W = 64
A_HEADS = 4
A_DH = 64
A_DV = 2 * A_DH
A_WIDTH = A_HEADS * A_DV
M_HEADS = 4
N_GATES = 4 * M_HEADS
CONV_K = 5
ROPE_BASE = 10000.0
ROPE_PAIRS_PER_AXIS = A_DH // 4
N_EXPERTS = 8
EPS = 1e-6

LANES = 128
SUBLANES = 8
ROW_TILE = 256
MLSTM_CHUNK = 128
VMEM_LIMIT = 56 * 1024 * 1024

NT_DIMS = (((1,), (1,)), ((), ()))
TN_DIMS = (((0,), (0,)), ((), ()))

GATE_ORDER = tuple(d * 2 * M_HEADS + k * M_HEADS + h for k in range(2) for d in range(2) for h in range(M_HEADS))


def _params(*sem):
    return pltpu.CompilerParams(dimension_semantics=sem, vmem_limit_bytes=VMEM_LIMIT)


def _silu(v):
    return v * jax.nn.sigmoid(v)


def _log_sigmoid(v):
    return jnp.minimum(v, 0.0) - jnp.log1p(jnp.exp(-jnp.abs(v)))


def _rms(v, axis=-1):
    return v * lax.rsqrt(jnp.mean(v * v, axis=axis, keepdims=True) + EPS)


def _split_bf16(w):
    hi = w.astype(BF16)
    return hi, (w - hi.astype(F32)).astype(BF16)


def _adaln_kernel(c_ref, w_ref, b_ref, o_ref):
    s = _silu(c_ref[...])
    o_ref[0] = jnp.dot(s, w_ref[0], precision=HIGHEST, preferred_element_type=F32) + b_ref[0]


def adaln_all(cc, ada_w, ada_b):
    depth, d, d6 = ada_w.shape
    n_col = d6 // d
    return pl.pallas_call(
        _adaln_kernel,
        grid=(depth, n_col),
        in_specs=[
            pl.BlockSpec(cc.shape, lambda l, j: (0, 0)),
            pl.BlockSpec((1, d, d), lambda l, j: (l, 0, j)),
            pl.BlockSpec((1, 1, d), lambda l, j: (l, 0, j)),
        ],
        out_specs=pl.BlockSpec((1, cc.shape[0], d), lambda l, j: (l, 0, j)),
        out_shape=jax.ShapeDtypeStruct((depth, cc.shape[0], d6), F32),
        compiler_params=_params("arbitrary", "arbitrary"),
        name="adaln",
    )(cc, ada_w, ada_b.reshape(depth, 1, d6))


def _proj_in_kernel(xa_ref, xb_ref, prev_ref, next_ref, mod_ref, g_ref, w_ref, wgh_ref, cw_ref, cb_ref,
                    cos_ref, sin_ref, qg_ref, kg_ref,
                    qk_ref, v_ref, mq_ref, mkt_ref, mv_ref, mo_ref, gr_ref, ext_ref,
                    *, n_ctx_tiles, tiles_per_seq, k_scale):
    i = pl.program_id(0)
    is_ctx = i < n_ctx_tiles
    pos = (i - n_ctx_tiles) % tiles_per_seq
    first = jnp.logical_or(is_ctx, pos == 0)
    last = jnp.logical_or(is_ctx, pos == tiles_per_seq - 1)
    tm = xa_ref.shape[0]
    halo = SUBLANES

    x_ext = jnp.concatenate([prev_ref[...], jnp.where(is_ctx, xa_ref[...], xb_ref[...]), next_ref[...]], axis=0)
    h_ext = _rms(x_ext) * g_ref[...] * (1.0 + mod_ref[0, 1:2, :]) + mod_ref[0, 0:1, :]
    h = h_ext[halo:halo + tm]
    hb = h.astype(BF16)

    def proj(j):
        return jnp.dot(hb, w_ref[:, j * A_WIDTH:(j + 1) * A_WIDTH], preferred_element_type=F32)

    pre = jnp.dot(h_ext.astype(BF16), w_ref[:, 3 * A_WIDTH:5 * A_WIDTH], preferred_element_type=F32)
    ext_ref[0:halo, :] = jnp.where(first, 0.0, pre[0:halo])
    ext_ref[halo:halo + tm, :] = pre[halo:halo + tm]
    ext_ref[halo + tm:, :] = jnp.where(last, 0.0, pre[halo + tm:])

    r = lax.broadcasted_iota(jnp.int32, (LANES, LANES), 0)
    c = lax.broadcasted_iota(jnp.int32, (LANES, LANES), 1)
    group_ones = jnp.where((r & -A_DH) == (c & -A_DH), 1.0, 0.0).astype(BF16)
    lane = lax.broadcasted_iota(jnp.int32, (1, LANES), 1)
    first_half = (lane & (A_DH - 1)) < (A_DH // 2)
    cos = cos_ref[...]
    sin = sin_ref[...]
    for j, gain_ref in ((0, qg_ref), (1, kg_ref)):
        acc = proj(j)
        for hd in range(A_HEADS):
            t = acc[:, hd * LANES:(hd + 1) * LANES]
            ss = jnp.dot((t * t).astype(BF16), group_ones, preferred_element_type=F32)
            tn = t * lax.rsqrt(ss * (1.0 / A_DH) + EPS) * gain_ref[...]
            rot = jnp.where(first_half, pltpu.roll(tn, LANES - A_DH // 2, 1), pltpu.roll(tn, A_DH // 2, 1))
            qk_ref[:, j * A_WIDTH + hd * LANES: j * A_WIDTH + (hd + 1) * LANES] = (tn * cos + rot * sin).astype(BF16)
    def conv_silu(c0, c1):
        acc = jnp.zeros((tm, c1 - c0), F32) + cb_ref[:, c0:c1]
        pad = CONV_K // 2
        for j in range(CONV_K):
            acc = acc + cw_ref[j:j + 1, c0:c1] * ext_ref[halo - pad + j:halo - pad + j + tm, c0:c1]
        return _silu(acc)

    def store_k(c0, c1):
        yk = conv_silu(A_WIDTH + c0, A_WIDTH + c1) * k_scale
        for ci in range(tm // MLSTM_CHUNK):
            mkt_ref[ci, c0:c1, :] = yk[ci * MLSTM_CHUNK:(ci + 1) * MLSTM_CHUNK].T.astype(BF16)

    half = A_WIDTH // 2
    v_ref[...] = proj(2).astype(BF16)
    mq_ref[:, :half] = conv_silu(0, half).astype(BF16)
    mv_ref[...] = proj(5).astype(BF16)
    mq_ref[:, half:] = conv_silu(half, A_WIDTH).astype(BF16)
    mo_ref[...] = proj(6)
    store_k(0, half)

    n_main = 7 * A_WIDTH
    p = jnp.dot(hb, w_ref[:, n_main:n_main + 2 * LANES], preferred_element_type=F32)
    h_lo = (h - hb.astype(F32)).astype(BF16)
    gates = p[:, :LANES] + p[:, LANES:] + jnp.dot(h_lo, wgh_ref[...], preferred_element_type=F32)
    gr_ref[...] = gates.T[:N_GATES, :]
    store_k(half, A_WIDTH)


def proj_in(xa, xb, b_first_tile, mods, norm_g, w_all, w_gate_hi, conv_w, conv_b, cos_t, sin_t, qg, kg,
            n_ctx_tiles, tiles_per_seq, k_scale):
    d = xa.shape[1]
    n_lat_tiles = xb.shape[0] // ROW_TILE - b_first_tile
    n_tiles = n_ctx_tiles + n_lat_tiles
    rows = n_tiles * ROW_TILE
    per = ROW_TILE // SUBLANES

    def seg(i):
        return jnp.where(i < n_ctx_tiles, 0, 1 + (i - n_ctx_tiles) // tiles_per_seq)

    def rope_blk(i):
        return jnp.where(i < n_ctx_tiles, 0, 1 + (i - n_ctx_tiles) % tiles_per_seq)

    def lat(i):
        return jnp.maximum(i - n_ctx_tiles, 0)

    row = lambda i: (i, 0)
    const = lambda i: (0, 0)
    kern = functools.partial(_proj_in_kernel, n_ctx_tiles=n_ctx_tiles, tiles_per_seq=tiles_per_seq, k_scale=k_scale)
    return pl.pallas_call(
        kern,
        grid=(n_tiles,),
        in_specs=[
            pl.BlockSpec((ROW_TILE, d), lambda i: (jnp.minimum(i, n_ctx_tiles - 1), 0)),
            pl.BlockSpec((ROW_TILE, d), lambda i: (lat(i) + b_first_tile, 0)),
            pl.BlockSpec((SUBLANES, d), lambda i: (jnp.maximum(lat(i) * per - 1, 0) + b_first_tile * per, 0)),
            pl.BlockSpec((SUBLANES, d),
                         lambda i: (jnp.minimum((lat(i) + 1) * per, n_lat_tiles * per - 1) + b_first_tile * per, 0)),
            pl.BlockSpec((1, 6, d), lambda i: (seg(i), 0, 0)),
            pl.BlockSpec((1, d), const),
            pl.BlockSpec(w_all.shape, const),
            pl.BlockSpec(w_gate_hi.shape, const),
            pl.BlockSpec(conv_w.shape, const),
            pl.BlockSpec(conv_b.shape, const),
            pl.BlockSpec((ROW_TILE, LANES), lambda i: (rope_blk(i), 0)),
            pl.BlockSpec((ROW_TILE, LANES), lambda i: (rope_blk(i), 0)),
            pl.BlockSpec((1, LANES), const),
            pl.BlockSpec((1, LANES), const),
        ],
        out_specs=[
            pl.BlockSpec((ROW_TILE, 2 * A_WIDTH), row),
            pl.BlockSpec((ROW_TILE, A_WIDTH), row),
            pl.BlockSpec((ROW_TILE, A_WIDTH), row),
            pl.BlockSpec((ROW_TILE // MLSTM_CHUNK, A_WIDTH, MLSTM_CHUNK), lambda i: (i, 0, 0)),
            pl.BlockSpec((ROW_TILE, A_WIDTH), row),
            pl.BlockSpec((ROW_TILE, A_WIDTH), row),
            pl.BlockSpec((N_GATES, ROW_TILE), lambda i: (0, i)),
        ],
        out_shape=[
            jax.ShapeDtypeStruct((rows, 2 * A_WIDTH), BF16),
            jax.ShapeDtypeStruct((rows, A_WIDTH), BF16),
            jax.ShapeDtypeStruct((rows, A_WIDTH), BF16),
            jax.ShapeDtypeStruct((rows // MLSTM_CHUNK, A_WIDTH, MLSTM_CHUNK), BF16),
            jax.ShapeDtypeStruct((rows, A_WIDTH), BF16),
            jax.ShapeDtypeStruct((rows, A_WIDTH), F32),
            jax.ShapeDtypeStruct((N_GATES, rows), F32),
        ],
        scratch_shapes=[pltpu.VMEM((ROW_TILE + 2 * SUBLANES, 2 * A_WIDTH), F32)],
        compiler_params=_params("arbitrary"),
        name="proj_in",
    )(xa, xb, xb, xb, mods, norm_g, w_all, w_gate_hi, conv_w, conv_b, cos_t, sin_t, qg, kg)


ATTN_KEY_BLOCK = 256


ATTN_SUB_TILE = 256
ATTN_Q_TILE = 2048


def _attn_kernel(lam_ref, q_ref, *refs, with_latent_keys, post_scale, n_cast):
    n_in = len(refs) - 2 * n_cast - 2
    cast_in, o_ref, cast_out, s_ref = refs[n_in:n_in + n_cast], refs[n_in + n_cast], refs[n_in + n_cast + 1:-1], refs[-1]
    if with_latent_keys:
        kc_ref, vc_ref, kl_ref, vl_ref, g_ref = refs[:n_in]
    else:
        kc_ref, vc_ref, g_ref = refs[:n_in]
    for src, dst in zip(cast_in, cast_out):
        dst[...] = src[...].astype(BF16)
    lam = lam_ref[0, 0]
    lane = lax.broadcasted_iota(jnp.int32, (1, LANES), 1)
    kb = ATTN_KEY_BLOCK
    ts = s_ref.shape[1]
    blocks = [(kc_ref, vc_ref, r0) for r0 in range(0, kc_ref.shape[0], kb)]
    if with_latent_keys:
        blocks += [(kl_ref, vl_ref, r0) for r0 in range(0, kl_ref.shape[0], kb)]

    units = [(u, m) for u in range(q_ref.shape[0] // ts) for m in range(2)]
    state = {}

    def pass1(n, j):
        u, m = units[n]
        if j == 0:
            q = q_ref[u * ts:(u + 1) * ts, :]
            keep = (lane < A_DH) if m == 0 else (lane >= A_DH)
            state[n] = {"q": jnp.where(keep, q, jnp.zeros_like(q)), "max": jnp.full((ts, LANES), -jnp.inf, F32)}
        k_ref, _, r0 = blocks[j]
        s = lax.dot_general(state[n]["q"], k_ref[r0:r0 + kb, :], NT_DIMS, preferred_element_type=F32)
        s_ref[n % 2, :, j * kb:(j + 1) * kb] = s
        for c0 in range(0, kb, LANES):
            state[n]["max"] = jnp.maximum(state[n]["max"], s[:, c0:c0 + LANES])

    ones = jnp.ones((kb, LANES), BF16)

    def pass2(n, j):
        st = state[n]
        if j == 0:
            st["mx"] = jnp.broadcast_to(jnp.max(st["max"], axis=1, keepdims=True), (ts, kb))
            st["acc"] = jnp.zeros((ts, 2 * LANES), F32)
        _, v_ref, r0 = blocks[j]
        e = jnp.exp2(s_ref[n % 2, :, j * kb:(j + 1) * kb] - st["mx"])
        v_ext = jnp.concatenate([v_ref[r0:r0 + kb, :], ones], axis=1)
        st["acc"] = st["acc"] + jnp.dot(e.astype(BF16), v_ext, preferred_element_type=F32)

    def finish(u):
        first, second = state.pop(2 * u)["acc"], state.pop(2 * u + 1)["acc"]
        out = first[:, :LANES] / first[:, LANES:] - second[:, :LANES] * (lam / second[:, LANES:])
        o_ref[u * ts:(u + 1) * ts, :] = (_rms(out) * g_ref[...] * post_scale).astype(BF16)

    for n in range(len(units) + 1):
        for j in range(len(blocks)):
            if n < len(units):
                pass1(n, j)
            if n >= 1:
                pass2(n - 1, j)
        if n >= 1 and units[n - 1][1] == 1:
            finish(units[n - 1][0])


def attention(lam, qk, v, subln_g, post_scale, batch, ctx_len, seq, latent_queries, to_bf16=()):
    lat_blk0 = batch * ctx_len // seq
    tq = ATTN_Q_TILE if latent_queries else ctx_len
    nq = seq // tq if latent_queries else 1
    q0 = batch * ctx_len // tq if latent_queries else 0
    q_per_batch = nq
    n_keys = ctx_len + (seq if latent_queries else 0)
    in_specs = [
        pl.BlockSpec(memory_space=pltpu.SMEM),
        pl.BlockSpec((tq, LANES), lambda b, h, t: (q0 + b * q_per_batch + t, h)),
        pl.BlockSpec((ctx_len, LANES), lambda b, h, t: (b, A_HEADS + h)),
        pl.BlockSpec((ctx_len, LANES), lambda b, h, t: (b, h)),
    ]
    args = [lam, qk, qk, v]
    if latent_queries:
        in_specs += [pl.BlockSpec((seq, LANES), lambda b, h, t: (lat_blk0 + b, A_HEADS + h)),
                     pl.BlockSpec((seq, LANES), lambda b, h, t: (lat_blk0 + b, h))]
        args += [qk, v]
    in_specs.append(pl.BlockSpec((1, LANES), lambda b, h, t: (0, 0)))
    args.append(subln_g)
    out_specs = [pl.BlockSpec((tq, LANES), lambda b, h, t: (b * q_per_batch + t, h))]
    out_shape = [jax.ShapeDtypeStruct((batch * nq * tq, A_WIDTH), BF16)]
    n_steps = batch * A_HEADS * nq
    for w in to_bf16:
        blk = pl.BlockSpec((w.shape[0] // n_steps, w.shape[1]), lambda b, h, t: ((b * A_HEADS + h) * nq + t, 0))
        in_specs.append(blk)
        args.append(w)
        out_specs.append(blk)
        out_shape.append(jax.ShapeDtypeStruct(w.shape, BF16))
    kern = functools.partial(_attn_kernel, with_latent_keys=latent_queries, post_scale=post_scale,
                             n_cast=len(to_bf16))
    outs = pl.pallas_call(
        kern,
        grid=(batch, A_HEADS, nq),
        in_specs=in_specs,
        out_specs=out_specs,
        out_shape=out_shape,
        scratch_shapes=[pltpu.VMEM((2, min(tq, ATTN_SUB_TILE), n_keys), F32)],
        compiler_params=_params("arbitrary", "arbitrary", "arbitrary"),
        name="diff_attention",
    )(*args)
    return outs if to_bf16 else outs[0]


def _mlstm_kernel(qc_ref, ql_ref, ktc_ref, ktl_ref, vc_ref, vl_ref, oc_ref, ol_ref, grc_ref, grl_ref, gb_ref, mg_ref,
                  outc_ref, outl_ref, row_ref, col_ref, s_ref, hf_ref, hb_ref):
    L = MLSTM_CHUNK
    dh = LANES
    n_ch = 2 * M_HEADS
    n_ctx = qc_ref.shape[0] // L
    n_lat = ql_ref.shape[0] // L
    n_chunks = n_ctx + n_lat
    nt = n_chunks * L

    g = jnp.concatenate([grc_ref[...], grl_ref[...]], axis=1) + gb_ref[...]
    log_i = g[:n_ch]
    log_f = _log_sigmoid(g[n_ch:])
    sub = lax.broadcasted_iota(jnp.int32, (n_ch, nt), 0)
    lane_in_chunk = lax.broadcasted_iota(jnp.int32, (n_ch, nt), 1) & (L - 1)
    fwd_rows = sub < M_HEADS

    def chunk_scan(x, op, ident):
        xf, xb, k = x, x, 1
        while k < L:
            xf = op(xf, jnp.where(lane_in_chunk >= k, pltpu.roll(xf, k, 1), ident))
            xb = op(xb, jnp.where(lane_in_chunk < L - k, pltpu.roll(xb, nt - k, 1), ident))
            k *= 2
        return jnp.where(fwd_rows, xf, xb)

    b = chunk_scan(log_f, jnp.add, 0.0)
    a = log_i - b
    a_run = chunk_scan(a, jnp.maximum, -jnp.inf)

    fwd_col = lax.broadcasted_iota(jnp.int32, (n_ch, 1), 0) < M_HEADS

    def at_chunk_end(x, ci):
        xc = x[:, ci * L:(ci + 1) * L]
        return jnp.where(fwd_col, xc[:, L - 1:L], xc[:, 0:1])

    b_last = [at_chunk_end(b, ci) for ci in range(n_chunks)]
    a_last = [at_chunk_end(a_run, ci) for ci in range(n_chunks)]
    order_f = list(range(n_chunks))
    order_b = list(range(n_ctx - 1, -1, -1)) + list(range(n_chunks - 1, n_ctx - 1, -1))

    def carried(order):
        m, out = jnp.zeros((n_ch, 1), F32), {}
        for ci in order:
            out[ci] = m
            m = b_last[ci] + jnp.maximum(m, a_last[ci])
        return out

    m_f, m_b = carried(order_f), carried(order_b)
    for ci in range(n_chunks):
        m_prev = jnp.where(fwd_col, m_f[ci], m_b[ci])
        sl = slice(ci * L, (ci + 1) * L)
        mu = jnp.maximum(m_prev, a_run[:, sl])
        mu_last = jnp.maximum(m_prev, a_last[ci])
        row_ref[ci] = jnp.concatenate([a[:, sl], jnp.exp(a[:, sl] - mu_last), jnp.broadcast_to(m_prev, (n_ch, L)),
                                       jnp.broadcast_to(jnp.exp(m_prev - mu_last), (n_ch, L))], axis=0)
        tile = jnp.concatenate([mu, b[:, sl] + mu, jnp.zeros((L - 2 * n_ch, L), F32)], axis=0)
        col_ref[ci] = tile.T

    s_ref[...] = jnp.zeros(s_ref.shape, F32)
    rr = lax.broadcasted_iota(jnp.int32, (L, L), 0)
    cc = lax.broadcasted_iota(jnp.int32, (L, L), 1)
    visible = (cc <= rr, cc >= rr)
    ones = jnp.ones((L, dh), BF16)

    def chunk_rows(ci):
        return pl.ds(ci * L if isinstance(ci, int) else pl.multiple_of(ci * L, L), L)

    def twice(t):
        return jnp.concatenate([t, t], axis=1)

    def step(local, glob, q_ref, kt_ref, v_ref):
        for d in range(2):
            rows = chunk_rows(local[d])
            grow = chunk_rows(glob[d])
            cs = col_ref[glob[d]]
            rs = row_ref[glob[d]]
            h_ref = hf_ref if d == 0 else hb_ref
            for hd in range(M_HEADS):
                r = d * M_HEADS + hd
                mu = jnp.broadcast_to(cs[:, r:r + 1], (L, L))
                m_t = jnp.broadcast_to(cs[:, n_ch + r:n_ch + r + 1], (L, dh))
                a_row, w_row, m_prev, decay = (rs[j * n_ch + r:j * n_ch + r + 1, :] for j in range(4))
                w_intra = jnp.exp(jnp.where(visible[d], a_row - mu, -jnp.inf))
                w_inter = jnp.exp(m_prev - mu)
                cols = slice(hd * dh, (hd + 1) * dh)
                q = q_ref[rows, cols]
                kt = kt_ref[local[d], cols, :]
                v_ext = jnp.concatenate([v_ref[rows, cols], ones], axis=1)
                state = s_ref[r]
                qa = jnp.dot(q, jnp.concatenate([kt, state.astype(BF16)], axis=1), preferred_element_type=F32)
                s = (qa[:, :L] * w_intra).astype(BF16)
                ktw = (kt.astype(F32) * w_row).astype(BF16)
                sv = jnp.dot(jnp.concatenate([s, ktw], axis=0), v_ext, preferred_element_type=F32)
                t = sv[:L] + twice(w_inter) * qa[:, L:]
                h_ref[grow, cols] = t[:, :dh] / jnp.maximum(jnp.abs(t[:, dh:]), jnp.exp(-m_t))
                s_ref[r] = twice(jnp.broadcast_to(decay, (dh, L))) * state + sv[L:]

    def finish(local, glob, o_ref, out_ref):
        rows = chunk_rows(local)
        grow = chunk_rows(glob)
        for hd in range(M_HEADS):
            cols = slice(hd * dh, (hd + 1) * dh)
            hn = _rms(hf_ref[grow, cols] + hb_ref[grow, cols]) * mg_ref[:, cols]
            out_ref[rows, cols] = (jax.nn.sigmoid(o_ref[rows, cols]) * hn).astype(BF16)

    for i in range(n_ctx):
        step((i, n_ctx - 1 - i), (i, n_ctx - 1 - i), qc_ref, ktc_ref, vc_ref)
    for i in range(n_ctx):
        finish(i, i, oc_ref, outc_ref)

    def lat_step(i, carry):
        step((i, n_lat - 1 - i), (n_ctx + i, n_chunks - 1 - i), ql_ref, ktl_ref, vl_ref)
        return carry

    def lat_step_and_finish(i, carry):
        lat_step(i, carry)
        finish(i, n_ctx + i, ol_ref, outl_ref)
        finish(n_lat - 1 - i, n_chunks - 1 - i, ol_ref, outl_ref)
        return carry

    assert n_lat % 2 == 0
    lax.fori_loop(0, n_lat // 2, lat_step, 0)
    lax.fori_loop(n_lat // 2, n_lat, lat_step_and_finish, 0)


def mlstm(mq, mkt, mv, mo, g_row, gate_b, mnorm_g, batch, ctx_len, seq):
    rows, width = mq.shape
    assert MLSTM_CHUNK == LANES and width == M_HEADS * LANES
    lat0 = batch * ctx_len // seq
    n_chunks = (ctx_len + seq) // MLSTM_CHUNK
    ctx = lambda b: (b, 0)
    lat = lambda b: (lat0 + b, 0)
    const = lambda b: (0, 0)
    out_c, out_l = pl.pallas_call(
        _mlstm_kernel,
        grid=(batch,),
        in_specs=[
            pl.BlockSpec((ctx_len, width), ctx), pl.BlockSpec((seq, width), lat),
            pl.BlockSpec((ctx_len // MLSTM_CHUNK, width, MLSTM_CHUNK), lambda b: (b, 0, 0)),
            pl.BlockSpec((seq // MLSTM_CHUNK, width, MLSTM_CHUNK), lambda b: (lat0 + b, 0, 0)),
            pl.BlockSpec((ctx_len, width), ctx), pl.BlockSpec((seq, width), lat),
            pl.BlockSpec((ctx_len, width), ctx), pl.BlockSpec((seq, width), lat),
            pl.BlockSpec((N_GATES, ctx_len), lambda b: (0, b)), pl.BlockSpec((N_GATES, seq), lambda b: (0, lat0 + b)),
            pl.BlockSpec((N_GATES, 1), const),
            pl.BlockSpec((1, width), const),
        ],
        out_specs=[pl.BlockSpec((ctx_len, width), ctx), pl.BlockSpec((seq, width), ctx)],
        out_shape=[jax.ShapeDtypeStruct((batch * ctx_len, width), BF16),
                   jax.ShapeDtypeStruct((batch * seq, width), BF16)],
        scratch_shapes=[
            pltpu.VMEM((n_chunks, 8 * M_HEADS, MLSTM_CHUNK), F32),
            pltpu.VMEM((n_chunks, MLSTM_CHUNK, MLSTM_CHUNK), F32),
            pltpu.VMEM((2 * M_HEADS, LANES, 2 * LANES), F32),
            pltpu.VMEM((ctx_len + seq, width), F32),
            pltpu.VMEM((ctx_len + seq, width), F32),
        ],
        compiler_params=_params("arbitrary"),
        name="mlstm",
    )(mq, mq, mkt, mkt, mv, mv, mo, mo, g_row, g_row, gate_b, mnorm_g)
    return out_c, out_l


def _router_top2(logits):
    lane = lax.broadcasted_iota(jnp.int32, logits.shape, 1)
    logits = jnp.where(lane < N_EXPERTS, logits, -jnp.inf)
    v1 = jnp.max(logits, axis=1, keepdims=True)
    i1 = jnp.min(jnp.where(logits == v1, lane, LANES), axis=1, keepdims=True)
    rest = jnp.where(lane == i1, -jnp.inf, logits)
    v2 = jnp.max(rest, axis=1, keepdims=True)
    i2 = jnp.min(jnp.where(rest == v2, lane, LANES), axis=1, keepdims=True)
    e = jnp.exp(v2 - v1)
    w1 = 1.0 / (1.0 + e)
    w2 = e / (1.0 + e)
    mask = jnp.where(jnp.logical_or(lane == i1, lane == i2), 1.0, 0.0)
    expert = lane & (N_EXPERTS - 1)
    comb = jnp.where(lane < 3 * N_EXPERTS,
                     jnp.where(expert == i1, w1, 0.0) + jnp.where(expert == i2, w2, 0.0), 0.0)
    hi = comb.astype(BF16)
    rem = comb - hi.astype(F32)
    mid = rem.astype(BF16)
    lo = (rem - mid.astype(F32)).astype(BF16)
    wcols = jnp.where(lane < N_EXPERTS, hi, jnp.where(lane < 2 * N_EXPERTS, mid, lo))
    return mask, wcols


def _mixer_router_kernel(a_ref, m_ref, w_ref, x_ref, mod_ref, g_ref, rw_ref, x1_ref, h2_ref, sel_ref, wc_ref, cnt_ref):
    half = a_ref.shape[1]
    mix = (jnp.dot(a_ref[...], w_ref[:half, :], preferred_element_type=F32)
           + jnp.dot(m_ref[...], w_ref[half:, :], preferred_element_type=F32))
    x1 = x_ref[...] + mod_ref[0, 2:3, :] * mix
    x1_ref[...] = x1
    h2 = _rms(x1) * g_ref[...] * (1.0 + mod_ref[0, 4:5, :]) + mod_ref[0, 3:4, :]
    hb = h2.astype(BF16)
    h2_ref[...] = hb
    p = jnp.dot(hb, rw_ref[...], preferred_element_type=F32)
    h_lo = (h2 - hb.astype(F32)).astype(BF16)
    logits = p[:, :LANES] + p[:, LANES:] + jnp.dot(h_lo, rw_ref[:, :LANES], preferred_element_type=F32)
    mask, wcols = _router_top2(logits)
    sel_ref[...] = mask.astype(BF16)
    wc_ref[...] = wcols
    cnt_ref[0] = jnp.sum(mask, axis=0, keepdims=True)


def mixer_router(a_lat, m_lat, w_out, xs, first_row, mods, norm_g, router_w, rows_per_seq):
    d = xs.shape[1]
    half = a_lat.shape[1]
    tm = MOE_TOKEN_TILE
    n_tiles = a_lat.shape[0] // tm
    first_tile = first_row // tm
    dst = lambda i: (i, 0)
    const = lambda i: (0, 0)
    pad = ((0, 0), (0, LANES - router_w.shape[1]))
    rw = jnp.concatenate([jnp.pad(t, pad) for t in _split_bf16(router_w)], axis=1)
    return pl.pallas_call(
        _mixer_router_kernel,
        grid=(n_tiles,),
        in_specs=[
            pl.BlockSpec((tm, half), dst), pl.BlockSpec((tm, half), dst),
            pl.BlockSpec(w_out.shape, const),
            pl.BlockSpec((tm, d), lambda i: (i + first_tile, 0)),
            pl.BlockSpec((1, 6, d), lambda i: (1 + (i * tm) // rows_per_seq, 0, 0)),
            pl.BlockSpec((1, d), const),
            pl.BlockSpec(rw.shape, const),
        ],
        out_specs=[pl.BlockSpec((tm, d), dst), pl.BlockSpec((tm, d), dst),
                   pl.BlockSpec((tm, LANES), dst), pl.BlockSpec((tm, LANES), dst),
                   pl.BlockSpec((1, 1, LANES), lambda i: (i, 0, 0))],
        out_shape=[jax.ShapeDtypeStruct((n_tiles * tm, d), F32),
                   jax.ShapeDtypeStruct((n_tiles * tm, d), BF16),
                   jax.ShapeDtypeStruct((n_tiles * tm, LANES), BF16),
                   jax.ShapeDtypeStruct((n_tiles * tm, LANES), BF16),
                   jax.ShapeDtypeStruct((n_tiles, 1, LANES), F32)],
        compiler_params=_params("arbitrary"),
        name="mixer_router",
    )(a_lat, m_lat, w_out, xs, mods, norm_g, rw)


FFN_ROW_TILE = 512
FFN_FF_TILE = 1408


def _mixer_ffn_kernel(a_ref, ac_ref, m_ref, mc_ref, wo_ref, xc_ref, x_ref, mod_ref, g_ref, w1_ref, w3_ref, w2_ref,
                      o_ref, x1_ref, h2_ref, acc_ref, *, n_ctx_tiles):
    f = pl.program_id(1)

    @pl.when(f == 0)
    def _():
        is_ctx = pl.program_id(0) < n_ctx_tiles
        a = jnp.where(is_ctx, ac_ref[...], a_ref[...])
        m = jnp.where(is_ctx, mc_ref[...], m_ref[...])
        x = jnp.where(is_ctx, xc_ref[...], x_ref[...])
        half = a.shape[1]
        mix = (jnp.dot(a, wo_ref[:half, :], preferred_element_type=F32)
               + jnp.dot(m, wo_ref[half:, :], preferred_element_type=F32))
        x1 = x + mod_ref[0, 2:3, :] * mix
        x1_ref[...] = x1
        h2_ref[...] = (_rms(x1) * g_ref[...] * (1.0 + mod_ref[0, 4:5, :]) + mod_ref[0, 3:4, :]).astype(BF16)
        acc_ref[...] = jnp.zeros(acc_ref.shape, F32)

    h = h2_ref[...]
    a = _silu(jnp.dot(h, w1_ref[...], preferred_element_type=F32)) * jnp.dot(h, w3_ref[...], preferred_element_type=F32)
    acc_ref[...] += jnp.dot(a.astype(BF16), w2_ref[...], preferred_element_type=F32)

    @pl.when(f == pl.num_programs(1) - 1)
    def _():
        o_ref[...] = x1_ref[...] + mod_ref[0, 5:6, :] * acc_ref[...]


def mixer_ffn(a_lat, a_ctx, m_lat, m_ctx, w_out, x_ctx, x_lat, mods, norm_g, w1, w3, w2, rows_per_seq):
    d = x_lat.shape[1]
    half = a_lat.shape[1]
    dff = w1.shape[1]
    tm, tf = FFN_ROW_TILE, FFN_FF_TILE
    n_ctx_tiles = x_ctx.shape[0] // tm
    n_tiles = n_ctx_tiles + x_lat.shape[0] // tm
    per_seq = rows_per_seq // tm
    ctx = lambda i, f: (jnp.minimum(i, n_ctx_tiles - 1), 0)
    lat = lambda i, f: (jnp.maximum(i - n_ctx_tiles, 0), 0)
    const = lambda i, f: (0, 0)

    def seg(i, f):
        return (jnp.where(i < n_ctx_tiles, 0, 1 + (i - n_ctx_tiles) // per_seq), 0, 0)

    return pl.pallas_call(
        functools.partial(_mixer_ffn_kernel, n_ctx_tiles=n_ctx_tiles),
        grid=(n_tiles, dff // tf),
        in_specs=[
            pl.BlockSpec((tm, half), lat), pl.BlockSpec((tm, half), ctx),
            pl.BlockSpec((tm, half), lat), pl.BlockSpec((tm, half), ctx),
            pl.BlockSpec(w_out.shape, const),
            pl.BlockSpec((tm, d), ctx), pl.BlockSpec((tm, d), lat),
            pl.BlockSpec((1, 6, d), seg),
            pl.BlockSpec((1, d), const),
            pl.BlockSpec((d, tf), lambda i, f: (0, f)),
            pl.BlockSpec((d, tf), lambda i, f: (0, f)),
            pl.BlockSpec((tf, d), lambda i, f: (f, 0)),
        ],
        out_specs=pl.BlockSpec((tm, d), lambda i, f: (i, 0)),
        out_shape=jax.ShapeDtypeStruct((n_tiles * tm, d), F32),
        scratch_shapes=[pltpu.VMEM((tm, d), F32), pltpu.VMEM((tm, d), BF16), pltpu.VMEM((tm, d), F32)],
        compiler_params=_params("arbitrary", "arbitrary"),
        name="mixer_ffn",
    )(a_lat, a_ctx, m_lat, m_ctx, w_out, x_ctx, x_lat, mods, norm_g, w1, w3, w2)


MOE_TOKEN_TILE = 512
MOE_ROW_TILE = 512
MOE_FF_TILE = 1792
SLOT_ALIGN = 16
STAGE_ROWS = 2 * MOE_TOKEN_TILE + LANES


def _pow2_chunks(max_chunks):
    k = 1
    while k * 2 <= max_chunks:
        k *= 2
    while k >= 1:
        yield k
        k //= 2


def _segment_copies(n_chunks, src, dst, src_ref, dst_ref, sem, max_chunks, action):
    for k in _pow2_chunks(max_chunks):
        rows = k * SLOT_ALIGN
        taken = (n_chunks & ~(2 * k - 1)) * SLOT_ALIGN

        @pl.when((n_chunks & k) != 0)
        def _(rows=rows, taken=taken):
            s = pl.multiple_of(src + taken, SLOT_ALIGN)
            t = pl.multiple_of(dst + taken, SLOT_ALIGN)
            cp = pltpu.make_async_copy(src_ref.at[pl.ds(s, rows), :], dst_ref.at[pl.ds(t, rows), :], sem)
            if action == "start":
                cp.start()
            else:
                cp.wait()


def _slot_one_hot(sel_ref, off_ref):
    sel = sel_ref[...]
    tt = sel.shape[0]
    r = lax.broadcasted_iota(jnp.int32, (tt, tt), 0)
    c = lax.broadcasted_iota(jnp.int32, (tt, tt), 1)
    earlier = jnp.where(c < r, 1.0, 0.0).astype(BF16)
    rank = jnp.dot(earlier, sel, preferred_element_type=F32)
    pos = off_ref[0] + rank
    chosen = sel > 0
    pa = jnp.min(jnp.where(chosen, pos, float(STAGE_ROWS)), axis=1, keepdims=True)
    pb = jnp.max(jnp.where(chosen, pos, -1.0), axis=1, keepdims=True)
    slot = lax.broadcasted_iota(jnp.int32, (tt, STAGE_ROWS), 1).astype(F32)
    return jnp.where(jnp.logical_or(slot == pa, slot == pb), 1.0, 0.0).astype(BF16)


def _dispatch_kernel(seg_start_ref, seg_chunks_ref, tile_off_ref, tail_start_ref, tail_chunks_ref, n_used_ref,
                     h_ref, wc_ref, sel_ref, off_ref, xs_ref, stage_ref, zero_ref, sem):
    t = pl.program_id(0)
    last = pl.num_programs(0) - 1
    buf = t % 2
    max_chunks = MOE_TOKEN_TILE // SLOT_ALIGN

    def tile_copies(tile, which, action):
        for e in range(N_EXPERTS):
            idx = tile * N_EXPERTS + e
            _segment_copies(seg_chunks_ref[idx], tile_off_ref[idx], seg_start_ref[idx],
                            stage_ref.at[which], xs_ref, sem.at[which], max_chunks, action)

    one_hot = _slot_one_hot(sel_ref, off_ref)
    payload = jnp.concatenate([h_ref[...], wc_ref[...]], axis=1)
    stage_ref[buf] = lax.dot_general(one_hot, payload, TN_DIMS, preferred_element_type=F32).astype(BF16)
    tile_copies(t, buf, "start")

    @pl.when(t > 0)
    def _():
        tile_copies(t - 1, 1 - buf, "wait")

    @pl.when(t == last)
    def _():
        tile_copies(t, buf, "wait")
        zero_ref[...] = jnp.zeros(zero_ref.shape, BF16)
        for action in ("start", "wait"):
            for e in range(N_EXPERTS):
                _segment_copies(tail_chunks_ref[e], 0, tail_start_ref[e], zero_ref, xs_ref, sem.at[0],
                                MOE_ROW_TILE // SLOT_ALIGN - 1, action)

        def clear_tile(j, carry):
            rows = pl.ds(pl.multiple_of(j * MOE_ROW_TILE, MOE_ROW_TILE), MOE_ROW_TILE)
            cp = pltpu.make_async_copy(zero_ref, xs_ref.at[rows, :], sem.at[0])
            cp.start()
            cp.wait()
            return carry

        lax.fori_loop(n_used_ref[0], xs_ref.shape[0] // MOE_ROW_TILE, clear_tile, 0)


def moe_dispatch(plan, h2, wcols, sel, n_slots):
    rows, d = h2.shape
    tt = MOE_TOKEN_TILE
    width = d + LANES
    tile = lambda t, *_: (t, 0)
    grid_spec = pltpu.PrefetchScalarGridSpec(
        num_scalar_prefetch=6,
        grid=(rows // tt,),
        in_specs=[
            pl.BlockSpec((tt, d), tile),
            pl.BlockSpec((tt, LANES), tile),
            pl.BlockSpec((tt, LANES), tile),
            pl.BlockSpec((1, 1, LANES), lambda t, *_: (t, 0, 0)),
        ],
        out_specs=pl.BlockSpec(memory_space=pl.ANY),
        scratch_shapes=[pltpu.VMEM((2, STAGE_ROWS, width), BF16),
                        pltpu.VMEM((MOE_ROW_TILE, width), BF16),
                        pltpu.SemaphoreType.DMA((2,))],
    )
    return pl.pallas_call(
        _dispatch_kernel,
        grid_spec=grid_spec,
        out_shape=jax.ShapeDtypeStruct((n_slots, width), BF16),
        compiler_params=_params("arbitrary"),
        name="moe_dispatch",
    )(plan["seg_start"], plan["seg_chunks"], plan["tile_off"], plan["tail_start"], plan["tail_chunks"],
      plan["n_used"], h2, wcols, sel, plan["off_lanes"])


def _moe_kernel(tile_expert_ref, n_used_ref, xs_ref, w1_ref, w3_ref, w2_ref, ys_ref, acc_ref):
    i = pl.program_id(0)
    f = pl.program_id(1)

    @pl.when(i < n_used_ref[0])
    def _():
        @pl.when(f == 0)
        def _():
            acc_ref[...] = jnp.zeros(acc_ref.shape, F32)

        d = w1_ref.shape[1]
        x = xs_ref[:, :d]
        wc = xs_ref[:, d:].astype(F32)
        lane = lax.broadcasted_iota(jnp.int32, wc.shape, 1)
        mine = jnp.logical_and((lane & (N_EXPERTS - 1)) == tile_expert_ref[i], lane < 3 * N_EXPERTS)
        wt = jnp.sum(jnp.where(mine, wc, 0.0), axis=1, keepdims=True)
        a = _silu(jnp.dot(x, w1_ref[0], preferred_element_type=F32)) * jnp.dot(x, w3_ref[0], preferred_element_type=F32)
        acc_ref[...] += jnp.dot((a * wt).astype(BF16), w2_ref[0], preferred_element_type=F32)

        @pl.when(f == pl.num_programs(1) - 1)
        def _():
            ys_ref[...] = acc_ref[...].astype(BF16)

    @pl.when(jnp.logical_and(i >= n_used_ref[0], f == 0))
    def _():
        ys_ref[...] = jnp.zeros(ys_ref.shape, BF16)


def moe_experts(plan, xs, w1, w3, w2):
    n_slots, width = xs.shape
    ne, d, dff = w1.shape
    tm, tf = MOE_ROW_TILE, MOE_FF_TILE
    n_f = dff // tf

    def row(i, f, te, nu):
        return jnp.minimum(i, nu[0] - 1)

    def ff(i, f, te, nu):
        return jnp.where(i < nu[0], f, n_f - 1)

    grid_spec = pltpu.PrefetchScalarGridSpec(
        num_scalar_prefetch=2,
        grid=(n_slots // tm, n_f),
        in_specs=[
            pl.BlockSpec((tm, width), lambda i, f, te, nu: (row(i, f, te, nu), 0)),
            pl.BlockSpec((1, d, tf), lambda i, f, te, nu: (te[row(i, f, te, nu)], 0, ff(i, f, te, nu))),
            pl.BlockSpec((1, d, tf), lambda i, f, te, nu: (te[row(i, f, te, nu)], 0, ff(i, f, te, nu))),
            pl.BlockSpec((1, tf, d), lambda i, f, te, nu: (te[row(i, f, te, nu)], ff(i, f, te, nu), 0)),
        ],
        out_specs=pl.BlockSpec((tm, d), lambda i, f, te, nu: (i, 0)),
        scratch_shapes=[pltpu.VMEM((tm, d), F32)],
    )
    return pl.pallas_call(
        _moe_kernel,
        grid_spec=grid_spec,
        out_shape=jax.ShapeDtypeStruct((n_slots, d), BF16),
        compiler_params=_params("arbitrary", "arbitrary"),
        name="moe_experts",
    )(plan["tile_expert"], plan["n_used"], xs, w1, w3, w2)


def _combine_kernel(seg_start_ref, seg_chunks_ref, tile_off_ref,
                    ys_ref, sel_ref, off_ref, x_ref, mod_ref, o_ref, ybuf_ref, sem):
    t = pl.program_id(0)
    buf = t % 2
    max_chunks = MOE_TOKEN_TILE // SLOT_ALIGN

    def tile_copies(tile, which, action):
        for e in range(N_EXPERTS):
            idx = tile * N_EXPERTS + e
            _segment_copies(seg_chunks_ref[idx], seg_start_ref[idx], tile_off_ref[idx],
                            ys_ref, ybuf_ref.at[which], sem.at[which], max_chunks, action)

    @pl.when(t == 0)
    def _():
        ybuf_ref[...] = jnp.zeros(ybuf_ref.shape, BF16)
        tile_copies(t, buf, "start")

    @pl.when(t + 1 < pl.num_programs(0))
    def _():
        tile_copies(t + 1, 1 - buf, "start")

    one_hot = _slot_one_hot(sel_ref, off_ref)
    tile_copies(t, buf, "wait")
    y = jnp.dot(one_hot, ybuf_ref[buf], preferred_element_type=F32)
    o_ref[...] = x_ref[...] + mod_ref[0, 5:6, :] * y


def moe_combine(plan, ys, sel, x1, mods, rows_per_seq):
    rows, d = x1.shape
    tt = MOE_TOKEN_TILE
    tile = lambda t, *_: (t, 0)
    grid_spec = pltpu.PrefetchScalarGridSpec(
        num_scalar_prefetch=3,
        grid=(rows // tt,),
        in_specs=[
            pl.BlockSpec(memory_space=pl.ANY),
            pl.BlockSpec((tt, LANES), tile),
            pl.BlockSpec((1, 1, LANES), lambda t, *_: (t, 0, 0)),
            pl.BlockSpec((tt, d), tile),
            pl.BlockSpec((1, 6, d), lambda t, *_: (1 + (t * tt) // rows_per_seq, 0, 0)),
        ],
        out_specs=pl.BlockSpec((tt, d), tile),
        scratch_shapes=[pltpu.VMEM((2, STAGE_ROWS, d), BF16), pltpu.SemaphoreType.DMA((2,))],
    )
    return pl.pallas_call(
        _combine_kernel,
        grid_spec=grid_spec,
        out_shape=jax.ShapeDtypeStruct((rows, d), F32),
        compiler_params=_params("arbitrary"),
        name="moe_combine",
    )(plan["seg_start"], plan["seg_chunks"], plan["tile_off"], ys, sel, plan["off_lanes"], x1, mods)


def _moe_plan(counts, n_row_tiles):
    counts = counts.astype(jnp.int32)
    seg = (counts + SLOT_ALIGN - 1) // SLOT_ALIGN * SLOT_ALIGN
    tile_off = jnp.cumsum(seg, axis=1) - seg
    total = jnp.sum(seg, axis=0)
    region = (total + MOE_ROW_TILE - 1) // MOE_ROW_TILE * MOE_ROW_TILE
    region_end = jnp.cumsum(region)
    region_start = region_end - region
    seg_start = region_start[None, :] + jnp.cumsum(seg, axis=0) - seg
    first_row = jnp.arange(n_row_tiles, dtype=jnp.int32) * MOE_ROW_TILE
    tile_expert = jnp.minimum(jnp.sum(first_row[:, None] >= region_end[None, :], axis=1), counts.shape[1] - 1)
    off_lanes = jnp.pad(tile_off.astype(F32), ((0, 0), (0, LANES - counts.shape[1])))
    return {
        "seg_start": seg_start.reshape(-1),
        "seg_chunks": (seg // SLOT_ALIGN).reshape(-1),
        "tile_off": tile_off.reshape(-1),
        "tail_start": region_start + total,
        "tail_chunks": (region - total) // SLOT_ALIGN,
        "tile_expert": tile_expert.astype(jnp.int32),
        "n_used": (region_end[-1:] // MOE_ROW_TILE).astype(jnp.int32),
        "off_lanes": off_lanes.reshape(counts.shape[0], 1, LANES),
    }


def _rope_tables(ctx_len, seq):
    rows = seq // GRID_W
    row = jnp.repeat(jnp.arange(rows, dtype=F32), GRID_W)
    col = jnp.tile(jnp.arange(GRID_W, dtype=F32), rows)
    inv = ROPE_BASE ** (-jnp.arange(ROPE_PAIRS_PER_AXIS, dtype=F32) / ROPE_PAIRS_PER_AXIS)
    ang = jnp.concatenate([row[:, None] * inv, col[:, None] * inv], axis=-1)
    cos, sin = jnp.cos(ang), jnp.sin(ang)
    cos_t = jnp.concatenate([cos, cos, cos, cos], axis=-1)
    sin_t = jnp.concatenate([-sin, sin, -sin, sin], axis=-1)
    cos_t = jnp.concatenate([jnp.ones((ctx_len, LANES), F32), cos_t], axis=0)
    sin_t = jnp.concatenate([jnp.zeros((ctx_len, LANES), F32), sin_t], axis=0)
    return cos_t, sin_t


def kernel(x, c, ctx, c_ctx, ada_w, ada_b, norm1_g, norm2_g, w_in, w_out, q_norm_g, k_norm_g, lambda_q1, lambda_k1, lambda_q2, lambda_k2, subln_g, conv_w, conv_b, gate_b, mnorm_g, ffn_w1, ffn_w3, ffn_w2, router_w, moe_w1, moe_w3, moe_w2):
    batch, seq, d = x.shape
    ctx_len = ctx.shape[1]
    depth = ada_w.shape[0]
    assert depth == 2, "layer 0 runs the dense FFN on all rows, layer 1 the experts on the latents"
    n_ctx_rows = batch * ctx_len
    n_ctx_tiles = n_ctx_rows // ROW_TILE
    tiles_per_seq = seq // ROW_TILE
    n_main = w_in.shape[2] - N_GATES
    assert n_main == 7 * A_WIDTH
    m_dh = (n_main - 3 * A_WIDTH) // 4 // M_HEADS
    gate_order = jnp.array(GATE_ORDER, jnp.int32)
    lane_pad = ((0, 0), (0, LANES - N_GATES))

    x_ctx = ctx.reshape(n_ctx_rows, d)
    x_lat = x.reshape(batch * seq, d)
    lat_first_tile = 0

    cc = jnp.zeros((2 * SUBLANES, d), F32).at[:batch].set(c).at[batch].set(c_ctx)
    mod_rows = adaln_all(cc, ada_w, ada_b)
    mods_all = jnp.concatenate([mod_rows[:, batch:batch + 1], mod_rows[:, :batch]], axis=1)
    mods_all = mods_all.reshape(depth, batch + 1, 6, d)

    cos_t, sin_t = _rope_tables(ctx_len, seq)

    for l in range(depth):
        last = l == depth - 1
        lam_init = 0.8 - 0.6 * math.exp(-0.3 * l)
        lam = (jnp.exp(jnp.sum(lambda_q1[l] * lambda_k1[l]).astype(F32))
               - jnp.exp(jnp.sum(lambda_q2[l] * lambda_k2[l]).astype(F32)) + lam_init)
        mods = mods_all[l]
        g_hi, g_lo = _split_bf16(w_in[l][:, n_main:][:, gate_order])
        g_hi, g_lo = jnp.pad(g_hi, lane_pad), jnp.pad(g_lo, lane_pad)
        w_all = jnp.concatenate([w_in[l, :, :n_main].astype(BF16), g_hi, g_lo], axis=1)
        qg = jnp.tile(q_norm_g[l] * (A_DH ** -0.5 * math.log2(math.e)), 2).reshape(1, LANES)
        kg = jnp.tile(k_norm_g[l], 2).reshape(1, LANES)

        qk, av, mq, mkt, mv, mo, g_row = proj_in(
            x_ctx, x_lat, lat_first_tile, mods, norm1_g[l].reshape(1, d), w_all, g_hi,
            conv_w[l], conv_b[l].reshape(1, -1), cos_t, sin_t, qg, kg, n_ctx_tiles, tiles_per_seq, m_dh ** -0.5)
        attn = functools.partial(attention, lam.reshape(1, 1), qk, av, subln_g[l].reshape(1, LANES),
                                 1.0 - lam_init, batch, ctx_len, seq)
        if l == 0:
            later = [w_out[0], w_out[1], ffn_w1[0], ffn_w3[0], ffn_w2[0], moe_w1[0], moe_w3[0], moe_w2[0]]
            views = [w.reshape(-1, w.shape[-1]) for w in later]
            views[4] = views[4].reshape(512, -1)
            a_out, *later_bf16 = attn(latent_queries=True, to_bf16=views)
            w_out_bf16 = later_bf16[:2]
            ffn_bf16 = [w.reshape(src.shape) for w, src in zip(later_bf16[2:5], later[2:5])]
            experts_bf16 = [w.reshape(src.shape) for w, src in zip(later_bf16[5:], later[5:])]
        else:
            a_out = attn(latent_queries=True)
        m_ctx, m_out = mlstm(mq, mkt, mv, mo, g_row, gate_b[l].reshape(-1)[gate_order].reshape(N_GATES, 1),
                             mnorm_g[l].reshape(1, -1), batch, ctx_len, seq)
        w_out_b = w_out_bf16[l]
        j = l // 2
        if not last:
            assert lat_first_tile == 0
            xs = mixer_ffn(a_out, attn(latent_queries=False), m_out, m_ctx, w_out_b, x_ctx, x_lat, mods,
                           norm2_g[l].reshape(1, d), *ffn_bf16, seq)
            x_ctx, x_lat, lat_first_tile = xs, xs, n_ctx_tiles
        else:
            x1, h2, sel, wcols, cnt = mixer_router(a_out, m_out, w_out_b, x_lat, lat_first_tile * ROW_TILE, mods,
                                                   norm2_g[l].reshape(1, d), router_w[j], seq)
            n_tok_tiles = x1.shape[0] // MOE_TOKEN_TILE
            counts = cnt[:, 0, :N_EXPERTS]
            worst = 2 * x1.shape[0] + n_tok_tiles * N_EXPERTS * (SLOT_ALIGN - 1) + N_EXPERTS * (MOE_ROW_TILE - SLOT_ALIGN)
            n_row_tiles = -(-worst // MOE_ROW_TILE)
            plan = _moe_plan(counts, n_row_tiles)
            slots = moe_dispatch(plan, h2, wcols, sel, n_row_tiles * MOE_ROW_TILE)
            ys = moe_experts(plan, slots, *experts_bf16)
            out = moe_combine(plan, ys, sel, x1, mods, seq)
    return out.reshape(batch, seq, d)
```

```python
import functools
import math

import jax
import jax.numpy as jnp
from jax import lax
from jax.experimental import pallas as pl
from jax.experimental.pallas import tpu as pltpu

F32 = jnp.float32
BF16 = jnp.bfloat16
HIGHEST = lax.Precision.HIGHEST

GRID_W = 64
A_HEADS = 4
A_DH = 64
A_DV = 2 * A_DH
A_WIDTH = A_HEADS * A_DV
M_HEADS = 4
N_GATES = 4 * M_HEADS
CONV_K = 5
ROPE_BASE = 10000.0
ROPE_PAIRS_PER_AXIS = A_DH // 4
N_EXPERTS = 8
EPS = 1e-6

LANES = 128
SUBLANES = 8
ROW_TILE = 256
MLSTM_CHUNK = 128
VMEM_LIMIT = 56 * 1024 * 1024

NT_DIMS = (((1,), (1,)), ((), ()))
TN_DIMS = (((0,), (0,)), ((), ()))


def _params(*sem):
    return pltpu.CompilerParams(dimension_semantics=sem, vmem_limit_bytes=VMEM_LIMIT)


def _silu(v):
    return v * jax.nn.sigmoid(v)


def _log_sigmoid(v):
    return jnp.minimum(v, 0.0) - jnp.log1p(jnp.exp(-jnp.abs(v)))


def _rms(v, axis=-1):
    return v * lax.rsqrt(jnp.mean(v * v, axis=axis, keepdims=True) + EPS)


def _split_bf16(w):
    hi = w.astype(BF16)
    return hi, (w - hi.astype(F32)).astype(BF16)


def _adaln_kernel(c_ref, w_ref, b_ref, o_ref):
    s = _silu(c_ref[...])
    o_ref[0] = jnp.dot(s, w_ref[0], precision=HIGHEST, preferred_element_type=F32) + b_ref[0]


def adaln_all(cc, ada_w, ada_b):
    depth, d, d6 = ada_w.shape
    n_col = d6 // d
    return pl.pallas_call(
        _adaln_kernel,
        grid=(depth, n_col),
        in_specs=[
            pl.BlockSpec(cc.shape, lambda l, j: (0, 0)),
            pl.BlockSpec((1, d, d), lambda l, j: (l, 0, j)),
            pl.BlockSpec((1, 1, d), lambda l, j: (l, 0, j)),
        ],
        out_specs=pl.BlockSpec((1, cc.shape[0], d), lambda l, j: (l, 0, j)),
        out_shape=jax.ShapeDtypeStruct((depth, cc.shape[0], d6), F32),
        compiler_params=_params("arbitrary", "arbitrary"),
        name="adaln",
    )(cc, ada_w, ada_b.reshape(depth, 1, d6))


def _proj_in_kernel(xa_ref, xb_ref, prev_ref, next_ref, mod_ref, g_ref, win_ref, cw_ref, cb_ref,
                    cos_ref, sin_ref, qg_ref, kg_ref,
                    qk_ref, v_ref, mq_ref, mkt_ref, mv_ref, mo_ref, gr_ref, ext_ref, w_ref,
                    *, n_ctx_tiles, tiles_per_seq, k_scale):
    i = pl.program_id(0)
    n_main = 7 * A_WIDTH

    @pl.when(i == 0)
    def _():
        w_ref[:, :n_main] = win_ref[0, :, :n_main].astype(BF16)
        w_ref[:, n_main:] = jnp.zeros((w_ref.shape[0], 2 * LANES), BF16)
        gate_hi, gate_lo = _split_bf16(win_ref[0, :, n_main:])
        w_ref[:, n_main:n_main + N_GATES] = gate_hi
        w_ref[:, n_main + LANES:n_main + LANES + N_GATES] = gate_lo

    is_ctx = i < n_ctx_tiles
    pos = (i - n_ctx_tiles) % tiles_per_seq
    first = jnp.logical_or(is_ctx, pos == 0)
    last = jnp.logical_or(is_ctx, pos == tiles_per_seq - 1)
    tm = xa_ref.shape[0]
    halo = SUBLANES

    x_ext = jnp.concatenate([prev_ref[...], jnp.where(is_ctx, xa_ref[...], xb_ref[...]), next_ref[...]], axis=0)
    h_ext = _rms(x_ext) * g_ref[...] * (1.0 + mod_ref[0, 1:2, :]) + mod_ref[0, 0:1, :]
    h = h_ext[halo:halo + tm]
    hb = h.astype(BF16)

    def proj(j):
        return jnp.dot(hb, w_ref[:, j * A_WIDTH:(j + 1) * A_WIDTH], preferred_element_type=F32)

    pre = jnp.dot(h_ext.astype(BF16), w_ref[:, 3 * A_WIDTH:5 * A_WIDTH], preferred_element_type=F32)
    ext_ref[0:halo, :] = jnp.where(first, 0.0, pre[0:halo])
    ext_ref[halo:halo + tm, :] = pre[halo:halo + tm]
    ext_ref[halo + tm:, :] = jnp.where(last, 0.0, pre[halo + tm:])

    r = lax.broadcasted_iota(jnp.int32, (LANES, LANES), 0)
    c = lax.broadcasted_iota(jnp.int32, (LANES, LANES), 1)
    group_ones = jnp.where((r & -A_DH) == (c & -A_DH), 1.0, 0.0).astype(BF16)
    lane = lax.broadcasted_iota(jnp.int32, (1, LANES), 1)
    first_half = (lane & (A_DH - 1)) < (A_DH // 2)
    cos = cos_ref[...]
    sin = sin_ref[...]
    for j, gain_ref in ((0, qg_ref), (1, kg_ref)):
        acc = proj(j)
        for hd in range(A_HEADS):
            t = acc[:, hd * LANES:(hd + 1) * LANES]
            ss = jnp.dot((t * t).astype(BF16), group_ones, preferred_element_type=F32)
            tn = t * lax.rsqrt(ss * (1.0 / A_DH) + EPS) * gain_ref[...]
            rot = jnp.where(first_half, pltpu.roll(tn, LANES - A_DH // 2, 1), pltpu.roll(tn, A_DH // 2, 1))
            qk_ref[:, j * A_WIDTH + hd * LANES: j * A_WIDTH + (hd + 1) * LANES] = (tn * cos + rot * sin).astype(BF16)
    def conv_silu(c0, c1):
        acc = jnp.zeros((tm, c1 - c0), F32) + cb_ref[:, c0:c1]
        pad = CONV_K // 2
        for j in range(CONV_K):
            acc = acc + cw_ref[j:j + 1, c0:c1] * ext_ref[halo - pad + j:halo - pad + j + tm, c0:c1]
        return _silu(acc)

    def store_k(c0, c1):
        yk = conv_silu(A_WIDTH + c0, A_WIDTH + c1) * k_scale
        for ci in range(tm // MLSTM_CHUNK):
            mkt_ref[ci, c0:c1, :] = yk[ci * MLSTM_CHUNK:(ci + 1) * MLSTM_CHUNK].T.astype(BF16)

    half = A_WIDTH // 2
    v_ref[...] = proj(2).astype(BF16)
    mq_ref[:, :half] = conv_silu(0, half).astype(BF16)
    mv_ref[...] = proj(5).astype(BF16)
    mq_ref[:, half:] = conv_silu(half, A_WIDTH).astype(BF16)
    mo_ref[...] = proj(6)
    store_k(0, half)

    p = jnp.dot(hb, w_ref[:, n_main:n_main + 2 * LANES], preferred_element_type=F32)
    h_lo = (h - hb.astype(F32)).astype(BF16)
    gates = p[:, :LANES] + p[:, LANES:] + jnp.dot(h_lo, w_ref[:, n_main:n_main + LANES], preferred_element_type=F32)
    gr_ref[...] = gates.T[:N_GATES, :]
    store_k(half, A_WIDTH)


def proj_in(xa, xb, b_first_tile, mods, norm_g, w_in, layer, conv_w, conv_b, cos_t, sin_t, qg, kg,
            n_ctx_tiles, tiles_per_seq, k_scale):
    d = xa.shape[1]
    n_lat_tiles = xb.shape[0] // ROW_TILE - b_first_tile
    n_tiles = n_ctx_tiles + n_lat_tiles
    rows = n_tiles * ROW_TILE
    per = ROW_TILE // SUBLANES

    def seg(i):
        return jnp.where(i < n_ctx_tiles, 0, 1 + (i - n_ctx_tiles) // tiles_per_seq)

    def rope_blk(i):
        return jnp.where(i < n_ctx_tiles, 0, 1 + (i - n_ctx_tiles) % tiles_per_seq)

    def lat(i):
        return jnp.maximum(i - n_ctx_tiles, 0)

    row = lambda i: (i, 0)
    const = lambda i: (0, 0)
    kern = functools.partial(_proj_in_kernel, n_ctx_tiles=n_ctx_tiles, tiles_per_seq=tiles_per_seq, k_scale=k_scale)
    return pl.pallas_call(
        kern,
        grid=(n_tiles,),
        in_specs=[
            pl.BlockSpec((ROW_TILE, d), lambda i: (jnp.minimum(i, n_ctx_tiles - 1), 0)),
            pl.BlockSpec((ROW_TILE, d), lambda i: (lat(i) + b_first_tile, 0)),
            pl.BlockSpec((SUBLANES, d), lambda i: (jnp.maximum(lat(i) * per - 1, 0) + b_first_tile * per, 0)),
            pl.BlockSpec((SUBLANES, d),
                         lambda i: (jnp.minimum((lat(i) + 1) * per, n_lat_tiles * per - 1) + b_first_tile * per, 0)),
            pl.BlockSpec((1, 6, d), lambda i: (seg(i), 0, 0)),
            pl.BlockSpec((1, d), const),
            pl.BlockSpec((1,) + w_in.shape[1:], lambda i: (layer, 0, 0), pipeline_mode=pl.Buffered(1)),
            pl.BlockSpec(conv_w.shape, const),
            pl.BlockSpec(conv_b.shape, const),
            pl.BlockSpec((ROW_TILE, LANES), lambda i: (rope_blk(i), 0)),
            pl.BlockSpec((ROW_TILE, LANES), lambda i: (rope_blk(i), 0)),
            pl.BlockSpec((1, LANES), const),
            pl.BlockSpec((1, LANES), const),
        ],
        out_specs=[
            pl.BlockSpec((ROW_TILE, 2 * A_WIDTH), row),
            pl.BlockSpec((ROW_TILE, A_WIDTH), row),
            pl.BlockSpec((ROW_TILE, A_WIDTH), row),
            pl.BlockSpec((ROW_TILE // MLSTM_CHUNK, A_WIDTH, MLSTM_CHUNK), lambda i: (i, 0, 0)),
            pl.BlockSpec((ROW_TILE, A_WIDTH), row),
            pl.BlockSpec((ROW_TILE, A_WIDTH), row),
            pl.BlockSpec((N_GATES, ROW_TILE), lambda i: (0, i)),
        ],
        out_shape=[
            jax.ShapeDtypeStruct((rows, 2 * A_WIDTH), BF16),
            jax.ShapeDtypeStruct((rows, A_WIDTH), BF16),
            jax.ShapeDtypeStruct((rows, A_WIDTH), BF16),
            jax.ShapeDtypeStruct((rows // MLSTM_CHUNK, A_WIDTH, MLSTM_CHUNK), BF16),
            jax.ShapeDtypeStruct((rows, A_WIDTH), BF16),
            jax.ShapeDtypeStruct((rows, A_WIDTH), F32),
            jax.ShapeDtypeStruct((N_GATES, rows), F32),
        ],
        scratch_shapes=[pltpu.VMEM((ROW_TILE + 2 * SUBLANES, 2 * A_WIDTH), F32),
                        pltpu.VMEM((d, 7 * A_WIDTH + 2 * LANES), BF16)],
        compiler_params=_params("arbitrary"),
        name="proj_in",
    )(xa, xb, xb, xb, mods, norm_g, w_in, conv_w, conv_b, cos_t, sin_t, qg, kg)


ATTN_KEY_BLOCK = 256


ATTN_SUB_TILE = 256
ATTN_Q_TILE = 2048


def _attn_kernel(lam_ref, q_ref, *refs, with_latent_keys, post_scale, n_cast):
    n_in = len(refs) - 2 * n_cast - 2
    cast_in, o_ref, cast_out, s_ref = refs[n_in:n_in + n_cast], refs[n_in + n_cast], refs[n_in + n_cast + 1:-1], refs[-1]
    if with_latent_keys:
        kc_ref, vc_ref, kl_ref, vl_ref, g_ref = refs[:n_in]
    else:
        kc_ref, vc_ref, g_ref = refs[:n_in]
    for src, dst in zip(cast_in, cast_out):
        dst[...] = src[...].astype(BF16)
    lam = lam_ref[0, 0]
    lane = lax.broadcasted_iota(jnp.int32, (1, LANES), 1)
    kb = ATTN_KEY_BLOCK
    ts = s_ref.shape[1]
    blocks = [(kc_ref, vc_ref, r0) for r0 in range(0, kc_ref.shape[0], kb)]
    if with_latent_keys:
        blocks += [(kl_ref, vl_ref, r0) for r0 in range(0, kl_ref.shape[0], kb)]

    units = [(u, m) for u in range(q_ref.shape[0] // ts) for m in range(2)]
    state = {}

    def pass1(n, j):
        u, m = units[n]
        if j == 0:
            q = q_ref[u * ts:(u + 1) * ts, :]
            keep = (lane < A_DH) if m == 0 else (lane >= A_DH)
            state[n] = {"q": jnp.where(keep, q, jnp.zeros_like(q)), "max": jnp.full((ts, LANES), -jnp.inf, F32)}
        k_ref, _, r0 = blocks[j]
        s = lax.dot_general(state[n]["q"], k_ref[r0:r0 + kb, :], NT_DIMS, preferred_element_type=F32)
        s_ref[n % 2, :, j * kb:(j + 1) * kb] = s
        for c0 in range(0, kb, LANES):
            state[n]["max"] = jnp.maximum(state[n]["max"], s[:, c0:c0 + LANES])

    ones = jnp.ones((kb, LANES), BF16)

    def pass2(n, j):
        st = state[n]
        if j == 0:
            st["mx"] = jnp.broadcast_to(jnp.max(st["max"], axis=1, keepdims=True), (ts, kb))
            st["acc"] = jnp.zeros((ts, 2 * LANES), F32)
        _, v_ref, r0 = blocks[j]
        e = jnp.exp2(s_ref[n % 2, :, j * kb:(j + 1) * kb] - st["mx"])
        v_ext = jnp.concatenate([v_ref[r0:r0 + kb, :], ones], axis=1)
        st["acc"] = st["acc"] + jnp.dot(e.astype(BF16), v_ext, preferred_element_type=F32)

    def finish(u):
        first, second = state.pop(2 * u)["acc"], state.pop(2 * u + 1)["acc"]
        out = first[:, :LANES] / first[:, LANES:] - second[:, :LANES] * (lam / second[:, LANES:])
        o_ref[u * ts:(u + 1) * ts, :] = (_rms(out) * g_ref[...] * post_scale).astype(BF16)

    for n in range(len(units) + 1):
        for j in range(len(blocks)):
            if n < len(units):
                pass1(n, j)
            if n >= 1:
                pass2(n - 1, j)
        if n >= 1 and units[n - 1][1] == 1:
            finish(units[n - 1][0])


def attention(lam, qk, v, subln_g, post_scale, batch, ctx_len, seq, latent_queries, to_bf16=()):
    lat_blk0 = batch * ctx_len // seq
    tq = ATTN_Q_TILE if latent_queries else ctx_len
    nq = seq // tq if latent_queries else 1
    q0 = batch * ctx_len // tq if latent_queries else 0
    q_per_batch = nq
    n_keys = ctx_len + (seq if latent_queries else 0)
    in_specs = [
        pl.BlockSpec(memory_space=pltpu.SMEM),
        pl.BlockSpec((tq, LANES), lambda b, h, t: (q0 + b * q_per_batch + t, h)),
        pl.BlockSpec((ctx_len, LANES), lambda b, h, t: (b, A_HEADS + h)),
        pl.BlockSpec((ctx_len, LANES), lambda b, h, t: (b, h)),
    ]
    args = [lam, qk, qk, v]
    if latent_queries:
        in_specs += [pl.BlockSpec((seq, LANES), lambda b, h, t: (lat_blk0 + b, A_HEADS + h)),
                     pl.BlockSpec((seq, LANES), lambda b, h, t: (lat_blk0 + b, h))]
        args += [qk, v]
    in_specs.append(pl.BlockSpec((1, LANES), lambda b, h, t: (0, 0)))
    args.append(subln_g)
    out_specs = [pl.BlockSpec((tq, LANES), lambda b, h, t: (b * q_per_batch + t, h))]
    out_shape = [jax.ShapeDtypeStruct((batch * nq * tq, A_WIDTH), BF16)]
    n_steps = batch * A_HEADS * nq
    for w in to_bf16:
        blk = pl.BlockSpec((w.shape[0] // n_steps, w.shape[1]), lambda b, h, t: ((b * A_HEADS + h) * nq + t, 0))
        in_specs.append(blk)
        args.append(w)
        out_specs.append(blk)
        out_shape.append(jax.ShapeDtypeStruct(w.shape, BF16))
    kern = functools.partial(_attn_kernel, with_latent_keys=latent_queries, post_scale=post_scale,
                             n_cast=len(to_bf16))
    outs = pl.pallas_call(
        kern,
        grid=(batch, A_HEADS, nq),
        in_specs=in_specs,
        out_specs=out_specs,
        out_shape=out_shape,
        scratch_shapes=[pltpu.VMEM((2, min(tq, ATTN_SUB_TILE), n_keys), F32)],
        compiler_params=_params("arbitrary", "arbitrary", "arbitrary"),
        name="diff_attention",
    )(*args)
    return outs if to_bf16 else outs[0]


def _mlstm_kernel(qc_ref, ql_ref, ktc_ref, ktl_ref, vc_ref, vl_ref, oc_ref, ol_ref, grc_ref, grl_ref, gb_ref, mg_ref,
                  outc_ref, outl_ref, row_ref, col_ref, s_ref, hf_ref, hb_ref):
    L = MLSTM_CHUNK
    dh = LANES
    n_ch = 2 * M_HEADS
    n_ctx = qc_ref.shape[0] // L
    n_lat = ql_ref.shape[0] // L
    n_chunks = n_ctx + n_lat
    nt = n_chunks * L

    g = jnp.concatenate([grc_ref[...], grl_ref[...]], axis=1) + gb_ref[...]
    hh = M_HEADS
    log_i = jnp.concatenate([g[0:hh], g[2 * hh:3 * hh]], axis=0)
    log_f = _log_sigmoid(jnp.concatenate([g[hh:2 * hh], g[3 * hh:4 * hh]], axis=0))
    sub = lax.broadcasted_iota(jnp.int32, (n_ch, nt), 0)
    lane_in_chunk = lax.broadcasted_iota(jnp.int32, (n_ch, nt), 1) & (L - 1)
    fwd_rows = sub < M_HEADS

    def chunk_scan(x, op, ident):
        xf, xb, k = x, x, 1
        while k < L:
            xf = op(xf, jnp.where(lane_in_chunk >= k, pltpu.roll(xf, k, 1), ident))
            xb = op(xb, jnp.where(lane_in_chunk < L - k, pltpu.roll(xb, nt - k, 1), ident))
            k *= 2
        return jnp.where(fwd_rows, xf, xb)

    b = chunk_scan(log_f, jnp.add, 0.0)
    a = log_i - b
    a_run = chunk_scan(a, jnp.maximum, -jnp.inf)

    fwd_col = lax.broadcasted_iota(jnp.int32, (n_ch, 1), 0) < M_HEADS

    def at_chunk_end(x, ci):
        xc = x[:, ci * L:(ci + 1) * L]
        return jnp.where(fwd_col, xc[:, L - 1:L], xc[:, 0:1])

    b_last = [at_chunk_end(b, ci) for ci in range(n_chunks)]
    a_last = [at_chunk_end(a_run, ci) for ci in range(n_chunks)]
    order_f = list(range(n_chunks))
    order_b = list(range(n_ctx - 1, -1, -1)) + list(range(n_chunks - 1, n_ctx - 1, -1))

    def carried(order):
        m, out = jnp.zeros((n_ch, 1), F32), {}
        for ci in order:
            out[ci] = m
            m = b_last[ci] + jnp.maximum(m, a_last[ci])
        return out

    m_f, m_b = carried(order_f), carried(order_b)
    for ci in range(n_chunks):
        m_prev = jnp.where(fwd_col, m_f[ci], m_b[ci])
        sl = slice(ci * L, (ci + 1) * L)
        mu = jnp.maximum(m_prev, a_run[:, sl])
        mu_last = jnp.maximum(m_prev, a_last[ci])
        row_ref[ci] = jnp.concatenate([a[:, sl], jnp.exp(a[:, sl] - mu_last), jnp.broadcast_to(m_prev, (n_ch, L)),
                                       jnp.broadcast_to(jnp.exp(m_prev - mu_last), (n_ch, L))], axis=0)
        tile = jnp.concatenate([mu, b[:, sl] + mu, jnp.zeros((L - 2 * n_ch, L), F32)], axis=0)
        col_ref[ci] = tile.T

    s_ref[...] = jnp.zeros(s_ref.shape, F32)
    rr = lax.broadcasted_iota(jnp.int32, (L, L), 0)
    cc = lax.broadcasted_iota(jnp.int32, (L, L), 1)
    visible = (cc <= rr, cc >= rr)
    ones = jnp.ones((L, dh), BF16)

    def chunk_rows(ci):
        return pl.ds(ci * L if isinstance(ci, int) else pl.multiple_of(ci * L, L), L)

    def twice(t):
        return jnp.concatenate([t, t], axis=1)

    def step(local, glob, q_ref, kt_ref, v_ref):
        for d in range(2):
            rows = chunk_rows(local[d])
            grow = chunk_rows(glob[d])
            cs = col_ref[glob[d]]
            rs = row_ref[glob[d]]
            h_ref = hf_ref if d == 0 else hb_ref
            for hd in range(M_HEADS):
                r = d * M_HEADS + hd
                mu = jnp.broadcast_to(cs[:, r:r + 1], (L, L))
                m_t = jnp.broadcast_to(cs[:, n_ch + r:n_ch + r + 1], (L, dh))
                a_row, w_row, m_prev, decay = (rs[j * n_ch + r:j * n_ch + r + 1, :] for j in range(4))
                w_intra = jnp.exp(jnp.where(visible[d], a_row - mu, -jnp.inf))
                w_inter = jnp.exp(m_prev - mu)
                cols = slice(hd * dh, (hd + 1) * dh)
                q = q_ref[rows, cols]
                kt = kt_ref[local[d], cols, :]
                v_ext = jnp.concatenate([v_ref[rows, cols], ones], axis=1)
                state = s_ref[r]
                qa = jnp.dot(q, jnp.concatenate([kt, state.astype(BF16)], axis=1), preferred_element_type=F32)
                s = (qa[:, :L] * w_intra).astype(BF16)
                ktw = (kt.astype(F32) * w_row).astype(BF16)
                sv = jnp.dot(jnp.concatenate([s, ktw], axis=0), v_ext, preferred_element_type=F32)
                t = sv[:L] + twice(w_inter) * qa[:, L:]
                h_ref[grow, cols] = t[:, :dh] / jnp.maximum(jnp.abs(t[:, dh:]), jnp.exp(-m_t))
                s_ref[r] = twice(jnp.broadcast_to(decay, (dh, L))) * state + sv[L:]

    def finish(local, glob, o_ref, out_ref):
        rows = chunk_rows(local)
        grow = chunk_rows(glob)
        for hd in range(M_HEADS):
            cols = slice(hd * dh, (hd + 1) * dh)
            hn = _rms(hf_ref[grow, cols] + hb_ref[grow, cols]) * mg_ref[:, cols]
            out_ref[rows, cols] = (jax.nn.sigmoid(o_ref[rows, cols]) * hn).astype(BF16)

    for i in range(n_ctx):
        step((i, n_ctx - 1 - i), (i, n_ctx - 1 - i), qc_ref, ktc_ref, vc_ref)
    for i in range(n_ctx):
        finish(i, i, oc_ref, outc_ref)

    def lat_step(i, carry):
        step((i, n_lat - 1 - i), (n_ctx + i, n_chunks - 1 - i), ql_ref, ktl_ref, vl_ref)
        return carry

    def lat_step_and_finish(i, carry):
        lat_step(i, carry)
        finish(i, n_ctx + i, ol_ref, outl_ref)
        finish(n_lat - 1 - i, n_chunks - 1 - i, ol_ref, outl_ref)
        return carry

    assert n_lat % 2 == 0
    lax.fori_loop(0, n_lat // 2, lat_step, 0)
    lax.fori_loop(n_lat // 2, n_lat, lat_step_and_finish, 0)


def mlstm(mq, mkt, mv, mo, g_row, gate_b, mnorm_g, batch, ctx_len, seq):
    rows, width = mq.shape
    assert MLSTM_CHUNK == LANES and width == M_HEADS * LANES
    lat0 = batch * ctx_len // seq
    n_chunks = (ctx_len + seq) // MLSTM_CHUNK
    ctx = lambda b: (b, 0)
    lat = lambda b: (lat0 + b, 0)
    const = lambda b: (0, 0)
    out_c, out_l = pl.pallas_call(
        _mlstm_kernel,
        grid=(batch,),
        in_specs=[
            pl.BlockSpec((ctx_len, width), ctx), pl.BlockSpec((seq, width), lat),
            pl.BlockSpec((ctx_len // MLSTM_CHUNK, width, MLSTM_CHUNK), lambda b: (b, 0, 0)),
            pl.BlockSpec((seq // MLSTM_CHUNK, width, MLSTM_CHUNK), lambda b: (lat0 + b, 0, 0)),
            pl.BlockSpec((ctx_len, width), ctx), pl.BlockSpec((seq, width), lat),
            pl.BlockSpec((ctx_len, width), ctx), pl.BlockSpec((seq, width), lat),
            pl.BlockSpec((N_GATES, ctx_len), lambda b: (0, b)), pl.BlockSpec((N_GATES, seq), lambda b: (0, lat0 + b)),
            pl.BlockSpec((N_GATES, 1), const),
            pl.BlockSpec((1, width), const),
        ],
        out_specs=[pl.BlockSpec((ctx_len, width), ctx), pl.BlockSpec((seq, width), ctx)],
        out_shape=[jax.ShapeDtypeStruct((batch * ctx_len, width), BF16),
                   jax.ShapeDtypeStruct((batch * seq, width), BF16)],
        scratch_shapes=[
            pltpu.VMEM((n_chunks, 8 * M_HEADS, MLSTM_CHUNK), F32),
            pltpu.VMEM((n_chunks, MLSTM_CHUNK, MLSTM_CHUNK), F32),
            pltpu.VMEM((2 * M_HEADS, LANES, 2 * LANES), F32),
            pltpu.VMEM((ctx_len + seq, width), F32),
            pltpu.VMEM((ctx_len + seq, width), F32),
        ],
        compiler_params=_params("arbitrary"),
        name="mlstm",
    )(mq, mq, mkt, mkt, mv, mv, mo, mo, g_row, g_row, gate_b, mnorm_g)
    return out_c, out_l


def _router_top2(logits):
    lane = lax.broadcasted_iota(jnp.int32, logits.shape, 1)
    logits = jnp.where(lane < N_EXPERTS, logits, -jnp.inf)
    v1 = jnp.max(logits, axis=1, keepdims=True)
    i1 = jnp.min(jnp.where(logits == v1, lane, LANES), axis=1, keepdims=True)
    rest = jnp.where(lane == i1, -jnp.inf, logits)
    v2 = jnp.max(rest, axis=1, keepdims=True)
    i2 = jnp.min(jnp.where(rest == v2, lane, LANES), axis=1, keepdims=True)
    e = jnp.exp(v2 - v1)
    w1 = 1.0 / (1.0 + e)
    w2 = e / (1.0 + e)
    mask = jnp.where(jnp.logical_or(lane == i1, lane == i2), 1.0, 0.0)
    expert = lane & (N_EXPERTS - 1)
    comb = jnp.where(lane < 3 * N_EXPERTS,
                     jnp.where(expert == i1, w1, 0.0) + jnp.where(expert == i2, w2, 0.0), 0.0)
    hi = comb.astype(BF16)
    rem = comb - hi.astype(F32)
    mid = rem.astype(BF16)
    lo = (rem - mid.astype(F32)).astype(BF16)
    wcols = jnp.where(lane < N_EXPERTS, hi, jnp.where(lane < 2 * N_EXPERTS, mid, lo))
    return mask, wcols


def _mixer_router_kernel(a_ref, m_ref, w_ref, x_ref, mod_ref, g_ref, rw_ref, x1_ref, h2_ref, sel_ref, wc_ref, cnt_ref):
    half = a_ref.shape[1]
    mix = (jnp.dot(a_ref[...], w_ref[:half, :], preferred_element_type=F32)
           + jnp.dot(m_ref[...], w_ref[half:, :], preferred_element_type=F32))
    x1 = x_ref[...] + mod_ref[0, 2:3, :] * mix
    x1_ref[...] = x1
    h2 = _rms(x1) * g_ref[...] * (1.0 + mod_ref[0, 4:5, :]) + mod_ref[0, 3:4, :]
    hb = h2.astype(BF16)
    h2_ref[...] = hb
    p = jnp.dot(hb, rw_ref[...], preferred_element_type=F32)
    h_lo = (h2 - hb.astype(F32)).astype(BF16)
    logits = p[:, :LANES] + p[:, LANES:] + jnp.dot(h_lo, rw_ref[:, :LANES], preferred_element_type=F32)
    mask, wcols = _router_top2(logits)
    sel_ref[...] = mask.astype(BF16)
    wc_ref[...] = wcols
    cnt_ref[0] = jnp.sum(mask, axis=0, keepdims=True)


def mixer_router(a_lat, m_lat, w_out, xs, first_row, mods, norm_g, router_w, rows_per_seq):
    d = xs.shape[1]
    half = a_lat.shape[1]
    tm = MOE_TOKEN_TILE
    n_tiles = a_lat.shape[0] // tm
    first_tile = first_row // tm
    dst = lambda i: (i, 0)
    const = lambda i: (0, 0)
    pad = ((0, 0), (0, LANES - router_w.shape[1]))
    rw = jnp.concatenate([jnp.pad(t, pad) for t in _split_bf16(router_w)], axis=1)
    return pl.pallas_call(
        _mixer_router_kernel,
        grid=(n_tiles,),
        in_specs=[
            pl.BlockSpec((tm, half), dst), pl.BlockSpec((tm, half), dst),
            pl.BlockSpec(w_out.shape, const),
            pl.BlockSpec((tm, d), lambda i: (i + first_tile, 0)),
            pl.BlockSpec((1, 6, d), lambda i: (1 + (i * tm) // rows_per_seq, 0, 0)),
            pl.BlockSpec((1, d), const),
            pl.BlockSpec(rw.shape, const),
        ],
        out_specs=[pl.BlockSpec((tm, d), dst), pl.BlockSpec((tm, d), dst),
                   pl.BlockSpec((tm, LANES), dst), pl.BlockSpec((tm, LANES), dst),
                   pl.BlockSpec((1, 1, LANES), lambda i: (i, 0, 0))],
        out_shape=[jax.ShapeDtypeStruct((n_tiles * tm, d), F32),
                   jax.ShapeDtypeStruct((n_tiles * tm, d), BF16),
                   jax.ShapeDtypeStruct((n_tiles * tm, LANES), BF16),
                   jax.ShapeDtypeStruct((n_tiles * tm, LANES), BF16),
                   jax.ShapeDtypeStruct((n_tiles, 1, LANES), F32)],
        compiler_params=_params("arbitrary"),
        name="mixer_router",
    )(a_lat, m_lat, w_out, xs, mods, norm_g, rw)


FFN_ROW_TILE = 512
FFN_FF_TILE = 1408


def _mixer_ffn_kernel(a_ref, ac_ref, m_ref, mc_ref, wo_ref, xc_ref, x_ref, mod_ref, g_ref, w1_ref, w3_ref, w2_ref,
                      o_ref, x1_ref, h2_ref, acc_ref, *, n_ctx_tiles):
    f = pl.program_id(1)

    @pl.when(f == 0)
    def _():
        is_ctx = pl.program_id(0) < n_ctx_tiles
        a = jnp.where(is_ctx, ac_ref[...], a_ref[...])
        m = jnp.where(is_ctx, mc_ref[...], m_ref[...])
        x = jnp.where(is_ctx, xc_ref[...], x_ref[...])
        half = a.shape[1]
        mix = (jnp.dot(a, wo_ref[:half, :], preferred_element_type=F32)
               + jnp.dot(m, wo_ref[half:, :], preferred_element_type=F32))
        x1 = x + mod_ref[0, 2:3, :] * mix
        x1_ref[...] = x1
        h2_ref[...] = (_rms(x1) * g_ref[...] * (1.0 + mod_ref[0, 4:5, :]) + mod_ref[0, 3:4, :]).astype(BF16)
        acc_ref[...] = jnp.zeros(acc_ref.shape, F32)

    h = h2_ref[...]
    a = _silu(jnp.dot(h, w1_ref[...], preferred_element_type=F32)) * jnp.dot(h, w3_ref[...], preferred_element_type=F32)
    acc_ref[...] += jnp.dot(a.astype(BF16), w2_ref[...], preferred_element_type=F32)

    @pl.when(f == pl.num_programs(1) - 1)
    def _():
        o_ref[...] = x1_ref[...] + mod_ref[0, 5:6, :] * acc_ref[...]


def mixer_ffn(a_lat, a_ctx, m_lat, m_ctx, w_out, x_ctx, x_lat, mods, norm_g, w1, w3, w2, rows_per_seq):
    d = x_lat.shape[1]
    half = a_lat.shape[1]
    dff = w1.shape[1]
    tm, tf = FFN_ROW_TILE, FFN_FF_TILE
    n_ctx_tiles = x_ctx.shape[0] // tm
    n_tiles = n_ctx_tiles + x_lat.shape[0] // tm
    per_seq = rows_per_seq // tm
    ctx = lambda i, f: (jnp.minimum(i, n_ctx_tiles - 1), 0)
    lat = lambda i, f: (jnp.maximum(i - n_ctx_tiles, 0), 0)
    const = lambda i, f: (0, 0)

    def seg(i, f):
        return (jnp.where(i < n_ctx_tiles, 0, 1 + (i - n_ctx_tiles) // per_seq), 0, 0)

    return pl.pallas_call(
        functools.partial(_mixer_ffn_kernel, n_ctx_tiles=n_ctx_tiles),
        grid=(n_tiles, dff // tf),
        in_specs=[
            pl.BlockSpec((tm, half), lat), pl.BlockSpec((tm, half), ctx),
            pl.BlockSpec((tm, half), lat), pl.BlockSpec((tm, half), ctx),
            pl.BlockSpec(w_out.shape, const),
            pl.BlockSpec((tm, d), ctx), pl.BlockSpec((tm, d), lat),
            pl.BlockSpec((1, 6, d), seg),
            pl.BlockSpec((1, d), const),
            pl.BlockSpec((d, tf), lambda i, f: (0, f)),
            pl.BlockSpec((d, tf), lambda i, f: (0, f)),
            pl.BlockSpec((tf, d), lambda i, f: (f, 0)),
        ],
        out_specs=pl.BlockSpec((tm, d), lambda i, f: (i, 0)),
        out_shape=jax.ShapeDtypeStruct((n_tiles * tm, d), F32),
        scratch_shapes=[pltpu.VMEM((tm, d), F32), pltpu.VMEM((tm, d), BF16), pltpu.VMEM((tm, d), F32)],
        compiler_params=_params("arbitrary", "arbitrary"),
        name="mixer_ffn",
    )(a_lat, a_ctx, m_lat, m_ctx, w_out, x_ctx, x_lat, mods, norm_g, w1, w3, w2)


MOE_TOKEN_TILE = 512
MOE_ROW_TILE = 512
MOE_FF_TILE = 1792
SLOT_ALIGN = 16
STAGE_ROWS = 2 * MOE_TOKEN_TILE + LANES


def _pow2_chunks(max_chunks):
    k = 1
    while k * 2 <= max_chunks:
        k *= 2
    while k >= 1:
        yield k
        k //= 2


def _segment_copies(n_chunks, src, dst, src_ref, dst_ref, sem, max_chunks, action):
    for k in _pow2_chunks(max_chunks):
        rows = k * SLOT_ALIGN
        taken = (n_chunks & ~(2 * k - 1)) * SLOT_ALIGN

        @pl.when((n_chunks & k) != 0)
        def _(rows=rows, taken=taken):
            s = pl.multiple_of(src + taken, SLOT_ALIGN)
            t = pl.multiple_of(dst + taken, SLOT_ALIGN)
            cp = pltpu.make_async_copy(src_ref.at[pl.ds(s, rows), :], dst_ref.at[pl.ds(t, rows), :], sem)
            if action == "start":
                cp.start()
            else:
                cp.wait()


def _slot_one_hot(sel_ref, off_ref):
    sel = sel_ref[...]
    tt = sel.shape[0]
    r = lax.broadcasted_iota(jnp.int32, (tt, tt), 0)
    c = lax.broadcasted_iota(jnp.int32, (tt, tt), 1)
    earlier = jnp.where(c < r, 1.0, 0.0).astype(BF16)
    rank = jnp.dot(earlier, sel, preferred_element_type=F32)
    pos = off_ref[0] + rank
    chosen = sel > 0
    pa = jnp.min(jnp.where(chosen, pos, float(STAGE_ROWS)), axis=1, keepdims=True)
    pb = jnp.max(jnp.where(chosen, pos, -1.0), axis=1, keepdims=True)
    slot = lax.broadcasted_iota(jnp.int32, (tt, STAGE_ROWS), 1).astype(F32)
    return jnp.where(jnp.logical_or(slot == pa, slot == pb), 1.0, 0.0).astype(BF16)


def _dispatch_kernel(seg_start_ref, seg_chunks_ref, tile_off_ref, tail_start_ref, tail_chunks_ref, n_used_ref,
                     h_ref, wc_ref, sel_ref, off_ref, xs_ref, stage_ref, zero_ref, sem):
    t = pl.program_id(0)
    last = pl.num_programs(0) - 1
    buf = t % 2
    max_chunks = MOE_TOKEN_TILE // SLOT_ALIGN

    def tile_copies(tile, which, action):
        for e in range(N_EXPERTS):
            idx = tile * N_EXPERTS + e
            _segment_copies(seg_chunks_ref[idx], tile_off_ref[idx], seg_start_ref[idx],
                            stage_ref.at[which], xs_ref, sem.at[which], max_chunks, action)

    one_hot = _slot_one_hot(sel_ref, off_ref)
    payload = jnp.concatenate([h_ref[...], wc_ref[...]], axis=1)
    stage_ref[buf] = lax.dot_general(one_hot, payload, TN_DIMS, preferred_element_type=F32).astype(BF16)
    tile_copies(t, buf, "start")

    @pl.when(t > 0)
    def _():
        tile_copies(t - 1, 1 - buf, "wait")

    @pl.when(t == last)
    def _():
        tile_copies(t, buf, "wait")
        zero_ref[...] = jnp.zeros(zero_ref.shape, BF16)
        for action in ("start", "wait"):
            for e in range(N_EXPERTS):
                _segment_copies(tail_chunks_ref[e], 0, tail_start_ref[e], zero_ref, xs_ref, sem.at[0],
                                MOE_ROW_TILE // SLOT_ALIGN - 1, action)

        def clear_tile(j, carry):
            rows = pl.ds(pl.multiple_of(j * MOE_ROW_TILE, MOE_ROW_TILE), MOE_ROW_TILE)
            cp = pltpu.make_async_copy(zero_ref, xs_ref.at[rows, :], sem.at[0])
            cp.start()
            cp.wait()
            return carry

        lax.fori_loop(n_used_ref[0], xs_ref.shape[0] // MOE_ROW_TILE, clear_tile, 0)


def moe_dispatch(plan, h2, wcols, sel, n_slots):
    rows, d = h2.shape
    tt = MOE_TOKEN_TILE
    width = d + LANES
    tile = lambda t, *_: (t, 0)
    grid_spec = pltpu.PrefetchScalarGridSpec(
        num_scalar_prefetch=6,
        grid=(rows // tt,),
        in_specs=[
            pl.BlockSpec((tt, d), tile),
            pl.BlockSpec((tt, LANES), tile),
            pl.BlockSpec((tt, LANES), tile),
            pl.BlockSpec((1, 1, LANES), lambda t, *_: (t, 0, 0)),
        ],
        out_specs=pl.BlockSpec(memory_space=pl.ANY),
        scratch_shapes=[pltpu.VMEM((2, STAGE_ROWS, width), BF16),
                        pltpu.VMEM((MOE_ROW_TILE, width), BF16),
                        pltpu.SemaphoreType.DMA((2,))],
    )
    return pl.pallas_call(
        _dispatch_kernel,
        grid_spec=grid_spec,
        out_shape=jax.ShapeDtypeStruct((n_slots, width), BF16),
        compiler_params=_params("arbitrary"),
        name="moe_dispatch",
    )(plan["seg_start"], plan["seg_chunks"], plan["tile_off"], plan["tail_start"], plan["tail_chunks"],
      plan["n_used"], h2, wcols, sel, plan["off_lanes"])


def _moe_kernel(tile_expert_ref, n_used_ref, xs_ref, w1_ref, w3_ref, w2_ref, ys_ref, acc_ref):
    i = pl.program_id(0)
    f = pl.program_id(1)

    @pl.when(i < n_used_ref[0])
    def _():
        @pl.when(f == 0)
        def _():
            acc_ref[...] = jnp.zeros(acc_ref.shape, F32)

        d = w1_ref.shape[1]
        x = xs_ref[:, :d]
        wc = xs_ref[:, d:].astype(F32)
        lane = lax.broadcasted_iota(jnp.int32, wc.shape, 1)
        mine = jnp.logical_and((lane & (N_EXPERTS - 1)) == tile_expert_ref[i], lane < 3 * N_EXPERTS)
        wt = jnp.sum(jnp.where(mine, wc, 0.0), axis=1, keepdims=True)
        a = _silu(jnp.dot(x, w1_ref[0], preferred_element_type=F32)) * jnp.dot(x, w3_ref[0], preferred_element_type=F32)
        acc_ref[...] += jnp.dot((a * wt).astype(BF16), w2_ref[0], preferred_element_type=F32)

        @pl.when(f == pl.num_programs(1) - 1)
        def _():
            ys_ref[...] = acc_ref[...].astype(BF16)

    @pl.when(jnp.logical_and(i >= n_used_ref[0], f == 0))
    def _():
        ys_ref[...] = jnp.zeros(ys_ref.shape, BF16)


def moe_experts(plan, xs, w1, w3, w2):
    n_slots, width = xs.shape
    ne, d, dff = w1.shape
    tm, tf = MOE_ROW_TILE, MOE_FF_TILE
    n_f = dff // tf

    def row(i, f, te, nu):
        return jnp.minimum(i, nu[0] - 1)

    def ff(i, f, te, nu):
        return jnp.where(i < nu[0], f, n_f - 1)

    grid_spec = pltpu.PrefetchScalarGridSpec(
        num_scalar_prefetch=2,
        grid=(n_slots // tm, n_f),
        in_specs=[
            pl.BlockSpec((tm, width), lambda i, f, te, nu: (row(i, f, te, nu), 0)),
            pl.BlockSpec((1, d, tf), lambda i, f, te, nu: (te[row(i, f, te, nu)], 0, ff(i, f, te, nu))),
            pl.BlockSpec((1, d, tf), lambda i, f, te, nu: (te[row(i, f, te, nu)], 0, ff(i, f, te, nu))),
            pl.BlockSpec((1, tf, d), lambda i, f, te, nu: (te[row(i, f, te, nu)], ff(i, f, te, nu), 0)),
        ],
        out_specs=pl.BlockSpec((tm, d), lambda i, f, te, nu: (i, 0)),
        scratch_shapes=[pltpu.VMEM((tm, d), F32)],
    )
    return pl.pallas_call(
        _moe_kernel,
        grid_spec=grid_spec,
        out_shape=jax.ShapeDtypeStruct((n_slots, d), BF16),
        compiler_params=_params("arbitrary", "arbitrary"),
        name="moe_experts",
    )(plan["tile_expert"], plan["n_used"], xs, w1, w3, w2)


def _combine_kernel(seg_start_ref, seg_chunks_ref, tile_off_ref,
                    ys_ref, sel_ref, off_ref, x_ref, mod_ref, o_ref, ybuf_ref, sem):
    t = pl.program_id(0)
    buf = t % 2
    max_chunks = MOE_TOKEN_TILE // SLOT_ALIGN

    def tile_copies(tile, which, action):
        for e in range(N_EXPERTS):
            idx = tile * N_EXPERTS + e
            _segment_copies(seg_chunks_ref[idx], seg_start_ref[idx], tile_off_ref[idx],
                            ys_ref, ybuf_ref.at[which], sem.at[which], max_chunks, action)

    @pl.when(t == 0)
    def _():
        ybuf_ref[...] = jnp.zeros(ybuf_ref.shape, BF16)
        tile_copies(t, buf, "start")

    @pl.when(t + 1 < pl.num_programs(0))
    def _():
        tile_copies(t + 1, 1 - buf, "start")

    one_hot = _slot_one_hot(sel_ref, off_ref)
    tile_copies(t, buf, "wait")
    y = jnp.dot(one_hot, ybuf_ref[buf], preferred_element_type=F32)
    o_ref[...] = x_ref[...] + mod_ref[0, 5:6, :] * y


def moe_combine(plan, ys, sel, x1, mods, rows_per_seq):
    rows, d = x1.shape
    tt = MOE_TOKEN_TILE
    tile = lambda t, *_: (t, 0)
    grid_spec = pltpu.PrefetchScalarGridSpec(
        num_scalar_prefetch=3,
        grid=(rows // tt,),
        in_specs=[
            pl.BlockSpec(memory_space=pl.ANY),
            pl.BlockSpec((tt, LANES), tile),
            pl.BlockSpec((1, 1, LANES), lambda t, *_: (t, 0, 0)),
            pl.BlockSpec((tt, d), tile),
            pl.BlockSpec((1, 6, d), lambda t, *_: (1 + (t * tt) // rows_per_seq, 0, 0)),
        ],
        out_specs=pl.BlockSpec((tt, d), tile),
        scratch_shapes=[pltpu.VMEM((2, STAGE_ROWS, d), BF16), pltpu.SemaphoreType.DMA((2,))],
    )
    return pl.pallas_call(
        _combine_kernel,
        grid_spec=grid_spec,
        out_shape=jax.ShapeDtypeStruct((rows, d), F32),
        compiler_params=_params("arbitrary"),
        name="moe_combine",
    )(plan["seg_start"], plan["seg_chunks"], plan["tile_off"], ys, sel, plan["off_lanes"], x1, mods)


def _moe_plan(counts, n_row_tiles):
    counts = counts.astype(jnp.int32)
    seg = (counts + SLOT_ALIGN - 1) // SLOT_ALIGN * SLOT_ALIGN
    tile_off = jnp.cumsum(seg, axis=1) - seg
    total = jnp.sum(seg, axis=0)
    region = (total + MOE_ROW_TILE - 1) // MOE_ROW_TILE * MOE_ROW_TILE
    region_end = jnp.cumsum(region)
    region_start = region_end - region
    seg_start = region_start[None, :] + jnp.cumsum(seg, axis=0) - seg
    first_row = jnp.arange(n_row_tiles, dtype=jnp.int32) * MOE_ROW_TILE
    tile_expert = jnp.minimum(jnp.sum(first_row[:, None] >= region_end[None, :], axis=1), counts.shape[1] - 1)
    off_lanes = jnp.pad(tile_off.astype(F32), ((0, 0), (0, LANES - counts.shape[1])))
    return {
        "seg_start": seg_start.reshape(-1),
        "seg_chunks": (seg // SLOT_ALIGN).reshape(-1),
        "tile_off": tile_off.reshape(-1),
        "tail_start": region_start + total,
        "tail_chunks": (region - total) // SLOT_ALIGN,
        "tile_expert": tile_expert.astype(jnp.int32),
        "n_used": (region_end[-1:] // MOE_ROW_TILE).astype(jnp.int32),
        "off_lanes": off_lanes.reshape(counts.shape[0], 1, LANES),
    }


def _rope_tables(ctx_len, seq):
    rows = seq // GRID_W
    row = jnp.repeat(jnp.arange(rows, dtype=F32), GRID_W)
    col = jnp.tile(jnp.arange(GRID_W, dtype=F32), rows)
    inv = ROPE_BASE ** (-jnp.arange(ROPE_PAIRS_PER_AXIS, dtype=F32) / ROPE_PAIRS_PER_AXIS)
    ang = jnp.concatenate([row[:, None] * inv, col[:, None] * inv], axis=-1)
    cos, sin = jnp.cos(ang), jnp.sin(ang)
    cos_t = jnp.concatenate([cos, cos, cos, cos], axis=-1)
    sin_t = jnp.concatenate([-sin, sin, -sin, sin], axis=-1)
    cos_t = jnp.concatenate([jnp.ones((ctx_len, LANES), F32), cos_t], axis=0)
    sin_t = jnp.concatenate([jnp.zeros((ctx_len, LANES), F32), sin_t], axis=0)
    return cos_t, sin_t


def kernel(x, c, ctx, c_ctx, ada_w, ada_b, norm1_g, norm2_g, w_in, w_out, q_norm_g, k_norm_g, lambda_q1, lambda_k1, lambda_q2, lambda_k2, subln_g, conv_w, conv_b, gate_b, mnorm_g, ffn_w1, ffn_w3, ffn_w2, router_w, moe_w1, moe_w3, moe_w2):
    batch, seq, d = x.shape
    ctx_len = ctx.shape[1]
    depth = ada_w.shape[0]
    assert depth == 2, "layer 0 runs the dense FFN on all rows, layer 1 the experts on the latents"
    n_ctx_rows = batch * ctx_len
    n_ctx_tiles = n_ctx_rows // ROW_TILE
    tiles_per_seq = seq // ROW_TILE
    n_main = w_in.shape[2] - N_GATES
    assert n_main == 7 * A_WIDTH
    m_dh = (n_main - 3 * A_WIDTH) // 4 // M_HEADS

    x_ctx = ctx.reshape(n_ctx_rows, d)
    x_lat = x.reshape(batch * seq, d)
    lat_first_tile = 0

    cc = jnp.zeros((2 * SUBLANES, d), F32).at[:batch].set(c).at[batch].set(c_ctx)
    mod_rows = adaln_all(cc, ada_w, ada_b)
    mods_all = jnp.concatenate([mod_rows[:, batch:batch + 1], mod_rows[:, :batch]], axis=1)
    mods_all = mods_all.reshape(depth, batch + 1, 6, d)

    cos_t, sin_t = _rope_tables(ctx_len, seq)

    for l in range(depth):
        last = l == depth - 1
        lam_init = 0.8 - 0.6 * math.exp(-0.3 * l)
        lam = (jnp.exp(jnp.sum(lambda_q1[l] * lambda_k1[l]).astype(F32))
               - jnp.exp(jnp.sum(lambda_q2[l] * lambda_k2[l]).astype(F32)) + lam_init)
        mods = mods_all[l]
        qg = jnp.tile(q_norm_g[l] * (A_DH ** -0.5 * math.log2(math.e)), 2).reshape(1, LANES)
        kg = jnp.tile(k_norm_g[l], 2).reshape(1, LANES)

        qk, av, mq, mkt, mv, mo, g_row = proj_in(
            x_ctx, x_lat, lat_first_tile, mods, norm1_g[l].reshape(1, d), w_in, l,
            conv_w[l], conv_b[l].reshape(1, -1), cos_t, sin_t, qg, kg, n_ctx_tiles, tiles_per_seq, m_dh ** -0.5)
        attn = functools.partial(attention, lam.reshape(1, 1), qk, av, subln_g[l].reshape(1, LANES),
                                 1.0 - lam_init, batch, ctx_len, seq)
        if l == 0:
            later = [w_out[0], w_out[1], ffn_w1[0], ffn_w3[0], moe_w1[0], moe_w3[0], moe_w2[0]]
            views = [w.reshape(-1, w.shape[-1]) for w in later]
            a_out, *later_bf16 = attn(latent_queries=True, to_bf16=views)
            w_out_bf16 = later_bf16[:2]
            ffn_bf16 = later_bf16[2:4] + [ffn_w2[0].astype(BF16)]
            experts_bf16 = [w.reshape(src.shape) for w, src in zip(later_bf16[4:], later[4:])]
        else:
            a_out = attn(latent_queries=True)
        m_ctx, m_out = mlstm(mq, mkt, mv, mo, g_row, gate_b[l].reshape(N_GATES, 1),
                             mnorm_g[l].reshape(1, -1), batch, ctx_len, seq)
        w_out_b = w_out_bf16[l]
        j = l // 2
        if not last:
            assert lat_first_tile == 0
            xs = mixer_ffn(a_out, attn(latent_queries=False), m_out, m_ctx, w_out_b, x_ctx, x_lat, mods,
                           norm2_g[l].reshape(1, d), *ffn_bf16, seq)
            x_ctx, x_lat, lat_first_tile = xs, xs, n_ctx_tiles
        else:
            x1, h2, sel, wcols, cnt = mixer_router(a_out, m_out, w_out_b, x_lat, lat_first_tile * ROW_TILE, mods,
                                                   norm2_g[l].reshape(1, d), router_w[j], seq)
            n_tok_tiles = x1.shape[0] // MOE_TOKEN_TILE
            counts = cnt[:, 0, :N_EXPERTS]
            worst = 2 * x1.shape[0] + n_tok_tiles * N_EXPERTS * (SLOT_ALIGN - 1) + N_EXPERTS * (MOE_ROW_TILE - SLOT_ALIGN)
            n_row_tiles = -(-worst // MOE_ROW_TILE)
            plan = _moe_plan(counts, n_row_tiles)
            slots = moe_dispatch(plan, h2, wcols, sel, n_row_tiles * MOE_ROW_TILE)
            ys = moe_experts(plan, slots, *experts_bf16)
            out = moe_combine(plan, ys, sel, x1, mods, seq)
    return out.reshape(batch, seq, d)
```

```python
import functools
import math

import jax
import jax.numpy as jnp
from jax import lax
from jax.experimental import pallas as pl
from jax.experimental.pallas import tpu as pltpu

F32 = jnp.float32
BF16 = jnp.bfloat16
HIGHEST = lax.Precision.HIGHEST

GRID_W = 64
A_HEADS = 4
A_DH = 64
A_DV = 2 * A_DH
A_WIDTH = A_HEADS * A_DV
M_HEADS = 4
N_GATES = 4 * M_HEADS
CONV_K = 5
ROPE_BASE = 10000.0
ROPE_PAIRS_PER_AXIS = A_DH // 4
N_EXPERTS = 8
EPS = 1e-6

LANES = 128
SUBLANES = 8
ROW_TILE = 256
MLSTM_CHUNK = 128
VMEM_LIMIT = 56 * 1024 * 1024

NT_DIMS = (((1,), (1,)), ((), ()))
TN_DIMS = (((0,), (0,)), ((), ()))


def _params(*sem):
    return pltpu.CompilerParams(dimension_semantics=sem, vmem_limit_bytes=VMEM_LIMIT)


def _silu(v):
    return v * jax.nn.sigmoid(v)


def _log_sigmoid(v):
    return jnp.minimum(v, 0.0) - jnp.log1p(jnp.exp(-jnp.abs(v)))


def _rms(v, axis=-1):
    return v * lax.rsqrt(jnp.mean(v * v, axis=axis, keepdims=True) + EPS)


def _split_bf16(w):
    hi = w.astype(BF16)
    return hi, (w - hi.astype(F32)).astype(BF16)


def _adaln_kernel(c_ref, w_ref, b_ref, o_ref):
    s = _silu(c_ref[...])
    o_ref[0] = jnp.dot(s, w_ref[0], precision=HIGHEST, preferred_element_type=F32) + b_ref[0]


def adaln_all(cc, ada_w, ada_b):
    depth, d, d6 = ada_w.shape
    n_col = d6 // d
    return pl.pallas_call(
        _adaln_kernel,
        grid=(depth, n_col),
        in_specs=[
            pl.BlockSpec(cc.shape, lambda l, j: (0, 0)),
            pl.BlockSpec((1, d, d), lambda l, j: (l, 0, j)),
            pl.BlockSpec((1, 1, d), lambda l, j: (l, 0, j)),
        ],
        out_specs=pl.BlockSpec((1, cc.shape[0], d), lambda l, j: (l, 0, j)),
        out_shape=jax.ShapeDtypeStruct((depth, cc.shape[0], d6), F32),
        compiler_params=_params("arbitrary", "arbitrary"),
        name="adaln",
    )(cc, ada_w, ada_b.reshape(depth, 1, d6))


def _proj_in_kernel(xa_ref, xb_ref, prev_ref, next_ref, mod_ref, g_ref, win_ref, cw_ref, cb_ref,
                    cos_ref, sin_ref, qg_ref, kg_ref,
                    qk_ref, v_ref, mq_ref, mkt_ref, mv_ref, mo_ref, gr_ref, ext_ref, w_ref,
                    *, n_ctx_tiles, tiles_per_seq, k_scale):
    i = pl.program_id(0)
    n_main = 7 * A_WIDTH

    @pl.when(i == 0)
    def _():
        w_ref[:, :n_main] = win_ref[0, :, :n_main].astype(BF16)
        w_ref[:, n_main:] = jnp.zeros((w_ref.shape[0], 2 * LANES), BF16)
        gate_hi, gate_lo = _split_bf16(win_ref[0, :, n_main:])
        w_ref[:, n_main:n_main + N_GATES] = gate_hi
        w_ref[:, n_main + LANES:n_main + LANES + N_GATES] = gate_lo

    is_ctx = i < n_ctx_tiles
    pos = (i - n_ctx_tiles) % tiles_per_seq
    first = jnp.logical_or(is_ctx, pos == 0)
    last = jnp.logical_or(is_ctx, pos == tiles_per_seq - 1)
    tm = xa_ref.shape[0]
    halo = SUBLANES

    x_ext = jnp.concatenate([prev_ref[...], jnp.where(is_ctx, xa_ref[...], xb_ref[...]), next_ref[...]], axis=0)
    h_ext = _rms(x_ext) * g_ref[...] * (1.0 + mod_ref[0, 1:2, :]) + mod_ref[0, 0:1, :]
    h = h_ext[halo:halo + tm]
    hb = h.astype(BF16)

    def proj(j):
        return jnp.dot(hb, w_ref[:, j * A_WIDTH:(j + 1) * A_WIDTH], preferred_element_type=F32)

    pre = jnp.dot(h_ext.astype(BF16), w_ref[:, 3 * A_WIDTH:5 * A_WIDTH], preferred_element_type=F32)
    ext_ref[0:halo, :] = jnp.where(first, 0.0, pre[0:halo])
    ext_ref[halo:halo + tm, :] = pre[halo:halo + tm]
    ext_ref[halo + tm:, :] = jnp.where(last, 0.0, pre[halo + tm:])

    r = lax.broadcasted_iota(jnp.int32, (LANES, LANES), 0)
    c = lax.broadcasted_iota(jnp.int32, (LANES, LANES), 1)
    group_ones = jnp.where((r & -A_DH) == (c & -A_DH), 1.0, 0.0).astype(BF16)
    lane = lax.broadcasted_iota(jnp.int32, (1, LANES), 1)
    first_half = (lane & (A_DH - 1)) < (A_DH // 2)
    cos = cos_ref[...]
    sin = sin_ref[...]
    for j, gain_ref in ((0, qg_ref), (1, kg_ref)):
        acc = proj(j)
        for hd in range(A_HEADS):
            t = acc[:, hd * LANES:(hd + 1) * LANES]
            ss = jnp.dot((t * t).astype(BF16), group_ones, preferred_element_type=F32)
            tn = t * lax.rsqrt(ss * (1.0 / A_DH) + EPS) * gain_ref[...]
            rot = jnp.where(first_half, pltpu.roll(tn, LANES - A_DH // 2, 1), pltpu.roll(tn, A_DH // 2, 1))
            qk_ref[:, j * A_WIDTH + hd * LANES: j * A_WIDTH + (hd + 1) * LANES] = (tn * cos + rot * sin).astype(BF16)
    def conv_silu(c0, c1):
        acc = jnp.zeros((tm, c1 - c0), F32) + cb_ref[:, c0:c1]
        pad = CONV_K // 2
        for j in range(CONV_K):
            acc = acc + cw_ref[j:j + 1, c0:c1] * ext_ref[halo - pad + j:halo - pad + j + tm, c0:c1]
        return _silu(acc)

    def store_k(c0, c1):
        yk = conv_silu(A_WIDTH + c0, A_WIDTH + c1) * k_scale
        for ci in range(tm // MLSTM_CHUNK):
            mkt_ref[ci, c0:c1, :] = yk[ci * MLSTM_CHUNK:(ci + 1) * MLSTM_CHUNK].T.astype(BF16)

    half = A_WIDTH // 2
    v_ref[...] = proj(2).astype(BF16)
    mq_ref[:, :half] = conv_silu(0, half).astype(BF16)
    mv_ref[...] = proj(5).astype(BF16)
    mq_ref[:, half:] = conv_silu(half, A_WIDTH).astype(BF16)
    mo_ref[...] = proj(6)
    store_k(0, half)

    p = jnp.dot(hb, w_ref[:, n_main:n_main + 2 * LANES], preferred_element_type=F32)
    h_lo = (h - hb.astype(F32)).astype(BF16)
    gates = p[:, :LANES] + p[:, LANES:] + jnp.dot(h_lo, w_ref[:, n_main:n_main + LANES], preferred_element_type=F32)
    gr_ref[...] = gates.T[:N_GATES, :]
    store_k(half, A_WIDTH)


def proj_in(xa, xb, b_first_tile, mods, norm_g, w_in, layer, conv_w, conv_b, cos_t, sin_t, qg, kg,
            n_ctx_tiles, tiles_per_seq, k_scale):
    d = xa.shape[1]
    n_lat_tiles = xb.shape[0] // ROW_TILE - b_first_tile
    n_tiles = n_ctx_tiles + n_lat_tiles
    rows = n_tiles * ROW_TILE
    per = ROW_TILE // SUBLANES

    def seg(i):
        return jnp.where(i < n_ctx_tiles, 0, 1 + (i - n_ctx_tiles) // tiles_per_seq)

    def rope_blk(i):
        return jnp.where(i < n_ctx_tiles, 0, 1 + (i - n_ctx_tiles) % tiles_per_seq)

    def lat(i):
        return jnp.maximum(i - n_ctx_tiles, 0)

    row = lambda i: (i, 0)
    const = lambda i: (0, 0)
    kern = functools.partial(_proj_in_kernel, n_ctx_tiles=n_ctx_tiles, tiles_per_seq=tiles_per_seq, k_scale=k_scale)
    return pl.pallas_call(
        kern,
        grid=(n_tiles,),
        in_specs=[
            pl.BlockSpec((ROW_TILE, d), lambda i: (jnp.minimum(i, n_ctx_tiles - 1), 0)),
            pl.BlockSpec((ROW_TILE, d), lambda i: (lat(i) + b_first_tile, 0)),
            pl.BlockSpec((SUBLANES, d), lambda i: (jnp.maximum(lat(i) * per - 1, 0) + b_first_tile * per, 0)),
            pl.BlockSpec((SUBLANES, d),
                         lambda i: (jnp.minimum((lat(i) + 1) * per, n_lat_tiles * per - 1) + b_first_tile * per, 0)),
            pl.BlockSpec((1, 6, d), lambda i: (seg(i), 0, 0)),
            pl.BlockSpec((1, d), const),
            pl.BlockSpec((1,) + w_in.shape[1:], lambda i: (layer, 0, 0), pipeline_mode=pl.Buffered(1)),
            pl.BlockSpec(conv_w.shape, const),
            pl.BlockSpec(conv_b.shape, const),
            pl.BlockSpec((ROW_TILE, LANES), lambda i: (rope_blk(i), 0)),
            pl.BlockSpec((ROW_TILE, LANES), lambda i: (rope_blk(i), 0)),
            pl.BlockSpec((1, LANES), const),
            pl.BlockSpec((1, LANES), const),
        ],
        out_specs=[
            pl.BlockSpec((ROW_TILE, 2 * A_WIDTH), row),
            pl.BlockSpec((ROW_TILE, A_WIDTH), row),
            pl.BlockSpec((ROW_TILE, A_WIDTH), row),
            pl.BlockSpec((ROW_TILE // MLSTM_CHUNK, A_WIDTH, MLSTM_CHUNK), lambda i: (i, 0, 0)),
            pl.BlockSpec((ROW_TILE, A_WIDTH), row),
            pl.BlockSpec((ROW_TILE, A_WIDTH), row),
            pl.BlockSpec((N_GATES, ROW_TILE), lambda i: (0, i)),
        ],
        out_shape=[
            jax.ShapeDtypeStruct((rows, 2 * A_WIDTH), BF16),
            jax.ShapeDtypeStruct((rows, A_WIDTH), BF16),
            jax.ShapeDtypeStruct((rows, A_WIDTH), BF16),
            jax.ShapeDtypeStruct((rows // MLSTM_CHUNK, A_WIDTH, MLSTM_CHUNK), BF16),
            jax.ShapeDtypeStruct((rows, A_WIDTH), BF16),
            jax.ShapeDtypeStruct((rows, A_WIDTH), F32),
            jax.ShapeDtypeStruct((N_GATES, rows), F32),
        ],
        scratch_shapes=[pltpu.VMEM((ROW_TILE + 2 * SUBLANES, 2 * A_WIDTH), F32),
                        pltpu.VMEM((d, 7 * A_WIDTH + 2 * LANES), BF16)],
        compiler_params=_params("arbitrary"),
        name="proj_in",
    )(xa, xb, xb, xb, mods, norm_g, w_in, conv_w, conv_b, cos_t, sin_t, qg, kg)


ATTN_KEY_BLOCK = 256


ATTN_SUB_TILE = 256
ATTN_Q_TILE = 2048


def _attn_kernel(lam_ref, q_ref, *refs, with_latent_keys, post_scale, n_cast):
    n_in = len(refs) - 2 * n_cast - 2
    cast_in, o_ref, cast_out, s_ref = refs[n_in:n_in + n_cast], refs[n_in + n_cast], refs[n_in + n_cast + 1:-1], refs[-1]
    if with_latent_keys:
        kc_ref, vc_ref, kl_ref, vl_ref, g_ref = refs[:n_in]
    else:
        kc_ref, vc_ref, g_ref = refs[:n_in]
    for src, dst in zip(cast_in, cast_out):
        dst[...] = src[...].astype(BF16)
    lam = lam_ref[0, 0]
    lane = lax.broadcasted_iota(jnp.int32, (1, LANES), 1)
    kb = ATTN_KEY_BLOCK
    ts = s_ref.shape[1]
    blocks = [(kc_ref, vc_ref, r0) for r0 in range(0, kc_ref.shape[0], kb)]
    if with_latent_keys:
        blocks += [(kl_ref, vl_ref, r0) for r0 in range(0, kl_ref.shape[0], kb)]

    units = [(u, m) for u in range(q_ref.shape[0] // ts) for m in range(2)]
    state = {}

    def pass1(n, j):
        u, m = units[n]
        if j == 0:
            q = q_ref[u * ts:(u + 1) * ts, :]
            keep = (lane < A_DH) if m == 0 else (lane >= A_DH)
            state[n] = {"q": jnp.where(keep, q, jnp.zeros_like(q)), "max": jnp.full((ts, LANES), -jnp.inf, F32)}
        k_ref, _, r0 = blocks[j]
        s = lax.dot_general(state[n]["q"], k_ref[r0:r0 + kb, :], NT_DIMS, preferred_element_type=F32)
        s_ref[n % 2, :, j * kb:(j + 1) * kb] = s
        for c0 in range(0, kb, LANES):
            state[n]["max"] = jnp.maximum(state[n]["max"], s[:, c0:c0 + LANES])

    ones = jnp.ones((kb, LANES), BF16)

    def pass2(n, j):
        st = state[n]
        if j == 0:
            st["mx"] = jnp.broadcast_to(jnp.max(st["max"], axis=1, keepdims=True), (ts, kb))
            st["acc"] = jnp.zeros((ts, 2 * LANES), F32)
        _, v_ref, r0 = blocks[j]
        e = jnp.exp2(s_ref[n % 2, :, j * kb:(j + 1) * kb] - st["mx"])
        v_ext = jnp.concatenate([v_ref[r0:r0 + kb, :], ones], axis=1)
        st["acc"] = st["acc"] + jnp.dot(e.astype(BF16), v_ext, preferred_element_type=F32)

    def finish(u):
        first, second = state.pop(2 * u)["acc"], state.pop(2 * u + 1)["acc"]
        out = first[:, :LANES] / first[:, LANES:] - second[:, :LANES] * (lam / second[:, LANES:])
        o_ref[u * ts:(u + 1) * ts, :] = (_rms(out) * g_ref[...] * post_scale).astype(BF16)

    for n in range(len(units) + 1):
        for j in range(len(blocks)):
            if n < len(units):
                pass1(n, j)
            if n >= 1:
                pass2(n - 1, j)
        if n >= 1 and units[n - 1][1] == 1:
            finish(units[n - 1][0])


def attention(lam, qk, v, subln_g, post_scale, batch, ctx_len, seq, latent_queries, to_bf16=()):
    lat_blk0 = batch * ctx_len // seq
    tq = ATTN_Q_TILE if latent_queries else ctx_len
    nq = seq // tq if latent_queries else 1
    q0 = batch * ctx_len // tq if latent_queries else 0
    q_per_batch = nq
    n_keys = ctx_len + (seq if latent_queries else 0)
    in_specs = [
        pl.BlockSpec(memory_space=pltpu.SMEM),
        pl.BlockSpec((tq, LANES), lambda b, h, t: (q0 + b * q_per_batch + t, h)),
        pl.BlockSpec((ctx_len, LANES), lambda b, h, t: (b, A_HEADS + h)),
        pl.BlockSpec((ctx_len, LANES), lambda b, h, t: (b, h)),
    ]
    args = [lam, qk, qk, v]
    if latent_queries:
        in_specs += [pl.BlockSpec((seq, LANES), lambda b, h, t: (lat_blk0 + b, A_HEADS + h)),
                     pl.BlockSpec((seq, LANES), lambda b, h, t: (lat_blk0 + b, h))]
        args += [qk, v]
    in_specs.append(pl.BlockSpec((1, LANES), lambda b, h, t: (0, 0)))
    args.append(subln_g)
    out_specs = [pl.BlockSpec((tq, LANES), lambda b, h, t: (b * q_per_batch + t, h))]
    out_shape = [jax.ShapeDtypeStruct((batch * nq * tq, A_WIDTH), BF16)]
    n_steps = batch * A_HEADS * nq
    for w in to_bf16:
        blk = pl.BlockSpec((w.shape[0] // n_steps, w.shape[1]), lambda b, h, t: ((b * A_HEADS + h) * nq + t, 0))
        in_specs.append(blk)
        args.append(w)
        out_specs.append(blk)
        out_shape.append(jax.ShapeDtypeStruct(w.shape, BF16))
    kern = functools.partial(_attn_kernel, with_latent_keys=latent_queries, post_scale=post_scale,
                             n_cast=len(to_bf16))
    outs = pl.pallas_call(
        kern,
        grid=(batch, A_HEADS, nq),
        in_specs=in_specs,
        out_specs=out_specs,
        out_shape=out_shape,
        scratch_shapes=[pltpu.VMEM((2, min(tq, ATTN_SUB_TILE), n_keys), F32)],
        compiler_params=_params("arbitrary", "arbitrary", "arbitrary"),
        name="diff_attention",
    )(*args)
    return outs if to_bf16 else outs[0]


def _mlstm_kernel(qc_ref, ql_ref, ktc_ref, ktl_ref, vc_ref, vl_ref, oc_ref, ol_ref, grc_ref, grl_ref, gb_ref, mg_ref,
                  outc_ref, outl_ref, row_ref, col_ref, s_ref, hf_ref, hb_ref):
    L = MLSTM_CHUNK
    dh = LANES
    n_ch = 2 * M_HEADS
    n_ctx = qc_ref.shape[0] // L
    n_lat = ql_ref.shape[0] // L
    n_chunks = n_ctx + n_lat
    nt = n_chunks * L

    g = jnp.concatenate([grc_ref[...], grl_ref[...]], axis=1) + gb_ref[...]
    hh = M_HEADS
    log_i = jnp.concatenate([g[0:hh], g[2 * hh:3 * hh]], axis=0)
    log_f = _log_sigmoid(jnp.concatenate([g[hh:2 * hh], g[3 * hh:4 * hh]], axis=0))
    sub = lax.broadcasted_iota(jnp.int32, (n_ch, nt), 0)
    lane_in_chunk = lax.broadcasted_iota(jnp.int32, (n_ch, nt), 1) & (L - 1)
    fwd_rows = sub < M_HEADS

    def chunk_scan(x, op, ident):
        xf, xb, k = x, x, 1
        while k < L:
            xf = op(xf, jnp.where(lane_in_chunk >= k, pltpu.roll(xf, k, 1), ident))
            xb = op(xb, jnp.where(lane_in_chunk < L - k, pltpu.roll(xb, nt - k, 1), ident))
            k *= 2
        return jnp.where(fwd_rows, xf, xb)

    b = chunk_scan(log_f, jnp.add, 0.0)
    a = log_i - b
    a_run = chunk_scan(a, jnp.maximum, -jnp.inf)

    fwd_col = lax.broadcasted_iota(jnp.int32, (n_ch, 1), 0) < M_HEADS

    def at_chunk_end(x, ci):
        xc = x[:, ci * L:(ci + 1) * L]
        return jnp.where(fwd_col, xc[:, L - 1:L], xc[:, 0:1])

    b_last = [at_chunk_end(b, ci) for ci in range(n_chunks)]
    a_last = [at_chunk_end(a_run, ci) for ci in range(n_chunks)]
    order_f = list(range(n_chunks))
    order_b = list(range(n_ctx - 1, -1, -1)) + list(range(n_chunks - 1, n_ctx - 1, -1))

    def carried(order):
        m, out = jnp.zeros((n_ch, 1), F32), {}
        for ci in order:
            out[ci] = m
            m = b_last[ci] + jnp.maximum(m, a_last[ci])
        return out

    m_f, m_b = carried(order_f), carried(order_b)
    for ci in range(n_chunks):
        m_prev = jnp.where(fwd_col, m_f[ci], m_b[ci])
        sl = slice(ci * L, (ci + 1) * L)
        mu = jnp.maximum(m_prev, a_run[:, sl])
        mu_last = jnp.maximum(m_prev, a_last[ci])
        row_ref[ci] = jnp.concatenate([a[:, sl], jnp.exp(a[:, sl] - mu_last), jnp.broadcast_to(m_prev, (n_ch, L)),
                                       jnp.broadcast_to(jnp.exp(m_prev - mu_last), (n_ch, L))], axis=0)
        tile = jnp.concatenate([mu, b[:, sl] + mu, jnp.zeros((L - 2 * n_ch, L), F32)], axis=0)
        col_ref[ci] = tile.T

    s_ref[...] = jnp.zeros(s_ref.shape, F32)
    rr = lax.broadcasted_iota(jnp.int32, (L, L), 0)
    cc = lax.broadcasted_iota(jnp.int32, (L, L), 1)
    visible = (cc <= rr, cc >= rr)
    ones = jnp.ones((L, dh), BF16)

    def chunk_rows(ci):
        return pl.ds(ci * L if isinstance(ci, int) else pl.multiple_of(ci * L, L), L)

    def twice(t):
        return jnp.concatenate([t, t], axis=1)

    def step(local, glob, q_ref, kt_ref, v_ref):
        for d in range(2):
            rows = chunk_rows(local[d])
            grow = chunk_rows(glob[d])
            cs = col_ref[glob[d]]
            rs = row_ref[glob[d]]
            h_ref = hf_ref if d == 0 else hb_ref
            for hd in range(M_HEADS):
                r = d * M_HEADS + hd
                mu = jnp.broadcast_to(cs[:, r:r + 1], (L, L))
                m_t = jnp.broadcast_to(cs[:, n_ch + r:n_ch + r + 1], (L, dh))
                a_row, w_row, m_prev, decay = (rs[j * n_ch + r:j * n_ch + r + 1, :] for j in range(4))
                w_intra = jnp.exp(jnp.where(visible[d], a_row - mu, -jnp.inf))
                w_inter = jnp.exp(m_prev - mu)
                cols = slice(hd * dh, (hd + 1) * dh)
                q = q_ref[rows, cols]
                kt = kt_ref[local[d], cols, :]
                v_ext = jnp.concatenate([v_ref[rows, cols], ones], axis=1)
                state = s_ref[r]
                qa = jnp.dot(q, jnp.concatenate([kt, state.astype(BF16)], axis=1), preferred_element_type=F32)
                s = (qa[:, :L] * w_intra).astype(BF16)
                ktw = (kt.astype(F32) * w_row).astype(BF16)
                sv = jnp.dot(jnp.concatenate([s, ktw], axis=0), v_ext, preferred_element_type=F32)
                t = sv[:L] + twice(w_inter) * qa[:, L:]
                h_ref[grow, cols] = t[:, :dh] / jnp.maximum(jnp.abs(t[:, dh:]), jnp.exp(-m_t))
                s_ref[r] = twice(jnp.broadcast_to(decay, (dh, L))) * state + sv[L:]

    def finish(local, glob, o_ref, out_ref):
        rows = chunk_rows(local)
        grow = chunk_rows(glob)
        for hd in range(M_HEADS):
            cols = slice(hd * dh, (hd + 1) * dh)
            hn = _rms(hf_ref[grow, cols] + hb_ref[grow, cols]) * mg_ref[:, cols]
            out_ref[rows, cols] = (jax.nn.sigmoid(o_ref[rows, cols]) * hn).astype(BF16)

    for i in range(n_ctx):
        step((i, n_ctx - 1 - i), (i, n_ctx - 1 - i), qc_ref, ktc_ref, vc_ref)
    for i in range(n_ctx):
        finish(i, i, oc_ref, outc_ref)

    def lat_step(i, carry):
        step((i, n_lat - 1 - i), (n_ctx + i, n_chunks - 1 - i), ql_ref, ktl_ref, vl_ref)
        return carry

    def lat_step_and_finish(i, carry):
        lat_step(i, carry)
        finish(i, n_ctx + i, ol_ref, outl_ref)
        finish(n_lat - 1 - i, n_chunks - 1 - i, ol_ref, outl_ref)
        return carry

    assert n_lat % 2 == 0
    lax.fori_loop(0, n_lat // 2, lat_step, 0, unroll=2)
    lax.fori_loop(n_lat // 2, n_lat, lat_step_and_finish, 0, unroll=2)


def mlstm(mq, mkt, mv, mo, g_row, gate_b, mnorm_g, batch, ctx_len, seq):
    rows, width = mq.shape
    assert MLSTM_CHUNK == LANES and width == M_HEADS * LANES
    lat0 = batch * ctx_len // seq
    n_chunks = (ctx_len + seq) // MLSTM_CHUNK
    ctx = lambda b: (b, 0)
    lat = lambda b: (lat0 + b, 0)
    const = lambda b: (0, 0)
    out_c, out_l = pl.pallas_call(
        _mlstm_kernel,
        grid=(batch,),
        in_specs=[
            pl.BlockSpec((ctx_len, width), ctx), pl.BlockSpec((seq, width), lat),
            pl.BlockSpec((ctx_len // MLSTM_CHUNK, width, MLSTM_CHUNK), lambda b: (b, 0, 0)),
            pl.BlockSpec((seq // MLSTM_CHUNK, width, MLSTM_CHUNK), lambda b: (lat0 + b, 0, 0)),
            pl.BlockSpec((ctx_len, width), ctx), pl.BlockSpec((seq, width), lat),
            pl.BlockSpec((ctx_len, width), ctx), pl.BlockSpec((seq, width), lat),
            pl.BlockSpec((N_GATES, ctx_len), lambda b: (0, b)), pl.BlockSpec((N_GATES, seq), lambda b: (0, lat0 + b)),
            pl.BlockSpec((N_GATES, 1), const),
            pl.BlockSpec((1, width), const),
        ],
        out_specs=[pl.BlockSpec((ctx_len, width), ctx), pl.BlockSpec((seq, width), ctx)],
        out_shape=[jax.ShapeDtypeStruct((batch * ctx_len, width), BF16),
                   jax.ShapeDtypeStruct((batch * seq, width), BF16)],
        scratch_shapes=[
            pltpu.VMEM((n_chunks, 8 * M_HEADS, MLSTM_CHUNK), F32),
            pltpu.VMEM((n_chunks, MLSTM_CHUNK, MLSTM_CHUNK), F32),
            pltpu.VMEM((2 * M_HEADS, LANES, 2 * LANES), F32),
            pltpu.VMEM((ctx_len + seq, width), F32),
            pltpu.VMEM((ctx_len + seq, width), F32),
        ],
        compiler_params=_params("arbitrary"),
        name="mlstm",
    )(mq, mq, mkt, mkt, mv, mv, mo, mo, g_row, g_row, gate_b, mnorm_g)
    return out_c, out_l


def _router_top2(logits):
    lane = lax.broadcasted_iota(jnp.int32, logits.shape, 1)
    logits = jnp.where(lane < N_EXPERTS, logits, -jnp.inf)
    v1 = jnp.max(logits, axis=1, keepdims=True)
    i1 = jnp.min(jnp.where(logits == v1, lane, LANES), axis=1, keepdims=True)
    rest = jnp.where(lane == i1, -jnp.inf, logits)
    v2 = jnp.max(rest, axis=1, keepdims=True)
    i2 = jnp.min(jnp.where(rest == v2, lane, LANES), axis=1, keepdims=True)
    e = jnp.exp(v2 - v1)
    w1 = 1.0 / (1.0 + e)
    w2 = e / (1.0 + e)
    mask = jnp.where(jnp.logical_or(lane == i1, lane == i2), 1.0, 0.0)
    expert = lane & (N_EXPERTS - 1)
    comb = jnp.where(lane < 3 * N_EXPERTS,
                     jnp.where(expert == i1, w1, 0.0) + jnp.where(expert == i2, w2, 0.0), 0.0)
    hi = comb.astype(BF16)
    rem = comb - hi.astype(F32)
    mid = rem.astype(BF16)
    lo = (rem - mid.astype(F32)).astype(BF16)
    wcols = jnp.where(lane < N_EXPERTS, hi, jnp.where(lane < 2 * N_EXPERTS, mid, lo))
    return mask, wcols


def _mixer_router_kernel(a_ref, m_ref, w_ref, x_ref, mod_ref, g_ref, rw_ref, x1_ref, h2_ref, sel_ref, wc_ref, cnt_ref):
    half = a_ref.shape[1]
    mix = (jnp.dot(a_ref[...], w_ref[:half, :], preferred_element_type=F32)
           + jnp.dot(m_ref[...], w_ref[half:, :], preferred_element_type=F32))
    x1 = x_ref[...] + mod_ref[0, 2:3, :] * mix
    x1_ref[...] = x1
    h2 = _rms(x1) * g_ref[...] * (1.0 + mod_ref[0, 4:5, :]) + mod_ref[0, 3:4, :]
    hb = h2.astype(BF16)
    h2_ref[...] = hb
    p = jnp.dot(hb, rw_ref[...], preferred_element_type=F32)
    h_lo = (h2 - hb.astype(F32)).astype(BF16)
    logits = p[:, :LANES] + p[:, LANES:] + jnp.dot(h_lo, rw_ref[:, :LANES], preferred_element_type=F32)
    mask, wcols = _router_top2(logits)
    sel_ref[...] = mask.astype(BF16)
    wc_ref[...] = wcols
    cnt_ref[0] = jnp.sum(mask, axis=0, keepdims=True)


def mixer_router(a_lat, m_lat, w_out, xs, first_row, mods, norm_g, router_w, rows_per_seq):
    d = xs.shape[1]
    half = a_lat.shape[1]
    tm = MOE_TOKEN_TILE
    n_tiles = a_lat.shape[0] // tm
    first_tile = first_row // tm
    dst = lambda i: (i, 0)
    const = lambda i: (0, 0)
    pad = ((0, 0), (0, LANES - router_w.shape[1]))
    rw = jnp.concatenate([jnp.pad(t, pad) for t in _split_bf16(router_w)], axis=1)
    return pl.pallas_call(
        _mixer_router_kernel,
        grid=(n_tiles,),
        in_specs=[
            pl.BlockSpec((tm, half), dst), pl.BlockSpec((tm, half), dst),
            pl.BlockSpec(w_out.shape, const),
            pl.BlockSpec((tm, d), lambda i: (i + first_tile, 0)),
            pl.BlockSpec((1, 6, d), lambda i: (1 + (i * tm) // rows_per_seq, 0, 0)),
            pl.BlockSpec((1, d), const),
            pl.BlockSpec(rw.shape, const),
        ],
        out_specs=[pl.BlockSpec((tm, d), dst), pl.BlockSpec((tm, d), dst),
                   pl.BlockSpec((tm, LANES), dst), pl.BlockSpec((tm, LANES), dst),
                   pl.BlockSpec((1, 1, LANES), lambda i: (i, 0, 0))],
        out_shape=[jax.ShapeDtypeStruct((n_tiles * tm, d), F32),
                   jax.ShapeDtypeStruct((n_tiles * tm, d), BF16),
                   jax.ShapeDtypeStruct((n_tiles * tm, LANES), BF16),
                   jax.ShapeDtypeStruct((n_tiles * tm, LANES), BF16),
                   jax.ShapeDtypeStruct((n_tiles, 1, LANES), F32)],
        compiler_params=_params("arbitrary"),
        name="mixer_router",
    )(a_lat, m_lat, w_out, xs, mods, norm_g, rw)


FFN_ROW_TILE = 512
FFN_FF_TILE = 1408


def _mixer_ffn_kernel(a_ref, ac_ref, m_ref, mc_ref, wo_ref, xc_ref, x_ref, mod_ref, g_ref, w1_ref, w3_ref, w2_ref,
                      o_ref, x1_ref, h2_ref, acc_ref, *, n_ctx_tiles):
    f = pl.program_id(1)

    @pl.when(f == 0)
    def _():
        is_ctx = pl.program_id(0) < n_ctx_tiles
        a = jnp.where(is_ctx, ac_ref[...], a_ref[...])
        m = jnp.where(is_ctx, mc_ref[...], m_ref[...])
        x = jnp.where(is_ctx, xc_ref[...], x_ref[...])
        half = a.shape[1]
        mix = (jnp.dot(a, wo_ref[:half, :], preferred_element_type=F32)
               + jnp.dot(m, wo_ref[half:, :], preferred_element_type=F32))
        x1 = x + mod_ref[0, 2:3, :] * mix
        x1_ref[...] = x1
        h2_ref[...] = (_rms(x1) * g_ref[...] * (1.0 + mod_ref[0, 4:5, :]) + mod_ref[0, 3:4, :]).astype(BF16)
        acc_ref[...] = jnp.zeros(acc_ref.shape, F32)

    h = h2_ref[...]
    a = _silu(jnp.dot(h, w1_ref[...], preferred_element_type=F32)) * jnp.dot(h, w3_ref[...], preferred_element_type=F32)
    acc_ref[...] += jnp.dot(a.astype(BF16), w2_ref[...], preferred_element_type=F32)

    @pl.when(f == pl.num_programs(1) - 1)
    def _():
        o_ref[...] = x1_ref[...] + mod_ref[0, 5:6, :] * acc_ref[...]


def mixer_ffn(a_lat, a_ctx, m_lat, m_ctx, w_out, x_ctx, x_lat, mods, norm_g, w1, w3, w2, rows_per_seq):
    d = x_lat.shape[1]
    half = a_lat.shape[1]
    dff = w1.shape[1]
    tm, tf = FFN_ROW_TILE, FFN_FF_TILE
    n_ctx_tiles = x_ctx.shape[0] // tm
    n_tiles = n_ctx_tiles + x_lat.shape[0] // tm
    per_seq = rows_per_seq // tm
    ctx = lambda i, f: (jnp.minimum(i, n_ctx_tiles - 1), 0)
    lat = lambda i, f: (jnp.maximum(i - n_ctx_tiles, 0), 0)
    const = lambda i, f: (0, 0)

    def seg(i, f):
        return (jnp.where(i < n_ctx_tiles, 0, 1 + (i - n_ctx_tiles) // per_seq), 0, 0)

    return pl.pallas_call(
        functools.partial(_mixer_ffn_kernel, n_ctx_tiles=n_ctx_tiles),
        grid=(n_tiles, dff // tf),
        in_specs=[
            pl.BlockSpec((tm, half), lat), pl.BlockSpec((tm, half), ctx),
            pl.BlockSpec((tm, half), lat), pl.BlockSpec((tm, half), ctx),
            pl.BlockSpec(w_out.shape, const),
            pl.BlockSpec((tm, d), ctx), pl.BlockSpec((tm, d), lat),
            pl.BlockSpec((1, 6, d), seg),
            pl.BlockSpec((1, d), const),
            pl.BlockSpec((d, tf), lambda i, f: (0, f)),
            pl.BlockSpec((d, tf), lambda i, f: (0, f)),
            pl.BlockSpec((tf, d), lambda i, f: (f, 0)),
        ],
        out_specs=pl.BlockSpec((tm, d), lambda i, f: (i, 0)),
        out_shape=jax.ShapeDtypeStruct((n_tiles * tm, d), F32),
        scratch_shapes=[pltpu.VMEM((tm, d), F32), pltpu.VMEM((tm, d), BF16), pltpu.VMEM((tm, d), F32)],
        compiler_params=_params("arbitrary", "arbitrary"),
        name="mixer_ffn",
    )(a_lat, a_ctx, m_lat, m_ctx, w_out, x_ctx, x_lat, mods, norm_g, w1, w3, w2)


MOE_TOKEN_TILE = 512
MOE_ROW_TILE = 512
MOE_FF_TILE = 1792
SLOT_ALIGN = 16
STAGE_ROWS = 2 * MOE_TOKEN_TILE + LANES


def _pow2_chunks(max_chunks):
    k = 1
    while k * 2 <= max_chunks:
        k *= 2
    while k >= 1:
        yield k
        k //= 2


def _segment_copies(n_chunks, src, dst, src_ref, dst_ref, sem, max_chunks, action):
    for k in _pow2_chunks(max_chunks):
        rows = k * SLOT_ALIGN
        taken = (n_chunks & ~(2 * k - 1)) * SLOT_ALIGN

        @pl.when((n_chunks & k) != 0)
        def _(rows=rows, taken=taken):
            s = pl.multiple_of(src + taken, SLOT_ALIGN)
            t = pl.multiple_of(dst + taken, SLOT_ALIGN)
            cp = pltpu.make_async_copy(src_ref.at[pl.ds(s, rows), :], dst_ref.at[pl.ds(t, rows), :], sem)
            if action == "start":
                cp.start()
            else:
                cp.wait()


def _slot_one_hot(sel_ref, off_ref):
    sel = sel_ref[...]
    tt = sel.shape[0]
    r = lax.broadcasted_iota(jnp.int32, (tt, tt), 0)
    c = lax.broadcasted_iota(jnp.int32, (tt, tt), 1)
    earlier = jnp.where(c < r, 1.0, 0.0).astype(BF16)
    rank = jnp.dot(earlier, sel, preferred_element_type=F32)
    pos = off_ref[0] + rank
    chosen = sel > 0
    pa = jnp.min(jnp.where(chosen, pos, float(STAGE_ROWS)), axis=1, keepdims=True)
    pb = jnp.max(jnp.where(chosen, pos, -1.0), axis=1, keepdims=True)
    slot = lax.broadcasted_iota(jnp.int32, (tt, STAGE_ROWS), 1).astype(F32)
    return jnp.where(jnp.logical_or(slot == pa, slot == pb), 1.0, 0.0).astype(BF16)


def _dispatch_kernel(seg_start_ref, seg_chunks_ref, tile_off_ref, tail_start_ref, tail_chunks_ref, n_used_ref,
                     h_ref, wc_ref, sel_ref, off_ref, xs_ref, stage_ref, zero_ref, sem):
    t = pl.program_id(0)
    last = pl.num_programs(0) - 1
    buf = t % 2
    max_chunks = MOE_TOKEN_TILE // SLOT_ALIGN

    def tile_copies(tile, which, action):
        for e in range(N_EXPERTS):
            idx = tile * N_EXPERTS + e
            _segment_copies(seg_chunks_ref[idx], tile_off_ref[idx], seg_start_ref[idx],
                            stage_ref.at[which], xs_ref, sem.at[which], max_chunks, action)

    one_hot = _slot_one_hot(sel_ref, off_ref)
    payload = jnp.concatenate([h_ref[...], wc_ref[...]], axis=1)
    stage_ref[buf] = lax.dot_general(one_hot, payload, TN_DIMS, preferred_element_type=F32).astype(BF16)
    tile_copies(t, buf, "start")

    @pl.when(t > 0)
    def _():
        tile_copies(t - 1, 1 - buf, "wait")

    @pl.when(t == last)
    def _():
        tile_copies(t, buf, "wait")
        zero_ref[...] = jnp.zeros(zero_ref.shape, BF16)
        for action in ("start", "wait"):
            for e in range(N_EXPERTS):
                _segment_copies(tail_chunks_ref[e], 0, tail_start_ref[e], zero_ref, xs_ref, sem.at[0],
                                MOE_ROW_TILE // SLOT_ALIGN - 1, action)

        def clear_tile(j, carry):
            rows = pl.ds(pl.multiple_of(j * MOE_ROW_TILE, MOE_ROW_TILE), MOE_ROW_TILE)
            cp = pltpu.make_async_copy(zero_ref, xs_ref.at[rows, :], sem.at[0])
            cp.start()
            cp.wait()
            return carry

        lax.fori_loop(n_used_ref[0], xs_ref.shape[0] // MOE_ROW_TILE, clear_tile, 0)


def moe_dispatch(plan, h2, wcols, sel, n_slots):
    rows, d = h2.shape
    tt = MOE_TOKEN_TILE
    width = d + LANES
    tile = lambda t, *_: (t, 0)
    grid_spec = pltpu.PrefetchScalarGridSpec(
        num_scalar_prefetch=6,
        grid=(rows // tt,),
        in_specs=[
            pl.BlockSpec((tt, d), tile),
            pl.BlockSpec((tt, LANES), tile),
            pl.BlockSpec((tt, LANES), tile),
            pl.BlockSpec((1, 1, LANES), lambda t, *_: (t, 0, 0)),
        ],
        out_specs=pl.BlockSpec(memory_space=pl.ANY),
        scratch_shapes=[pltpu.VMEM((2, STAGE_ROWS, width), BF16),
                        pltpu.VMEM((MOE_ROW_TILE, width), BF16),
                        pltpu.SemaphoreType.DMA((2,))],
    )
    return pl.pallas_call(
        _dispatch_kernel,
        grid_spec=grid_spec,
        out_shape=jax.ShapeDtypeStruct((n_slots, width), BF16),
        compiler_params=_params("arbitrary"),
        name="moe_dispatch",
    )(plan["seg_start"], plan["seg_chunks"], plan["tile_off"], plan["tail_start"], plan["tail_chunks"],
      plan["n_used"], h2, wcols, sel, plan["off_lanes"])


def _moe_kernel(tile_expert_ref, n_used_ref, xs_ref, w1_ref, w3_ref, w2_ref, ys_ref, acc_ref):
    i = pl.program_id(0)
    f = pl.program_id(1)

    @pl.when(i < n_used_ref[0])
    def _():
        @pl.when(f == 0)
        def _():
            acc_ref[...] = jnp.zeros(acc_ref.shape, F32)

        d = w1_ref.shape[1]
        x = xs_ref[:, :d]
        wc = xs_ref[:, d:].astype(F32)
        lane = lax.broadcasted_iota(jnp.int32, wc.shape, 1)
        mine = jnp.logical_and((lane & (N_EXPERTS - 1)) == tile_expert_ref[i], lane < 3 * N_EXPERTS)
        wt = jnp.sum(jnp.where(mine, wc, 0.0), axis=1, keepdims=True)
        a = _silu(jnp.dot(x, w1_ref[0], preferred_element_type=F32)) * jnp.dot(x, w3_ref[0], preferred_element_type=F32)
        acc_ref[...] += jnp.dot((a * wt).astype(BF16), w2_ref[0], preferred_element_type=F32)

        @pl.when(f == pl.num_programs(1) - 1)
        def _():
            ys_ref[...] = acc_ref[...].astype(BF16)

    @pl.when(jnp.logical_and(i >= n_used_ref[0], f == 0))
    def _():
        ys_ref[...] = jnp.zeros(ys_ref.shape, BF16)


def moe_experts(plan, xs, w1, w3, w2):
    n_slots, width = xs.shape
    ne, d, dff = w1.shape
    tm, tf = MOE_ROW_TILE, MOE_FF_TILE
    n_f = dff // tf

    def row(i, f, te, nu):
        return jnp.minimum(i, nu[0] - 1)

    def ff(i, f, te, nu):
        return jnp.where(i < nu[0], f, n_f - 1)

    grid_spec = pltpu.PrefetchScalarGridSpec(
        num_scalar_prefetch=2,
        grid=(n_slots // tm, n_f),
        in_specs=[
            pl.BlockSpec((tm, width), lambda i, f, te, nu: (row(i, f, te, nu), 0)),
            pl.BlockSpec((1, d, tf), lambda i, f, te, nu: (te[row(i, f, te, nu)], 0, ff(i, f, te, nu))),
            pl.BlockSpec((1, d, tf), lambda i, f, te, nu: (te[row(i, f, te, nu)], 0, ff(i, f, te, nu))),
            pl.BlockSpec((1, tf, d), lambda i, f, te, nu: (te[row(i, f, te, nu)], ff(i, f, te, nu), 0)),
        ],
        out_specs=pl.BlockSpec((tm, d), lambda i, f, te, nu: (i, 0)),
        scratch_shapes=[pltpu.VMEM((tm, d), F32)],
    )
    return pl.pallas_call(
        _moe_kernel,
        grid_spec=grid_spec,
        out_shape=jax.ShapeDtypeStruct((n_slots, d), BF16),
        compiler_params=_params("arbitrary", "arbitrary"),
        name="moe_experts",
    )(plan["tile_expert"], plan["n_used"], xs, w1, w3, w2)


def _combine_kernel(seg_start_ref, seg_chunks_ref, tile_off_ref,
                    ys_ref, sel_ref, off_ref, x_ref, mod_ref, o_ref, ybuf_ref, sem):
    t = pl.program_id(0)
    buf = t % 2
    max_chunks = MOE_TOKEN_TILE // SLOT_ALIGN

    def tile_copies(tile, which, action):
        for e in range(N_EXPERTS):
            idx = tile * N_EXPERTS + e
            _segment_copies(seg_chunks_ref[idx], seg_start_ref[idx], tile_off_ref[idx],
                            ys_ref, ybuf_ref.at[which], sem.at[which], max_chunks, action)

    @pl.when(t == 0)
    def _():
        ybuf_ref[...] = jnp.zeros(ybuf_ref.shape, BF16)
        tile_copies(t, buf, "start")

    @pl.when(t + 1 < pl.num_programs(0))
    def _():
        tile_copies(t + 1, 1 - buf, "start")

    one_hot = _slot_one_hot(sel_ref, off_ref)
    tile_copies(t, buf, "wait")
    y = jnp.dot(one_hot, ybuf_ref[buf], preferred_element_type=F32)
    o_ref[...] = x_ref[...] + mod_ref[0, 5:6, :] * y


def moe_combine(plan, ys, sel, x1, mods, rows_per_seq):
    rows, d = x1.shape
    tt = MOE_TOKEN_TILE
    tile = lambda t, *_: (t, 0)
    grid_spec = pltpu.PrefetchScalarGridSpec(
        num_scalar_prefetch=3,
        grid=(rows // tt,),
        in_specs=[
            pl.BlockSpec(memory_space=pl.ANY),
            pl.BlockSpec((tt, LANES), tile),
            pl.BlockSpec((1, 1, LANES), lambda t, *_: (t, 0, 0)),
            pl.BlockSpec((tt, d), tile),
            pl.BlockSpec((1, 6, d), lambda t, *_: (1 + (t * tt) // rows_per_seq, 0, 0)),
        ],
        out_specs=pl.BlockSpec((tt, d), tile),
        scratch_shapes=[pltpu.VMEM((2, STAGE_ROWS, d), BF16), pltpu.SemaphoreType.DMA((2,))],
    )
    return pl.pallas_call(
        _combine_kernel,
        grid_spec=grid_spec,
        out_shape=jax.ShapeDtypeStruct((rows, d), F32),
        compiler_params=_params("arbitrary"),
        name="moe_combine",
    )(plan["seg_start"], plan["seg_chunks"], plan["tile_off"], ys, sel, plan["off_lanes"], x1, mods)


def _moe_plan(counts, n_row_tiles):
    counts = counts.astype(jnp.int32)
    seg = (counts + SLOT_ALIGN - 1) // SLOT_ALIGN * SLOT_ALIGN
    tile_off = jnp.cumsum(seg, axis=1) - seg
    total = jnp.sum(seg, axis=0)
    region = (total + MOE_ROW_TILE - 1) // MOE_ROW_TILE * MOE_ROW_TILE
    region_end = jnp.cumsum(region)
    region_start = region_end - region
    seg_start = region_start[None, :] + jnp.cumsum(seg, axis=0) - seg
    first_row = jnp.arange(n_row_tiles, dtype=jnp.int32) * MOE_ROW_TILE
    tile_expert = jnp.minimum(jnp.sum(first_row[:, None] >= region_end[None, :], axis=1), counts.shape[1] - 1)
    off_lanes = jnp.pad(tile_off.astype(F32), ((0, 0), (0, LANES - counts.shape[1])))
    return {
        "seg_start": seg_start.reshape(-1),
        "seg_chunks": (seg // SLOT_ALIGN).reshape(-1),
        "tile_off": tile_off.reshape(-1),
        "tail_start": region_start + total,
        "tail_chunks": (region - total) // SLOT_ALIGN,
        "tile_expert": tile_expert.astype(jnp.int32),
        "n_used": (region_end[-1:] // MOE_ROW_TILE).astype(jnp.int32),
        "off_lanes": off_lanes.reshape(counts.shape[0], 1, LANES),
    }


def _rope_tables(ctx_len, seq):
    rows = seq // GRID_W
    row = jnp.repeat(jnp.arange(rows, dtype=F32), GRID_W)
    col = jnp.tile(jnp.arange(GRID_W, dtype=F32), rows)
    inv = ROPE_BASE ** (-jnp.arange(ROPE_PAIRS_PER_AXIS, dtype=F32) / ROPE_PAIRS_PER_AXIS)
    ang = jnp.concatenate([row[:, None] * inv, col[:, None] * inv], axis=-1)
    cos, sin = jnp.cos(ang), jnp.sin(ang)
    cos_t = jnp.concatenate([cos, cos, cos, cos], axis=-1)
    sin_t = jnp.concatenate([-sin, sin, -sin, sin], axis=-1)
    cos_t = jnp.concatenate([jnp.ones((ctx_len, LANES), F32), cos_t], axis=0)
    sin_t = jnp.concatenate([jnp.zeros((ctx_len, LANES), F32), sin_t], axis=0)
    return cos_t, sin_t


def kernel(x, c, ctx, c_ctx, ada_w, ada_b, norm1_g, norm2_g, w_in, w_out, q_norm_g, k_norm_g, lambda_q1, lambda_k1, lambda_q2, lambda_k2, subln_g, conv_w, conv_b, gate_b, mnorm_g, ffn_w1, ffn_w3, ffn_w2, router_w, moe_w1, moe_w3, moe_w2):
    batch, seq, d = x.shape
    ctx_len = ctx.shape[1]
    depth = ada_w.shape[0]
    assert depth == 2, "layer 0 runs the dense FFN on all rows, layer 1 the experts on the latents"
    n_ctx_rows = batch * ctx_len
    n_ctx_tiles = n_ctx_rows // ROW_TILE
    tiles_per_seq = seq // ROW_TILE
    n_main = w_in.shape[2] - N_GATES
    assert n_main == 7 * A_WIDTH
    m_dh = (n_main - 3 * A_WIDTH) // 4 // M_HEADS

    x_ctx = ctx.reshape(n_ctx_rows, d)
    x_lat = x.reshape(batch * seq, d)
    lat_first_tile = 0

    cc = jnp.zeros((2 * SUBLANES, d), F32).at[:batch].set(c).at[batch].set(c_ctx)
    mod_rows = adaln_all(cc, ada_w, ada_b)
    mods_all = jnp.concatenate([mod_rows[:, batch:batch + 1], mod_rows[:, :batch]], axis=1)
    mods_all = mods_all.reshape(depth, batch + 1, 6, d)

    cos_t, sin_t = _rope_tables(ctx_len, seq)

    for l in range(depth):
        last = l == depth - 1
        lam_init = 0.8 - 0.6 * math.exp(-0.3 * l)
        lam = (jnp.exp(jnp.sum(lambda_q1[l] * lambda_k1[l]).astype(F32))
               - jnp.exp(jnp.sum(lambda_q2[l] * lambda_k2[l]).astype(F32)) + lam_init)
        mods = mods_all[l]
        qg = jnp.tile(q_norm_g[l] * (A_DH ** -0.5 * math.log2(math.e)), 2).reshape(1, LANES)
        kg = jnp.tile(k_norm_g[l], 2).reshape(1, LANES)

        qk, av, mq, mkt, mv, mo, g_row = proj_in(
            x_ctx, x_lat, lat_first_tile, mods, norm1_g[l].reshape(1, d), w_in, l,
            conv_w[l], conv_b[l].reshape(1, -1), cos_t, sin_t, qg, kg, n_ctx_tiles, tiles_per_seq, m_dh ** -0.5)
        attn = functools.partial(attention, lam.reshape(1, 1), qk, av, subln_g[l].reshape(1, LANES),
                                 1.0 - lam_init, batch, ctx_len, seq)
        if l == 0:
            later = [w_out[0], w_out[1], ffn_w1[0], ffn_w3[0], moe_w1[0], moe_w3[0], moe_w2[0]]
            views = [w.reshape(-1, w.shape[-1]) for w in later]
            a_out, *later_bf16 = attn(latent_queries=True, to_bf16=views)
            w_out_bf16 = later_bf16[:2]
            ffn_bf16 = later_bf16[2:4] + [ffn_w2[0].astype(BF16)]
            experts_bf16 = [w.reshape(src.shape) for w, src in zip(later_bf16[4:], later[4:])]
        else:
            a_out = attn(latent_queries=True)
        m_ctx, m_out = mlstm(mq, mkt, mv, mo, g_row, gate_b[l].reshape(N_GATES, 1),
                             mnorm_g[l].reshape(1, -1), batch, ctx_len, seq)
        w_out_b = w_out_bf16[l]
        j = l // 2
        if not last:
            assert lat_first_tile == 0
            xs = mixer_ffn(a_out, attn(latent_queries=False), m_out, m_ctx, w_out_b, x_ctx, x_lat, mods,
                           norm2_g[l].reshape(1, d), *ffn_bf16, seq)
            x_ctx, x_lat, lat_first_tile = xs, xs, n_ctx_tiles
        else:
            x1, h2, sel, wcols, cnt = mixer_router(a_out, m_out, w_out_b, x_lat, lat_first_tile * ROW_TILE, mods,
                                                   norm2_g[l].reshape(1, d), router_w[j], seq)
            n_tok_tiles = x1.shape[0] // MOE_TOKEN_TILE
            counts = cnt[:, 0, :N_EXPERTS]
            worst = 2 * x1.shape[0] + n_tok_tiles * N_EXPERTS * (SLOT_ALIGN - 1) + N_EXPERTS * (MOE_ROW_TILE - SLOT_ALIGN)
            n_row_tiles = -(-worst // MOE_ROW_TILE)
            plan = _moe_plan(counts, n_row_tiles)
            slots = moe_dispatch(plan, h2, wcols, sel, n_row_tiles * MOE_ROW_TILE)
            ys = moe_experts(plan, slots, *experts_bf16)
            out = moe_combine(plan, ys, sel, x1, mods, seq)
    return out.reshape(batch, seq, d)
```

```python
import functools
import math

import jax
import jax.numpy as jnp
from jax import lax
from jax.experimental import pallas as pl
from jax.experimental.pallas import tpu as pltpu

F32 = jnp.float32
BF16 = jnp.bfloat16
HIGHEST = lax.Precision.HIGHEST

GRID_W = 64
A_HEADS = 4
A_DH = 64
A_DV = 2 * A_DH
A_WIDTH = A_HEADS * A_DV
M_HEADS = 4
N_GATES = 4 * M_HEADS
CONV_K = 5
ROPE_BASE = 10000.0
ROPE_PAIRS_PER_AXIS = A_DH // 4
N_EXPERTS = 8
EPS = 1e-6

LANES = 128
SUBLANES = 8
ROW_TILE = 256
MLSTM_CHUNK = 128
VMEM_LIMIT = 56 * 1024 * 1024

NT_DIMS = (((1,), (1,)), ((), ()))
TN_DIMS = (((0,), (0,)), ((), ()))


def _params(*sem):
    return pltpu.CompilerParams(dimension_semantics=sem, vmem_limit_bytes=VMEM_LIMIT)


def _silu(v):
    return v * jax.nn.sigmoid(v)


def _log_sigmoid(v):
    return jnp.minimum(v, 0.0) - jnp.log1p(jnp.exp(-jnp.abs(v)))


def _rms(v, axis=-1):
    return v * lax.rsqrt(jnp.mean(v * v, axis=axis, keepdims=True) + EPS)


def _split_bf16(w):
    hi = w.astype(BF16)
    return hi, (w - hi.astype(F32)).astype(BF16)


def _adaln_kernel(c_ref, w_ref, b_ref, o_ref):
    s = _silu(c_ref[...])
    o_ref[0] = jnp.dot(s, w_ref[0], precision=HIGHEST, preferred_element_type=F32) + b_ref[0]


def adaln_all(cc, ada_w, ada_b):
    depth, d, d6 = ada_w.shape
    n_col = d6 // d
    return pl.pallas_call(
        _adaln_kernel,
        grid=(depth, n_col),
        in_specs=[
            pl.BlockSpec(cc.shape, lambda l, j: (0, 0)),
            pl.BlockSpec((1, d, d), lambda l, j: (l, 0, j)),
            pl.BlockSpec((1, 1, d), lambda l, j: (l, 0, j)),
        ],
        out_specs=pl.BlockSpec((1, cc.shape[0], d), lambda l, j: (l, 0, j)),
        out_shape=jax.ShapeDtypeStruct((depth, cc.shape[0], d6), F32),
        compiler_params=_params("arbitrary", "arbitrary"),
        name="adaln",
    )(cc, ada_w, ada_b.reshape(depth, 1, d6))


def _proj_in_kernel(xa_ref, xb_ref, prev_ref, next_ref, mod_ref, g_ref, win_ref, cw_ref, cb_ref,
                    cos_ref, sin_ref, qg_ref, kg_ref,
                    qk_ref, v_ref, mq_ref, mkt_ref, mv_ref, mo_ref, gr_ref, ext_ref, w_ref,
                    *, n_ctx_tiles, tiles_per_seq, k_scale):
    i = pl.program_id(0)
    n_main = 7 * A_WIDTH

    @pl.when(i == 0)
    def _():
        w_ref[:, :n_main] = win_ref[0, :, :n_main].astype(BF16)
        w_ref[:, n_main:] = jnp.zeros((w_ref.shape[0], 2 * LANES), BF16)
        gate_hi, gate_lo = _split_bf16(win_ref[0, :, n_main:])
        w_ref[:, n_main:n_main + N_GATES] = gate_hi
        w_ref[:, n_main + LANES:n_main + LANES + N_GATES] = gate_lo

    is_ctx = i < n_ctx_tiles
    pos = (i - n_ctx_tiles) % tiles_per_seq
    first = jnp.logical_or(is_ctx, pos == 0)
    last = jnp.logical_or(is_ctx, pos == tiles_per_seq - 1)
    tm = xa_ref.shape[0]
    halo = SUBLANES

    x_ext = jnp.concatenate([prev_ref[...], jnp.where(is_ctx, xa_ref[...], xb_ref[...]), next_ref[...]], axis=0)
    h_ext = _rms(x_ext) * g_ref[...] * (1.0 + mod_ref[0, 1:2, :]) + mod_ref[0, 0:1, :]
    h = h_ext[halo:halo + tm]
    hb = h.astype(BF16)

    def proj(j):
        return jnp.dot(hb, w_ref[:, j * A_WIDTH:(j + 1) * A_WIDTH], preferred_element_type=F32)

    pre = jnp.dot(h_ext.astype(BF16), w_ref[:, 3 * A_WIDTH:5 * A_WIDTH], preferred_element_type=F32)
    ext_ref[0:halo, :] = jnp.where(first, 0.0, pre[0:halo])
    ext_ref[halo:halo + tm, :] = pre[halo:halo + tm]
    ext_ref[halo + tm:, :] = jnp.where(last, 0.0, pre[halo + tm:])

    r = lax.broadcasted_iota(jnp.int32, (LANES, LANES), 0)
    c = lax.broadcasted_iota(jnp.int32, (LANES, LANES), 1)
    group_ones = jnp.where((r & -A_DH) == (c & -A_DH), 1.0, 0.0).astype(BF16)
    lane = lax.broadcasted_iota(jnp.int32, (1, LANES), 1)
    first_half = (lane & (A_DH - 1)) < (A_DH // 2)
    cos = cos_ref[...]
    sin = sin_ref[...]
    for j, gain_ref in ((0, qg_ref), (1, kg_ref)):
        acc = proj(j)
        for hd in range(A_HEADS):
            t = acc[:, hd * LANES:(hd + 1) * LANES]
            ss = jnp.dot((t * t).astype(BF16), group_ones, preferred_element_type=F32)
            tn = t * lax.rsqrt(ss * (1.0 / A_DH) + EPS) * gain_ref[...]
            rot = jnp.where(first_half, pltpu.roll(tn, LANES - A_DH // 2, 1), pltpu.roll(tn, A_DH // 2, 1))
            qk_ref[:, j * A_WIDTH + hd * LANES: j * A_WIDTH + (hd + 1) * LANES] = (tn * cos + rot * sin).astype(BF16)
    def conv_silu(c0, c1):
        acc = jnp.zeros((tm, c1 - c0), F32) + cb_ref[:, c0:c1]
        pad = CONV_K // 2
        for j in range(CONV_K):
            acc = acc + cw_ref[j:j + 1, c0:c1] * ext_ref[halo - pad + j:halo - pad + j + tm, c0:c1]
        return _silu(acc)

    def store_k(c0, c1):
        yk = conv_silu(A_WIDTH + c0, A_WIDTH + c1) * k_scale
        for ci in range(tm // MLSTM_CHUNK):
            mkt_ref[ci, c0:c1, :] = yk[ci * MLSTM_CHUNK:(ci + 1) * MLSTM_CHUNK].T.astype(BF16)

    half = A_WIDTH // 2
    v_ref[...] = proj(2).astype(BF16)
    mq_ref[:, :half] = conv_silu(0, half).astype(BF16)
    mv_ref[...] = proj(5).astype(BF16)
    mq_ref[:, half:] = conv_silu(half, A_WIDTH).astype(BF16)
    mo_ref[...] = proj(6)
    store_k(0, half)

    p = jnp.dot(hb, w_ref[:, n_main:n_main + 2 * LANES], preferred_element_type=F32)
    h_lo = (h - hb.astype(F32)).astype(BF16)
    gates = p[:, :LANES] + p[:, LANES:] + jnp.dot(h_lo, w_ref[:, n_main:n_main + LANES], preferred_element_type=F32)
    gr_ref[...] = gates.T[:N_GATES, :]
    store_k(half, A_WIDTH)


def proj_in(xa, xb, b_first_tile, mods, norm_g, w_in, layer, conv_w, conv_b, cos_t, sin_t, qg, kg,
            n_ctx_tiles, tiles_per_seq, k_scale):
    d = xa.shape[1]
    n_lat_tiles = xb.shape[0] // ROW_TILE - b_first_tile
    n_tiles = n_ctx_tiles + n_lat_tiles
    rows = n_tiles * ROW_TILE
    per = ROW_TILE // SUBLANES

    def seg(i):
        return jnp.where(i < n_ctx_tiles, 0, 1 + (i - n_ctx_tiles) // tiles_per_seq)

    def rope_blk(i):
        return jnp.where(i < n_ctx_tiles, 0, 1 + (i - n_ctx_tiles) % tiles_per_seq)

    def lat(i):
        return jnp.maximum(i - n_ctx_tiles, 0)

    row = lambda i: (i, 0)
    const = lambda i: (0, 0)
    kern = functools.partial(_proj_in_kernel, n_ctx_tiles=n_ctx_tiles, tiles_per_seq=tiles_per_seq, k_scale=k_scale)
    return pl.pallas_call(
        kern,
        grid=(n_tiles,),
        in_specs=[
            pl.BlockSpec((ROW_TILE, d), lambda i: (jnp.minimum(i, n_ctx_tiles - 1), 0)),
            pl.BlockSpec((ROW_TILE, d), lambda i: (lat(i) + b_first_tile, 0)),
            pl.BlockSpec((SUBLANES, d), lambda i: (jnp.maximum(lat(i) * per - 1, 0) + b_first_tile * per, 0)),
            pl.BlockSpec((SUBLANES, d),
                         lambda i: (jnp.minimum((lat(i) + 1) * per, n_lat_tiles * per - 1) + b_first_tile * per, 0)),
            pl.BlockSpec((1, 6, d), lambda i: (seg(i), 0, 0)),
            pl.BlockSpec((1, d), const),
            pl.BlockSpec((1,) + w_in.shape[1:], lambda i: (layer, 0, 0), pipeline_mode=pl.Buffered(1)),
            pl.BlockSpec(conv_w.shape, const),
            pl.BlockSpec(conv_b.shape, const),
            pl.BlockSpec((ROW_TILE, LANES), lambda i: (rope_blk(i), 0)),
            pl.BlockSpec((ROW_TILE, LANES), lambda i: (rope_blk(i), 0)),
            pl.BlockSpec((1, LANES), const),
            pl.BlockSpec((1, LANES), const),
        ],
        out_specs=[
            pl.BlockSpec((ROW_TILE, 2 * A_WIDTH), row),
            pl.BlockSpec((ROW_TILE, A_WIDTH), row),
            pl.BlockSpec((ROW_TILE, A_WIDTH), row),
            pl.BlockSpec((ROW_TILE // MLSTM_CHUNK, A_WIDTH, MLSTM_CHUNK), lambda i: (i, 0, 0)),
            pl.BlockSpec((ROW_TILE, A_WIDTH), row),
            pl.BlockSpec((ROW_TILE, A_WIDTH), row),
            pl.BlockSpec((N_GATES, ROW_TILE), lambda i: (0, i)),
        ],
        out_shape=[
            jax.ShapeDtypeStruct((rows, 2 * A_WIDTH), BF16),
            jax.ShapeDtypeStruct((rows, A_WIDTH), BF16),
            jax.ShapeDtypeStruct((rows, A_WIDTH), BF16),
            jax.ShapeDtypeStruct((rows // MLSTM_CHUNK, A_WIDTH, MLSTM_CHUNK), BF16),
            jax.ShapeDtypeStruct((rows, A_WIDTH), BF16),
            jax.ShapeDtypeStruct((rows, A_WIDTH), F32),
            jax.ShapeDtypeStruct((N_GATES, rows), F32),
        ],
        scratch_shapes=[pltpu.VMEM((ROW_TILE + 2 * SUBLANES, 2 * A_WIDTH), F32),
                        pltpu.VMEM((d, 7 * A_WIDTH + 2 * LANES), BF16)],
        compiler_params=_params("arbitrary"),
        name="proj_in",
    )(xa, xb, xb, xb, mods, norm_g, w_in, conv_w, conv_b, cos_t, sin_t, qg, kg)


ATTN_KEY_BLOCK = 256


ATTN_SUB_TILE = 256
ATTN_Q_TILE = 2048


def _attn_kernel(lam_ref, q_ref, *refs, with_latent_keys, post_scale, n_cast):
    n_in = len(refs) - 2 * n_cast - 2
    cast_in, o_ref, cast_out, s_ref = refs[n_in:n_in + n_cast], refs[n_in + n_cast], refs[n_in + n_cast + 1:-1], refs[-1]
    if with_latent_keys:
        kc_ref, vc_ref, kl_ref, vl_ref, g_ref = refs[:n_in]
    else:
        kc_ref, vc_ref, g_ref = refs[:n_in]
    for src, dst in zip(cast_in, cast_out):
        dst[...] = src[...].astype(BF16)
    lam = lam_ref[0, 0]
    lane = lax.broadcasted_iota(jnp.int32, (1, LANES), 1)
    kb = ATTN_KEY_BLOCK
    ts = s_ref.shape[1]
    blocks = [(kc_ref, vc_ref, r0) for r0 in range(0, kc_ref.shape[0], kb)]
    if with_latent_keys:
        blocks += [(kl_ref, vl_ref, r0) for r0 in range(0, kl_ref.shape[0], kb)]

    units = [(u, m) for u in range(q_ref.shape[0] // ts) for m in range(2)]
    state = {}

    def pass1(n, j):
        u, m = units[n]
        if j == 0:
            q = q_ref[u * ts:(u + 1) * ts, :]
            keep = (lane < A_DH) if m == 0 else (lane >= A_DH)
            state[n] = {"q": jnp.where(keep, q, jnp.zeros_like(q)), "max": jnp.full((ts, LANES), -jnp.inf, F32)}
        k_ref, _, r0 = blocks[j]
        s = lax.dot_general(state[n]["q"], k_ref[r0:r0 + kb, :], NT_DIMS, preferred_element_type=F32)
        s_ref[n % 2, :, j * kb:(j + 1) * kb] = s
        for c0 in range(0, kb, LANES):
            state[n]["max"] = jnp.maximum(state[n]["max"], s[:, c0:c0 + LANES])

    ones = jnp.ones((kb, LANES), BF16)

    def pass2(n, j):
        st = state[n]
        if j == 0:
            st["mx"] = jnp.broadcast_to(jnp.max(st["max"], axis=1, keepdims=True), (ts, kb))
            st["acc"] = jnp.zeros((ts, 2 * LANES), F32)
        _, v_ref, r0 = blocks[j]
        e = jnp.exp2(s_ref[n % 2, :, j * kb:(j + 1) * kb] - st["mx"])
        v_ext = jnp.concatenate([v_ref[r0:r0 + kb, :], ones], axis=1)
        st["acc"] = st["acc"] + jnp.dot(e.astype(BF16), v_ext, preferred_element_type=F32)

    def finish(u):
        first, second = state.pop(2 * u)["acc"], state.pop(2 * u + 1)["acc"]
        out = first[:, :LANES] / first[:, LANES:] - second[:, :LANES] * (lam / second[:, LANES:])
        o_ref[u * ts:(u + 1) * ts, :] = (_rms(out) * g_ref[...] * post_scale).astype(BF16)

    for n in range(len(units) + 1):
        for j in range(len(blocks)):
            if n < len(units):
                pass1(n, j)
            if n >= 1:
                pass2(n - 1, j)
        if n >= 1 and units[n - 1][1] == 1:
            finish(units[n - 1][0])


def attention(lam, qk, v, subln_g, post_scale, batch, ctx_len, seq, latent_queries, to_bf16=()):
    lat_blk0 = batch * ctx_len // seq
    tq = ATTN_Q_TILE if latent_queries else ctx_len
    nq = seq // tq if latent_queries else 1
    q0 = batch * ctx_len // tq if latent_queries else 0
    q_per_batch = nq
    n_keys = ctx_len + (seq if latent_queries else 0)
    in_specs = [
        pl.BlockSpec(memory_space=pltpu.SMEM),
        pl.BlockSpec((tq, LANES), lambda b, h, t: (q0 + b * q_per_batch + t, h)),
        pl.BlockSpec((ctx_len, LANES), lambda b, h, t: (b, A_HEADS + h)),
        pl.BlockSpec((ctx_len, LANES), lambda b, h, t: (b, h)),
    ]
    args = [lam, qk, qk, v]
    if latent_queries:
        in_specs += [pl.BlockSpec((seq, LANES), lambda b, h, t: (lat_blk0 + b, A_HEADS + h)),
                     pl.BlockSpec((seq, LANES), lambda b, h, t: (lat_blk0 + b, h))]
        args += [qk, v]
    in_specs.append(pl.BlockSpec((1, LANES), lambda b, h, t: (0, 0)))
    args.append(subln_g)
    out_specs = [pl.BlockSpec((tq, LANES), lambda b, h, t: (b * q_per_batch + t, h))]
    out_shape = [jax.ShapeDtypeStruct((batch * nq * tq, A_WIDTH), BF16)]
    n_steps = batch * A_HEADS * nq
    for w in to_bf16:
        blk = pl.BlockSpec((w.shape[0] // n_steps, w.shape[1]), lambda b, h, t: ((b * A_HEADS + h) * nq + t, 0))
        in_specs.append(blk)
        args.append(w)
        out_specs.append(blk)
        out_shape.append(jax.ShapeDtypeStruct(w.shape, BF16))
    kern = functools.partial(_attn_kernel, with_latent_keys=latent_queries, post_scale=post_scale,
                             n_cast=len(to_bf16))
    outs = pl.pallas_call(
        kern,
        grid=(batch, A_HEADS, nq),
        in_specs=in_specs,
        out_specs=out_specs,
        out_shape=out_shape,
        scratch_shapes=[pltpu.VMEM((2, min(tq, ATTN_SUB_TILE), n_keys), F32)],
        compiler_params=_params("arbitrary", "arbitrary", "arbitrary"),
        name="diff_attention",
    )(*args)
    return outs if to_bf16 else outs[0]


def _mlstm_kernel(qc_ref, ql_ref, ktc_ref, ktl_ref, vc_ref, vl_ref, oc_ref, ol_ref, grc_ref, grl_ref, gb_ref, mg_ref,
                  outc_ref, outl_ref, row_ref, col_ref, s_ref, hf_ref, hb_ref):
    L = MLSTM_CHUNK
    dh = LANES
    n_ch = 2 * M_HEADS
    n_ctx = qc_ref.shape[0] // L
    n_lat = ql_ref.shape[0] // L
    n_chunks = n_ctx + n_lat
    nt = n_chunks * L

    g = jnp.concatenate([grc_ref[...], grl_ref[...]], axis=1) + gb_ref[...]
    hh = M_HEADS
    log_i = jnp.concatenate([g[0:hh], g[2 * hh:3 * hh]], axis=0)
    log_f = _log_sigmoid(jnp.concatenate([g[hh:2 * hh], g[3 * hh:4 * hh]], axis=0))
    sub = lax.broadcasted_iota(jnp.int32, (n_ch, nt), 0)
    lane_in_chunk = lax.broadcasted_iota(jnp.int32, (n_ch, nt), 1) & (L - 1)
    fwd_rows = sub < M_HEADS

    def chunk_scan(x, op, ident):
        xf, xb, k = x, x, 1
        while k < L:
            xf = op(xf, jnp.where(lane_in_chunk >= k, pltpu.roll(xf, k, 1), ident))
            xb = op(xb, jnp.where(lane_in_chunk < L - k, pltpu.roll(xb, nt - k, 1), ident))
            k *= 2
        return jnp.where(fwd_rows, xf, xb)

    b = chunk_scan(log_f, jnp.add, 0.0)
    a = log_i - b
    a_run = chunk_scan(a, jnp.maximum, -jnp.inf)

    fwd_col = lax.broadcasted_iota(jnp.int32, (n_ch, 1), 0) < M_HEADS

    def at_chunk_end(x, ci):
        xc = x[:, ci * L:(ci + 1) * L]
        return jnp.where(fwd_col, xc[:, L - 1:L], xc[:, 0:1])

    b_last = [at_chunk_end(b, ci) for ci in range(n_chunks)]
    a_last = [at_chunk_end(a_run, ci) for ci in range(n_chunks)]
    order_f = list(range(n_chunks))
    order_b = list(range(n_ctx - 1, -1, -1)) + list(range(n_chunks - 1, n_ctx - 1, -1))

    def carried(order):
        m, out = jnp.zeros((n_ch, 1), F32), {}
        for ci in order:
            out[ci] = m
            m = b_last[ci] + jnp.maximum(m, a_last[ci])
        return out

    m_f, m_b = carried(order_f), carried(order_b)
    for ci in range(n_chunks):
        m_prev = jnp.where(fwd_col, m_f[ci], m_b[ci])
        sl = slice(ci * L, (ci + 1) * L)
        mu = jnp.maximum(m_prev, a_run[:, sl])
        mu_last = jnp.maximum(m_prev, a_last[ci])
        row_ref[ci] = jnp.concatenate([a[:, sl], jnp.exp(a[:, sl] - mu_last), jnp.broadcast_to(m_prev, (n_ch, L)),
                                       jnp.broadcast_to(jnp.exp(m_prev - mu_last), (n_ch, L))], axis=0)
        tile = jnp.concatenate([mu, b[:, sl] + mu, jnp.zeros((L - 2 * n_ch, L), F32)], axis=0)
        col_ref[ci] = tile.T

    s_ref[...] = jnp.zeros(s_ref.shape, F32)
    rr = lax.broadcasted_iota(jnp.int32, (L, L), 0)
    cc = lax.broadcasted_iota(jnp.int32, (L, L), 1)
    visible = (cc <= rr, cc >= rr)
    ones = jnp.ones((L, dh), BF16)

    def chunk_rows(ci):
        return pl.ds(ci * L if isinstance(ci, int) else pl.multiple_of(ci * L, L), L)

    def twice(t):
        return jnp.concatenate([t, t], axis=1)

    def step(local, glob, q_ref, kt_ref, v_ref):
        for d in range(2):
            rows = chunk_rows(local[d])
            grow = chunk_rows(glob[d])
            cs = col_ref[glob[d]]
            rs = row_ref[glob[d]]
            h_ref = hf_ref if d == 0 else hb_ref
            for hd in range(M_HEADS):
                r = d * M_HEADS + hd
                mu = jnp.broadcast_to(cs[:, r:r + 1], (L, L))
                m_t = jnp.broadcast_to(cs[:, n_ch + r:n_ch + r + 1], (L, dh))
                a_row, w_row, m_prev, decay = (rs[j * n_ch + r:j * n_ch + r + 1, :] for j in range(4))
                w_intra = jnp.exp(jnp.where(visible[d], a_row - mu, -jnp.inf))
                w_inter = jnp.exp(m_prev - mu)
                cols = slice(hd * dh, (hd + 1) * dh)
                q = q_ref[rows, cols]
                kt = kt_ref[local[d], cols, :]
                v_ext = jnp.concatenate([v_ref[rows, cols], ones], axis=1)
                state = s_ref[r]
                qa = jnp.dot(q, jnp.concatenate([kt, state.astype(BF16)], axis=1), preferred_element_type=F32)
                s = (qa[:, :L] * w_intra).astype(BF16)
                ktw = (kt.astype(F32) * w_row).astype(BF16)
                sv = jnp.dot(jnp.concatenate([s, ktw], axis=0), v_ext, preferred_element_type=F32)
                t = sv[:L] + twice(w_inter) * qa[:, L:]
                h_ref[grow, cols] = t[:, :dh] / jnp.maximum(jnp.abs(t[:, dh:]), jnp.exp(-m_t))
                s_ref[r] = twice(jnp.broadcast_to(decay, (dh, L))) * state + sv[L:]

    def finish(local, glob, o_ref, out_ref):
        rows = chunk_rows(local)
        grow = chunk_rows(glob)
        for hd in range(M_HEADS):
            cols = slice(hd * dh, (hd + 1) * dh)
            hn = _rms(hf_ref[grow, cols] + hb_ref[grow, cols]) * mg_ref[:, cols]
            out_ref[rows, cols] = (jax.nn.sigmoid(o_ref[rows, cols]) * hn).astype(BF16)

    for i in range(n_ctx):
        step((i, n_ctx - 1 - i), (i, n_ctx - 1 - i), qc_ref, ktc_ref, vc_ref)
    for i in range(n_ctx):
        finish(i, i, oc_ref, outc_ref)

    def lat_step(i, carry):
        step((i, n_lat - 1 - i), (n_ctx + i, n_chunks - 1 - i), ql_ref, ktl_ref, vl_ref)
        return carry

    def lat_step_and_finish(i, carry):
        lat_step(i, carry)
        finish(i, n_ctx + i, ol_ref, outl_ref)
        finish(n_lat - 1 - i, n_chunks - 1 - i, ol_ref, outl_ref)
        return carry

    assert n_lat % 2 == 0
    lax.fori_loop(0, n_lat // 2, lat_step, 0, unroll=4)
    lax.fori_loop(n_lat // 2, n_lat, lat_step_and_finish, 0, unroll=4)


def mlstm(mq, mkt, mv, mo, g_row, gate_b, mnorm_g, batch, ctx_len, seq):
    rows, width = mq.shape
    assert MLSTM_CHUNK == LANES and width == M_HEADS * LANES
    lat0 = batch * ctx_len // seq
    n_chunks = (ctx_len + seq) // MLSTM_CHUNK
    ctx = lambda b: (b, 0)
    lat = lambda b: (lat0 + b, 0)
    const = lambda b: (0, 0)
    out_c, out_l = pl.pallas_call(
        _mlstm_kernel,
        grid=(batch,),
        in_specs=[
            pl.BlockSpec((ctx_len, width), ctx), pl.BlockSpec((seq, width), lat),
            pl.BlockSpec((ctx_len // MLSTM_CHUNK, width, MLSTM_CHUNK), lambda b: (b, 0, 0)),
            pl.BlockSpec((seq // MLSTM_CHUNK, width, MLSTM_CHUNK), lambda b: (lat0 + b, 0, 0)),
            pl.BlockSpec((ctx_len, width), ctx), pl.BlockSpec((seq, width), lat),
            pl.BlockSpec((ctx_len, width), ctx), pl.BlockSpec((seq, width), lat),
            pl.BlockSpec((N_GATES, ctx_len), lambda b: (0, b)), pl.BlockSpec((N_GATES, seq), lambda b: (0, lat0 + b)),
            pl.BlockSpec((N_GATES, 1), const),
            pl.BlockSpec((1, width), const),
        ],
        out_specs=[pl.BlockSpec((ctx_len, width), ctx), pl.BlockSpec((seq, width), ctx)],
        out_shape=[jax.ShapeDtypeStruct((batch * ctx_len, width), BF16),
                   jax.ShapeDtypeStruct((batch * seq, width), BF16)],
        scratch_shapes=[
            pltpu.VMEM((n_chunks, 8 * M_HEADS, MLSTM_CHUNK), F32),
            pltpu.VMEM((n_chunks, MLSTM_CHUNK, MLSTM_CHUNK), F32),
            pltpu.VMEM((2 * M_HEADS, LANES, 2 * LANES), F32),
            pltpu.VMEM((ctx_len + seq, width), F32),
            pltpu.VMEM((ctx_len + seq, width), F32),
        ],
        compiler_params=_params("arbitrary"),
        name="mlstm",
    )(mq, mq, mkt, mkt, mv, mv, mo, mo, g_row, g_row, gate_b, mnorm_g)
    return out_c, out_l


def _router_top2(logits):
    lane = lax.broadcasted_iota(jnp.int32, logits.shape, 1)
    logits = jnp.where(lane < N_EXPERTS, logits, -jnp.inf)
    v1 = jnp.max(logits, axis=1, keepdims=True)
    i1 = jnp.min(jnp.where(logits == v1, lane, LANES), axis=1, keepdims=True)
    rest = jnp.where(lane == i1, -jnp.inf, logits)
    v2 = jnp.max(rest, axis=1, keepdims=True)
    i2 = jnp.min(jnp.where(rest == v2, lane, LANES), axis=1, keepdims=True)
    e = jnp.exp(v2 - v1)
    w1 = 1.0 / (1.0 + e)
    w2 = e / (1.0 + e)
    mask = jnp.where(jnp.logical_or(lane == i1, lane == i2), 1.0, 0.0)
    expert = lane & (N_EXPERTS - 1)
    comb = jnp.where(lane < 3 * N_EXPERTS,
                     jnp.where(expert == i1, w1, 0.0) + jnp.where(expert == i2, w2, 0.0), 0.0)
    hi = comb.astype(BF16)
    rem = comb - hi.astype(F32)
    mid = rem.astype(BF16)
    lo = (rem - mid.astype(F32)).astype(BF16)
    wcols = jnp.where(lane < N_EXPERTS, hi, jnp.where(lane < 2 * N_EXPERTS, mid, lo))
    return mask, wcols


def _mixer_router_kernel(a_ref, m_ref, w_ref, x_ref, mod_ref, g_ref, rw_ref, x1_ref, h2_ref, sel_ref, wc_ref, cnt_ref):
    half = a_ref.shape[1]
    mix = (jnp.dot(a_ref[...], w_ref[:half, :], preferred_element_type=F32)
           + jnp.dot(m_ref[...], w_ref[half:, :], preferred_element_type=F32))
    x1 = x_ref[...] + mod_ref[0, 2:3, :] * mix
    x1_ref[...] = x1
    h2 = _rms(x1) * g_ref[...] * (1.0 + mod_ref[0, 4:5, :]) + mod_ref[0, 3:4, :]
    hb = h2.astype(BF16)
    h2_ref[...] = hb
    p = jnp.dot(hb, rw_ref[...], preferred_element_type=F32)
    h_lo = (h2 - hb.astype(F32)).astype(BF16)
    logits = p[:, :LANES] + p[:, LANES:] + jnp.dot(h_lo, rw_ref[:, :LANES], preferred_element_type=F32)
    mask, wcols = _router_top2(logits)
    sel_ref[...] = mask.astype(BF16)
    wc_ref[...] = wcols
    cnt_ref[0] = jnp.sum(mask, axis=0, keepdims=True)


def mixer_router(a_lat, m_lat, w_out, xs, first_row, mods, norm_g, router_w, rows_per_seq):
    d = xs.shape[1]
    half = a_lat.shape[1]
    tm = MOE_TOKEN_TILE
    n_tiles = a_lat.shape[0] // tm
    first_tile = first_row // tm
    dst = lambda i: (i, 0)
    const = lambda i: (0, 0)
    pad = ((0, 0), (0, LANES - router_w.shape[1]))
    rw = jnp.concatenate([jnp.pad(t, pad) for t in _split_bf16(router_w)], axis=1)
    return pl.pallas_call(
        _mixer_router_kernel,
        grid=(n_tiles,),
        in_specs=[
            pl.BlockSpec((tm, half), dst), pl.BlockSpec((tm, half), dst),
            pl.BlockSpec(w_out.shape, const),
            pl.BlockSpec((tm, d), lambda i: (i + first_tile, 0)),
            pl.BlockSpec((1, 6, d), lambda i: (1 + (i * tm) // rows_per_seq, 0, 0)),
            pl.BlockSpec((1, d), const),
            pl.BlockSpec(rw.shape, const),
        ],
        out_specs=[pl.BlockSpec((tm, d), dst), pl.BlockSpec((tm, d), dst),
                   pl.BlockSpec((tm, LANES), dst), pl.BlockSpec((tm, LANES), dst),
                   pl.BlockSpec((1, 1, LANES), lambda i: (i, 0, 0))],
        out_shape=[jax.ShapeDtypeStruct((n_tiles * tm, d), F32),
                   jax.ShapeDtypeStruct((n_tiles * tm, d), BF16),
                   jax.ShapeDtypeStruct((n_tiles * tm, LANES), BF16),
                   jax.ShapeDtypeStruct((n_tiles * tm, LANES), BF16),
                   jax.ShapeDtypeStruct((n_tiles, 1, LANES), F32)],
        compiler_params=_params("arbitrary"),
        name="mixer_router",
    )(a_lat, m_lat, w_out, xs, mods, norm_g, rw)


FFN_ROW_TILE = 512
FFN_FF_TILE = 1408


def _mixer_ffn_kernel(a_ref, ac_ref, m_ref, mc_ref, wo_ref, xc_ref, x_ref, mod_ref, g_ref, w1_ref, w3_ref, w2_ref,
                      o_ref, x1_ref, h2_ref, acc_ref, *, n_ctx_tiles):
    f = pl.program_id(1)

    @pl.when(f == 0)
    def _():
        is_ctx = pl.program_id(0) < n_ctx_tiles
        a = jnp.where(is_ctx, ac_ref[...], a_ref[...])
        m = jnp.where(is_ctx, mc_ref[...], m_ref[...])
        x = jnp.where(is_ctx, xc_ref[...], x_ref[...])
        half = a.shape[1]
        mix = (jnp.dot(a, wo_ref[:half, :], preferred_element_type=F32)
               + jnp.dot(m, wo_ref[half:, :], preferred_element_type=F32))
        x1 = x + mod_ref[0, 2:3, :] * mix
        x1_ref[...] = x1
        h2_ref[...] = (_rms(x1) * g_ref[...] * (1.0 + mod_ref[0, 4:5, :]) + mod_ref[0, 3:4, :]).astype(BF16)
        acc_ref[...] = jnp.zeros(acc_ref.shape, F32)

    h = h2_ref[...]
    a = _silu(jnp.dot(h, w1_ref[...], preferred_element_type=F32)) * jnp.dot(h, w3_ref[...], preferred_element_type=F32)
    acc_ref[...] += jnp.dot(a.astype(BF16), w2_ref[...], preferred_element_type=F32)

    @pl.when(f == pl.num_programs(1) - 1)
    def _():
        o_ref[...] = x1_ref[...] + mod_ref[0, 5:6, :] * acc_ref[...]


def mixer_ffn(a_lat, a_ctx, m_lat, m_ctx, w_out, x_ctx, x_lat, mods, norm_g, w1, w3, w2, rows_per_seq):
    d = x_lat.shape[1]
    half = a_lat.shape[1]
    dff = w1.shape[1]
    tm, tf = FFN_ROW_TILE, FFN_FF_TILE
    n_ctx_tiles = x_ctx.shape[0] // tm
    n_tiles = n_ctx_tiles + x_lat.shape[0] // tm
    per_seq = rows_per_seq // tm
    ctx = lambda i, f: (jnp.minimum(i, n_ctx_tiles - 1), 0)
    lat = lambda i, f: (jnp.maximum(i - n_ctx_tiles, 0), 0)
    const = lambda i, f: (0, 0)

    def seg(i, f):
        return (jnp.where(i < n_ctx_tiles, 0, 1 + (i - n_ctx_tiles) // per_seq), 0, 0)

    return pl.pallas_call(
        functools.partial(_mixer_ffn_kernel, n_ctx_tiles=n_ctx_tiles),
        grid=(n_tiles, dff // tf),
        in_specs=[
            pl.BlockSpec((tm, half), lat), pl.BlockSpec((tm, half), ctx),
            pl.BlockSpec((tm, half), lat), pl.BlockSpec((tm, half), ctx),
            pl.BlockSpec(w_out.shape, const),
            pl.BlockSpec((tm, d), ctx), pl.BlockSpec((tm, d), lat),
            pl.BlockSpec((1, 6, d), seg),
            pl.BlockSpec((1, d), const),
            pl.BlockSpec((d, tf), lambda i, f: (0, f)),
            pl.BlockSpec((d, tf), lambda i, f: (0, f)),
            pl.BlockSpec((tf, d), lambda i, f: (f, 0)),
        ],
        out_specs=pl.BlockSpec((tm, d), lambda i, f: (i, 0)),
        out_shape=jax.ShapeDtypeStruct((n_tiles * tm, d), F32),
        scratch_shapes=[pltpu.VMEM((tm, d), F32), pltpu.VMEM((tm, d), BF16), pltpu.VMEM((tm, d), F32)],
        compiler_params=_params("arbitrary", "arbitrary"),
        name="mixer_ffn",
    )(a_lat, a_ctx, m_lat, m_ctx, w_out, x_ctx, x_lat, mods, norm_g, w1, w3, w2)


MOE_TOKEN_TILE = 512
MOE_ROW_TILE = 512
MOE_FF_TILE = 1792
SLOT_ALIGN = 16
STAGE_ROWS = 2 * MOE_TOKEN_TILE + LANES


def _pow2_chunks(max_chunks):
    k = 1
    while k * 2 <= max_chunks:
        k *= 2
    while k >= 1:
        yield k
        k //= 2


def _segment_copies(n_chunks, src, dst, src_ref, dst_ref, sem, max_chunks, action):
    for k in _pow2_chunks(max_chunks):
        rows = k * SLOT_ALIGN
        taken = (n_chunks & ~(2 * k - 1)) * SLOT_ALIGN

        @pl.when((n_chunks & k) != 0)
        def _(rows=rows, taken=taken):
            s = pl.multiple_of(src + taken, SLOT_ALIGN)
            t = pl.multiple_of(dst + taken, SLOT_ALIGN)
            cp = pltpu.make_async_copy(src_ref.at[pl.ds(s, rows), :], dst_ref.at[pl.ds(t, rows), :], sem)
            if action == "start":
                cp.start()
            else:
                cp.wait()


def _slot_one_hot(sel_ref, off_ref):
    sel = sel_ref[...]
    tt = sel.shape[0]
    r = lax.broadcasted_iota(jnp.int32, (tt, tt), 0)
    c = lax.broadcasted_iota(jnp.int32, (tt, tt), 1)
    earlier = jnp.where(c < r, 1.0, 0.0).astype(BF16)
    rank = jnp.dot(earlier, sel, preferred_element_type=F32)
    pos = off_ref[0] + rank
    chosen = sel > 0
    pa = jnp.min(jnp.where(chosen, pos, float(STAGE_ROWS)), axis=1, keepdims=True)
    pb = jnp.max(jnp.where(chosen, pos, -1.0), axis=1, keepdims=True)
    slot = lax.broadcasted_iota(jnp.int32, (tt, STAGE_ROWS), 1).astype(F32)
    return jnp.where(jnp.logical_or(slot == pa, slot == pb), 1.0, 0.0).astype(BF16)


def _dispatch_kernel(seg_start_ref, seg_chunks_ref, tile_off_ref, tail_start_ref, tail_chunks_ref, n_used_ref,
                     h_ref, wc_ref, sel_ref, off_ref, xs_ref, stage_ref, zero_ref, sem):
    t = pl.program_id(0)
    last = pl.num_programs(0) - 1
    buf = t % 2
    max_chunks = MOE_TOKEN_TILE // SLOT_ALIGN

    def tile_copies(tile, which, action):
        for e in range(N_EXPERTS):
            idx = tile * N_EXPERTS + e
            _segment_copies(seg_chunks_ref[idx], tile_off_ref[idx], seg_start_ref[idx],
                            stage_ref.at[which], xs_ref, sem.at[which], max_chunks, action)

    one_hot = _slot_one_hot(sel_ref, off_ref)
    payload = jnp.concatenate([h_ref[...], wc_ref[...]], axis=1)
    stage_ref[buf] = lax.dot_general(one_hot, payload, TN_DIMS, preferred_element_type=F32).astype(BF16)
    tile_copies(t, buf, "start")

    @pl.when(t > 0)
    def _():
        tile_copies(t - 1, 1 - buf, "wait")

    @pl.when(t == last)
    def _():
        tile_copies(t, buf, "wait")
        zero_ref[...] = jnp.zeros(zero_ref.shape, BF16)
        for action in ("start", "wait"):
            for e in range(N_EXPERTS):
                _segment_copies(tail_chunks_ref[e], 0, tail_start_ref[e], zero_ref, xs_ref, sem.at[0],
                                MOE_ROW_TILE // SLOT_ALIGN - 1, action)

        def clear_tile(j, carry):
            rows = pl.ds(pl.multiple_of(j * MOE_ROW_TILE, MOE_ROW_TILE), MOE_ROW_TILE)
            cp = pltpu.make_async_copy(zero_ref, xs_ref.at[rows, :], sem.at[0])
            cp.start()
            cp.wait()
            return carry

        lax.fori_loop(n_used_ref[0], xs_ref.shape[0] // MOE_ROW_TILE, clear_tile, 0)


def moe_dispatch(plan, h2, wcols, sel, n_slots):
    rows, d = h2.shape
    tt = MOE_TOKEN_TILE
    width = d + LANES
    tile = lambda t, *_: (t, 0)
    grid_spec = pltpu.PrefetchScalarGridSpec(
        num_scalar_prefetch=6,
        grid=(rows // tt,),
        in_specs=[
            pl.BlockSpec((tt, d), tile),
            pl.BlockSpec((tt, LANES), tile),
            pl.BlockSpec((tt, LANES), tile),
            pl.BlockSpec((1, 1, LANES), lambda t, *_: (t, 0, 0)),
        ],
        out_specs=pl.BlockSpec(memory_space=pl.ANY),
        scratch_shapes=[pltpu.VMEM((2, STAGE_ROWS, width), BF16),
                        pltpu.VMEM((MOE_ROW_TILE, width), BF16),
                        pltpu.SemaphoreType.DMA((2,))],
    )
    return pl.pallas_call(
        _dispatch_kernel,
        grid_spec=grid_spec,
        out_shape=jax.ShapeDtypeStruct((n_slots, width), BF16),
        compiler_params=_params("arbitrary"),
        name="moe_dispatch",
    )(plan["seg_start"], plan["seg_chunks"], plan["tile_off"], plan["tail_start"], plan["tail_chunks"],
      plan["n_used"], h2, wcols, sel, plan["off_lanes"])


def _moe_kernel(tile_expert_ref, n_used_ref, xs_ref, w1_ref, w3_ref, w2_ref, ys_ref, acc_ref):
    i = pl.program_id(0)
    f = pl.program_id(1)

    @pl.when(i < n_used_ref[0])
    def _():
        @pl.when(f == 0)
        def _():
            acc_ref[...] = jnp.zeros(acc_ref.shape, F32)

        d = w1_ref.shape[1]
        x = xs_ref[:, :d]
        wc = xs_ref[:, d:].astype(F32)
        lane = lax.broadcasted_iota(jnp.int32, wc.shape, 1)
        mine = jnp.logical_and((lane & (N_EXPERTS - 1)) == tile_expert_ref[i], lane < 3 * N_EXPERTS)
        wt = jnp.sum(jnp.where(mine, wc, 0.0), axis=1, keepdims=True)
        a = _silu(jnp.dot(x, w1_ref[0], preferred_element_type=F32)) * jnp.dot(x, w3_ref[0], preferred_element_type=F32)
        acc_ref[...] += jnp.dot((a * wt).astype(BF16), w2_ref[0], preferred_element_type=F32)

        @pl.when(f == pl.num_programs(1) - 1)
        def _():
            ys_ref[...] = acc_ref[...].astype(BF16)

    @pl.when(jnp.logical_and(i >= n_used_ref[0], f == 0))
    def _():
        ys_ref[...] = jnp.zeros(ys_ref.shape, BF16)


def moe_experts(plan, xs, w1, w3, w2):
    n_slots, width = xs.shape
    ne, d, dff = w1.shape
    tm, tf = MOE_ROW_TILE, MOE_FF_TILE
    n_f = dff // tf

    def row(i, f, te, nu):
        return jnp.minimum(i, nu[0] - 1)

    def ff(i, f, te, nu):
        return jnp.where(i < nu[0], f, n_f - 1)

    grid_spec = pltpu.PrefetchScalarGridSpec(
        num_scalar_prefetch=2,
        grid=(n_slots // tm, n_f),
        in_specs=[
            pl.BlockSpec((tm, width), lambda i, f, te, nu: (row(i, f, te, nu), 0)),
            pl.BlockSpec((1, d, tf), lambda i, f, te, nu: (te[row(i, f, te, nu)], 0, ff(i, f, te, nu))),
            pl.BlockSpec((1, d, tf), lambda i, f, te, nu: (te[row(i, f, te, nu)], 0, ff(i, f, te, nu))),
            pl.BlockSpec((1, tf, d), lambda i, f, te, nu: (te[row(i, f, te, nu)], ff(i, f, te, nu), 0)),
        ],
        out_specs=pl.BlockSpec((tm, d), lambda i, f, te, nu: (i, 0)),
        scratch_shapes=[pltpu.VMEM((tm, d), F32)],
    )
    return pl.pallas_call(
        _moe_kernel,
        grid_spec=grid_spec,
        out_shape=jax.ShapeDtypeStruct((n_slots, d), BF16),
        compiler_params=_params("arbitrary", "arbitrary"),
        name="moe_experts",
    )(plan["tile_expert"], plan["n_used"], xs, w1, w3, w2)


def _combine_kernel(seg_start_ref, seg_chunks_ref, tile_off_ref,
                    ys_ref, sel_ref, off_ref, x_ref, mod_ref, o_ref, ybuf_ref, sem):
    t = pl.program_id(0)
    buf = t % 2
    max_chunks = MOE_TOKEN_TILE // SLOT_ALIGN

    def tile_copies(tile, which, action):
        for e in range(N_EXPERTS):
            idx = tile * N_EXPERTS + e
            _segment_copies(seg_chunks_ref[idx], seg_start_ref[idx], tile_off_ref[idx],
                            ys_ref, ybuf_ref.at[which], sem.at[which], max_chunks, action)

    @pl.when(t == 0)
    def _():
        ybuf_ref[...] = jnp.zeros(ybuf_ref.shape, BF16)
        tile_copies(t, buf, "start")

    @pl.when(t + 1 < pl.num_programs(0))
    def _():
        tile_copies(t + 1, 1 - buf, "start")

    one_hot = _slot_one_hot(sel_ref, off_ref)
    tile_copies(t, buf, "wait")
    y = jnp.dot(one_hot, ybuf_ref[buf], preferred_element_type=F32)
    o_ref[...] = x_ref[...] + mod_ref[0, 5:6, :] * y


def moe_combine(plan, ys, sel, x1, mods, rows_per_seq):
    rows, d = x1.shape
    tt = MOE_TOKEN_TILE
    tile = lambda t, *_: (t, 0)
    grid_spec = pltpu.PrefetchScalarGridSpec(
        num_scalar_prefetch=3,
        grid=(rows // tt,),
        in_specs=[
            pl.BlockSpec(memory_space=pl.ANY),
            pl.BlockSpec((tt, LANES), tile),
            pl.BlockSpec((1, 1, LANES), lambda t, *_: (t, 0, 0)),
            pl.BlockSpec((tt, d), tile),
            pl.BlockSpec((1, 6, d), lambda t, *_: (1 + (t * tt) // rows_per_seq, 0, 0)),
        ],
        out_specs=pl.BlockSpec((tt, d), tile),
        scratch_shapes=[pltpu.VMEM((2, STAGE_ROWS, d), BF16), pltpu.SemaphoreType.DMA((2,))],
    )
    return pl.pallas_call(
        _combine_kernel,
        grid_spec=grid_spec,
        out_shape=jax.ShapeDtypeStruct((rows, d), F32),
        compiler_params=_params("arbitrary"),
        name="moe_combine",
    )(plan["seg_start"], plan["seg_chunks"], plan["tile_off"], ys, sel, plan["off_lanes"], x1, mods)


def _moe_plan(counts, n_row_tiles):
    counts = counts.astype(jnp.int32)
    seg = (counts + SLOT_ALIGN - 1) // SLOT_ALIGN * SLOT_ALIGN
    tile_off = jnp.cumsum(seg, axis=1) - seg
    total = jnp.sum(seg, axis=0)
    region = (total + MOE_ROW_TILE - 1) // MOE_ROW_TILE * MOE_ROW_TILE
    region_end = jnp.cumsum(region)
    region_start = region_end - region
    seg_start = region_start[None, :] + jnp.cumsum(seg, axis=0) - seg
    first_row = jnp.arange(n_row_tiles, dtype=jnp.int32) * MOE_ROW_TILE
    tile_expert = jnp.minimum(jnp.sum(first_row[:, None] >= region_end[None, :], axis=1), counts.shape[1] - 1)
    off_lanes = jnp.pad(tile_off.astype(F32), ((0, 0), (0, LANES - counts.shape[1])))
    return {
        "seg_start": seg_start.reshape(-1),
        "seg_chunks": (seg // SLOT_ALIGN).reshape(-1),
        "tile_off": tile_off.reshape(-1),
        "tail_start": region_start + total,
        "tail_chunks": (region - total) // SLOT_ALIGN,
        "tile_expert": tile_expert.astype(jnp.int32),
        "n_used": (region_end[-1:] // MOE_ROW_TILE).astype(jnp.int32),
        "off_lanes": off_lanes.reshape(counts.shape[0], 1, LANES),
    }


def _rope_tables(ctx_len, seq):
    rows = seq // GRID_W
    row = jnp.repeat(jnp.arange(rows, dtype=F32), GRID_W)
    col = jnp.tile(jnp.arange(GRID_W, dtype=F32), rows)
    inv = ROPE_BASE ** (-jnp.arange(ROPE_PAIRS_PER_AXIS, dtype=F32) / ROPE_PAIRS_PER_AXIS)
    ang = jnp.concatenate([row[:, None] * inv, col[:, None] * inv], axis=-1)
    cos, sin = jnp.cos(ang), jnp.sin(ang)
    cos_t = jnp.concatenate([cos, cos, cos, cos], axis=-1)
    sin_t = jnp.concatenate([-sin, sin, -sin, sin], axis=-1)
    cos_t = jnp.concatenate([jnp.ones((ctx_len, LANES), F32), cos_t], axis=0)
    sin_t = jnp.concatenate([jnp.zeros((ctx_len, LANES), F32), sin_t], axis=0)
    return cos_t, sin_t


def kernel(x, c, ctx, c_ctx, ada_w, ada_b, norm1_g, norm2_g, w_in, w_out, q_norm_g, k_norm_g, lambda_q1, lambda_k1, lambda_q2, lambda_k2, subln_g, conv_w, conv_b, gate_b, mnorm_g, ffn_w1, ffn_w3, ffn_w2, router_w, moe_w1, moe_w3, moe_w2):
    batch, seq, d = x.shape
    ctx_len = ctx.shape[1]
    depth = ada_w.shape[0]
    assert depth == 2, "layer 0 runs the dense FFN on all rows, layer 1 the experts on the latents"
    n_ctx_rows = batch * ctx_len
    n_ctx_tiles = n_ctx_rows // ROW_TILE
    tiles_per_seq = seq // ROW_TILE
    n_main = w_in.shape[2] - N_GATES
    assert n_main == 7 * A_WIDTH
    m_dh = (n_main - 3 * A_WIDTH) // 4 // M_HEADS

    x_ctx = ctx.reshape(n_ctx_rows, d)
    x_lat = x.reshape(batch * seq, d)
    lat_first_tile = 0

    cc = jnp.zeros((2 * SUBLANES, d), F32).at[:batch].set(c).at[batch].set(c_ctx)
    mod_rows = adaln_all(cc, ada_w, ada_b)
    mods_all = jnp.concatenate([mod_rows[:, batch:batch + 1], mod_rows[:, :batch]], axis=1)
    mods_all = mods_all.reshape(depth, batch + 1, 6, d)

    cos_t, sin_t = _rope_tables(ctx_len, seq)

    for l in range(depth):
        last = l == depth - 1
        lam_init = 0.8 - 0.6 * math.exp(-0.3 * l)
        lam = (jnp.exp(jnp.sum(lambda_q1[l] * lambda_k1[l]).astype(F32))
               - jnp.exp(jnp.sum(lambda_q2[l] * lambda_k2[l]).astype(F32)) + lam_init)
        mods = mods_all[l]
        qg = jnp.tile(q_norm_g[l] * (A_DH ** -0.5 * math.log2(math.e)), 2).reshape(1, LANES)
        kg = jnp.tile(k_norm_g[l], 2).reshape(1, LANES)

        qk, av, mq, mkt, mv, mo, g_row = proj_in(
            x_ctx, x_lat, lat_first_tile, mods, norm1_g[l].reshape(1, d), w_in, l,
            conv_w[l], conv_b[l].reshape(1, -1), cos_t, sin_t, qg, kg, n_ctx_tiles, tiles_per_seq, m_dh ** -0.5)
        attn = functools.partial(attention, lam.reshape(1, 1), qk, av, subln_g[l].reshape(1, LANES),
                                 1.0 - lam_init, batch, ctx_len, seq)
        if l == 0:
            later = [w_out[0], w_out[1], ffn_w1[0], ffn_w3[0], moe_w1[0], moe_w3[0], moe_w2[0]]
            views = [w.reshape(-1, w.shape[-1]) for w in later]
            a_out, *later_bf16 = attn(latent_queries=True, to_bf16=views)
            w_out_bf16 = later_bf16[:2]
            ffn_bf16 = later_bf16[2:4] + [ffn_w2[0].astype(BF16)]
            experts_bf16 = [w.reshape(src.shape) for w, src in zip(later_bf16[4:], later[4:])]
        else:
            a_out = attn(latent_queries=True)
        m_ctx, m_out = mlstm(mq, mkt, mv, mo, g_row, gate_b[l].reshape(N_GATES, 1),
                             mnorm_g[l].reshape(1, -1), batch, ctx_len, seq)
        w_out_b = w_out_bf16[l]
        j = l // 2
        if not last:
            assert lat_first_tile == 0
            xs = mixer_ffn(a_out, attn(latent_queries=False), m_out, m_ctx, w_out_b, x_ctx, x_lat, mods,
                           norm2_g[l].reshape(1, d), *ffn_bf16, seq)
            x_ctx, x_lat, lat_first_tile = xs, xs, n_ctx_tiles
        else:
            x1, h2, sel, wcols, cnt = mixer_router(a_out, m_out, w_out_b, x_lat, lat_first_tile * ROW_TILE, mods,
                                                   norm2_g[l].reshape(1, d), router_w[j], seq)
            n_tok_tiles = x1.shape[0] // MOE_TOKEN_TILE
            counts = cnt[:, 0, :N_EXPERTS]
            worst = 2 * x1.shape[0] + n_tok_tiles * N_EXPERTS * (SLOT_ALIGN - 1) + N_EXPERTS * (MOE_ROW_TILE - SLOT_ALIGN)
            n_row_tiles = -(-worst // MOE_ROW_TILE)
            plan = _moe_plan(counts, n_row_tiles)
            slots = moe_dispatch(plan, h2, wcols, sel, n_row_tiles * MOE_ROW_TILE)
            ys = moe_experts(plan, slots, *experts_bf16)
            out = moe_combine(plan, ys, sel, x1, mods, seq)
    return out.reshape(batch, seq, d)
```
